```python
import jax
import jax.numpy as jnp
from jax import lax
import numpy as np

D_MODEL = 1024
BATCH = 8
SEQ = 2048
DEPTH = 2

GRID_W = 64
CTX_LEN = 256
CHUNK = 128
Q_BLOCK = 128
EPS = 1e-6
N_MOD = 6

SGU_HEADS = 4
SGU_HEAD_DIM = 64
SGU_WIDTH = SGU_HEADS * SGU_HEAD_DIM

SSD_HEADS = 6
SSD_HEAD_DIM = 64
SSD_WIDTH = SSD_HEADS * SSD_HEAD_DIM
SSD_GROUPS = 2
SSD_HPG = SSD_HEADS // SSD_GROUPS
SSD_STATE = 128
SSD_CONV = 5
SSD_BC = SSD_GROUPS * SSD_STATE
SSD_CONV_CH = SSD_WIDTH + 2 * SSD_BC

MLA_HEADS = 6
MLA_NOPE = 64
MLA_ROPE = 32
MLA_V = 64
MLA_WIDTH = MLA_HEADS * MLA_V
Q_LORA = 384
KV_LORA = 256
ROPE_THETA = 10000.0
MLA_SCALE = (MLA_NOPE + MLA_ROPE) ** -0.5

MIX_WIDTH = SGU_WIDTH + SSD_WIDTH + MLA_WIDTH
IN_SGU = 2 * SGU_WIDTH
IN_SSD = SSD_WIDTH + SSD_CONV_CH + 2 * SSD_HEADS
IN_MLA = Q_LORA + KV_LORA + MLA_ROPE
IN_WIDTH = IN_SGU + IN_SSD + IN_MLA

D_FF = 2816
N_EXPERTS = 8
TOP_K = 2
D_FF_EXPERT = 1408

kernel_name = "hybrid_sgu_ssd_mla_moe_dit"


def rmsnorm(x, g):
    xf = x.astype(jnp.float32)
    y = xf * lax.rsqrt(jnp.mean(xf * xf, axis=-1, keepdims=True) + EPS)
    return y.astype(x.dtype) * g


def layernorm(x, g, b):
    xf = x.astype(jnp.float32)
    mu = jnp.mean(xf, axis=-1, keepdims=True)
    xc = xf - mu
    y = xc * lax.rsqrt(jnp.mean(xc * xc, axis=-1, keepdims=True) + EPS)
    return y.astype(x.dtype) * g + b


def modulate(x, g, shift, scale):
    return rmsnorm(x, g) * (1 + scale) + shift


def axial_rope_tables(n_tokens):
    grid_rows = n_tokens // GRID_W
    row = jnp.repeat(jnp.arange(grid_rows, dtype=jnp.float32), GRID_W)
    col = jnp.tile(jnp.arange(GRID_W, dtype=jnp.float32), grid_rows)
    n_freq = MLA_ROPE // 4
    inv_freq = ROPE_THETA ** (-jnp.arange(n_freq, dtype=jnp.float32) / n_freq)
    ang = jnp.stack([row[:, None] * inv_freq, col[:, None] * inv_freq], axis=1)
    return jnp.cos(ang), jnp.sin(ang)


def apply_axial_rope(t, cos, sin):
    shp = t.shape
    t = t.reshape(shp[:-1] + (2, 2, MLA_ROPE // 4))
    t1, t2 = t[..., 0, :], t[..., 1, :]
    cos = cos.astype(t.dtype)
    sin = sin.astype(t.dtype)
    out = jnp.stack([t1 * cos - t2 * sin, t1 * sin + t2 * cos], axis=-2)
    return out.reshape(shp)


def sgu_mixer(p, w_s, b_s, g_n, beta_n):
    bsz, n, _ = p.shape
    u, v = jnp.split(jax.nn.gelu(p), 2, axis=-1)
    v = layernorm(v, g_n, beta_n).reshape(bsz, n // CHUNK, CHUNK, SGU_HEADS, SGU_HEAD_DIM)
    mixed = jnp.einsum('hts,bcshd->bcthd', w_s, v) + b_s.T[:, :, None]
    return u * mixed.reshape(bsz, n, SGU_WIDTH)


def depthwise_conv(x, w, b):
    pad = SSD_CONV // 2
    y = lax.conv_general_dilated(x, w[:, None, :], window_strides=(1,), padding=[(pad, pad)],
                                 dimension_numbers=('NWC', 'WIO', 'NWC'), feature_group_count=x.shape[-1])
    return y + b


def ssd_streams(p, conv_w, conv_b, dt_bias):
    bsz, n, _ = p.shape
    z = p[..., :SSD_WIDTH]
    xbc = jax.nn.silu(depthwise_conv(p[..., SSD_WIDTH:SSD_WIDTH + SSD_CONV_CH], conv_w, conv_b))
    dt = jax.nn.softplus(p[..., SSD_WIDTH + SSD_CONV_CH:] + dt_bias.reshape(-1))
    xs = xbc[..., :SSD_WIDTH].reshape(bsz, n, SSD_GROUPS, SSD_HPG, SSD_HEAD_DIM)
    bm = xbc[..., SSD_WIDTH:SSD_WIDTH + SSD_BC].reshape(bsz, n, SSD_GROUPS, SSD_STATE)
    cm = xbc[..., SSD_WIDTH + SSD_BC:].reshape(bsz, n, SSD_GROUPS, SSD_STATE)
    dt = dt.reshape(bsz, n, 2, SSD_GROUPS, SSD_HPG)
    return z, xs, bm, cm, dt


def ssd_final_state(x, dt, a, bm):
    acs = jnp.cumsum(dt.astype(jnp.float32) * a, axis=1)
    w = jnp.exp(acs[:, -1:] - acs).astype(x.dtype) * dt
    return jnp.einsum('blgn,blgj,blgjp->bgjpn', bm, w, x)


def ssd_chunked(x, dt, a, bm, cm, h0):
    bsz, n, g, j, p = x.shape
    nc = n // CHUNK
    xc = x.reshape(bsz, nc, CHUNK, g, j, p)
    dtc = dt.reshape(bsz, nc, CHUNK, g, j)
    bc = bm.reshape(bsz, nc, CHUNK, g, SSD_STATE)
    cc = cm.reshape(bsz, nc, CHUNK, g, SSD_STATE)
    acs = jnp.cumsum(dtc.astype(jnp.float32) * a, axis=2)
    xdt = xc * dtc[..., None]
    seg = acs[:, :, :, None] - acs[:, :, None, :]
    lower = jnp.tril(jnp.ones((CHUNK, CHUNK), dtype=bool))[:, :, None, None]
    decay = jnp.exp(jnp.where(lower, seg, -jnp.inf)).astype(x.dtype)
    cb = jnp.einsum('bclgn,bcsgn->bclsg', cc, bc)
    y_diag = jnp.einsum('bclsgj,bcsgjp->bclgjp', cb[..., None] * decay, xdt)
    to_end = jnp.exp(acs[:, :, -1:] - acs).astype(x.dtype)
    states = jnp.einsum('bcsgn,bcsgj,bcsgjp->bcgjpn', bc, to_end, xdt)
    chunk_decay = jnp.exp(acs[:, :, -1]).astype(x.dtype)

    def carry_step(h, inp):
        dec, st = inp
        return dec[..., None, None] * h + st, h

    _, h_in = lax.scan(carry_step, h0, (jnp.moveaxis(chunk_decay, 1, 0), jnp.moveaxis(states, 1, 0)))
    h_in = jnp.moveaxis(h_in, 0, 1)
    y_off = jnp.einsum('bclgn,bcgjpn,bclgj->bclgjp', cc, h_in, jnp.exp(acs).astype(x.dtype))
    return (y_diag + y_off).reshape(bsz, n, g, j, p)


def ssd_mixer(p, pc, conv_w, conv_b, dt_bias, a_log, d_skip, g_ssd, need_ctx):
    z, xs, bm, cm, dt = ssd_streams(p, conv_w, conv_b, dt_bias)
    zc, xsc, bmc, cmc, dtc = ssd_streams(pc, conv_w, conv_b, dt_bias)
    a = -jnp.exp(a_log.astype(jnp.float32)).reshape(2, SSD_GROUPS, SSD_HPG)
    skip = d_skip.reshape(SSD_GROUPS, SSD_HPG, 1)
    bsz, n = xs.shape[:2]
    y = skip * xs
    yc = skip * xsc
    for direction in range(2):
        order = (lambda t: t) if direction == 0 else (lambda t: jnp.flip(t, axis=1))
        h_ctx = ssd_final_state(order(xsc), order(dtc[:, :, direction]), a[direction], order(bmc))
        y = y + order(ssd_chunked(order(xs), order(dt[:, :, direction]), a[direction],
                                  order(bm), order(cm), h_ctx))
        if need_ctx:
            yc = yc + order(ssd_chunked(order(xsc), order(dtc[:, :, direction]), a[direction],
                                        order(bmc), order(cmc), jnp.zeros_like(h_ctx)))
    out = rmsnorm(y.reshape(bsz, n, SSD_WIDTH) * jax.nn.silu(z), g_ssd)
    if not need_ctx:
        return out, None
    outc = rmsnorm(yc.reshape(bsz, pc.shape[1], SSD_WIDTH) * jax.nn.silu(zc), g_ssd)
    return out, outc


def mla_queries(p, g_q, w_uq):
    bsz, n, _ = p.shape
    q = (rmsnorm(p[..., :Q_LORA], g_q) @ w_uq).reshape(bsz, n, MLA_HEADS, MLA_NOPE + MLA_ROPE)
    return q[..., :MLA_NOPE], q[..., MLA_NOPE:]


def mla_keys_values(p, g_kv, w_ukv):
    bsz, n, _ = p.shape
    kv = (rmsnorm(p[..., Q_LORA:Q_LORA + KV_LORA], g_kv) @ w_ukv).reshape(bsz, n, MLA_HEADS, MLA_NOPE + MLA_V)
    k_rope = p[..., Q_LORA + KV_LORA:]
    return kv[..., :MLA_NOPE], k_rope, kv[..., MLA_NOPE:]


def mla_attend(qn, qr, kn, kr, v):
    s = jnp.einsum('bqhd,bkhd->bhqk', qn, kn) + jnp.einsum('bqhr,bkr->bhqk', qr, kr)
    prob = jax.nn.softmax(s.astype(jnp.float32) * MLA_SCALE, axis=-1).astype(v.dtype)
    return jnp.einsum('bhqk,bkhd->bqhd', prob, v)


def mla_mixer(p, pc, g_q, w_uq, g_kv, w_ukv, cos, sin, need_ctx):
    bsz, n, _ = p.shape
    qn, qr = mla_queries(p, g_q, w_uq)
    qr = apply_axial_rope(qr, cos[:, None], sin[:, None])
    kn, kr, v = mla_keys_values(p, g_kv, w_ukv)
    kr = apply_axial_rope(kr, cos, sin)
    knc, krc, vc = mla_keys_values(pc, g_kv, w_ukv)
    kn_all = jnp.concatenate([knc, kn], axis=1)
    kr_all = jnp.concatenate([krc, kr], axis=1)
    v_all = jnp.concatenate([vc, v], axis=1)
    n_blk = n // Q_BLOCK

    def to_blocks(t):
        return jnp.moveaxis(t.reshape((bsz, n_blk, Q_BLOCK) + t.shape[2:]), 1, 0)

    o = lax.map(lambda qb: mla_attend(qb[0], qb[1], kn_all, kr_all, v_all), (to_blocks(qn), to_blocks(qr)))
    o = jnp.moveaxis(o, 0, 1).reshape(bsz, n, MLA_WIDTH)
    if not need_ctx:
        return o, None
    qnc, qrc = mla_queries(pc, g_q, w_uq)
    oc = mla_attend(qnc, qrc, knc, krc, vc).reshape(bsz, pc.shape[1], MLA_WIDTH)
    return o, oc


def swiglu(h, wg, wu, wd):
    return (jax.nn.silu(h @ wg) * (h @ wu)) @ wd


def moe_ffn(h, w_router, wg, wu, wd):
    logits = (h @ w_router).astype(jnp.float32)
    top_logit, top_idx = lax.top_k(logits, TOP_K)
    top_w = jax.nn.softmax(top_logit, axis=-1).astype(h.dtype)
    out = jnp.zeros_like(h)
    for e in range(N_EXPERTS):
        gate_e = jnp.sum(jnp.where(top_idx == e, top_w, 0), axis=-1, keepdims=True)
        out = out + gate_e * swiglu(h, wg[e], wu[e], wd[e])
    return out


def setup_inputs(seed: int = 0) -> dict:
    key = jax.random.key(seed)
    ks = iter(jax.random.split(key, 40))
    L = DEPTH
    n_dense = (DEPTH + 1) // 2
    n_moe = DEPTH // 2

    def nrm(shape, scale):
        return jax.random.normal(next(ks), shape, jnp.float32) * scale

    def gain(shape):
        return 1.0 + nrm(shape, 0.02)

    x = nrm((BATCH, SEQ, D_MODEL), 1.0)
    c = nrm((BATCH, D_MODEL), 1.0)
    ctx = nrm((BATCH, CTX_LEN, D_MODEL), 1.0)
    c_ctx = nrm((D_MODEL,), 1.0)
    w_mod = nrm((L, D_MODEL, N_MOD * D_MODEL), 0.5 * D_MODEL ** -0.5)
    b_mod = nrm((L, N_MOD * D_MODEL), 0.02)
    g_mix = gain((L, D_MODEL))
    w_in = nrm((L, D_MODEL, IN_WIDTH), D_MODEL ** -0.5)
    w_sgu = nrm((L, SGU_HEADS, CHUNK, CHUNK), CHUNK ** -0.5)
    b_sgu = nrm((L, SGU_HEADS, CHUNK), 0.02)
    g_sgu = gain((L, SGU_WIDTH))
    beta_sgu = nrm((L, SGU_WIDTH), 0.02)
    conv_w = nrm((L, SSD_CONV, SSD_CONV_CH), SSD_CONV ** -0.5)
    conv_b = nrm((L, SSD_CONV_CH), 0.02)
    dt0 = jnp.exp(jax.random.uniform(next(ks), (L, 2, SSD_HEADS), jnp.float32, np.log(1e-3), np.log(1e-1)))
    dt_bias = dt0 + jnp.log(-jnp.expm1(-dt0))
    a_log = jnp.log(jax.random.uniform(next(ks), (L, 2, SSD_HEADS), jnp.float32, 1.0, 16.0))
    d_skip = gain((L, SSD_HEADS))
    g_ssd = gain((L, SSD_WIDTH))
    g_q = gain((L, Q_LORA))
    w_uq = nrm((L, Q_LORA, MLA_HEADS * (MLA_NOPE + MLA_ROPE)), Q_LORA ** -0.5)
    g_kv = gain((L, KV_LORA))
    w_ukv = nrm((L, KV_LORA, MLA_HEADS * (MLA_NOPE + MLA_V)), KV_LORA ** -0.5)
    w_out = nrm((L, MIX_WIDTH, D_MODEL), MIX_WIDTH ** -0.5)
    g_ffn = gain((L, D_MODEL))
    w_gate = nrm((n_dense, D_MODEL, D_FF), D_MODEL ** -0.5)
    w_up = nrm((n_dense, D_MODEL, D_FF), D_MODEL ** -0.5)
    w_down = nrm((n_dense, D_FF, D_MODEL), D_FF ** -0.5)
    w_router = nrm((n_moe, D_MODEL, N_EXPERTS), D_MODEL ** -0.5)
    w_gate_e = nrm((n_moe, N_EXPERTS, D_MODEL, D_FF_EXPERT), D_MODEL ** -0.5)
    w_up_e = nrm((n_moe, N_EXPERTS, D_MODEL, D_FF_EXPERT), D_MODEL ** -0.5)
    w_down_e = nrm((n_moe, N_EXPERTS, D_FF_EXPERT, D_MODEL), D_FF_EXPERT ** -0.5)
    g_final = gain((D_MODEL,))
    return {"x": x, "c": c, "ctx": ctx, "c_ctx": c_ctx, "w_mod": w_mod, "b_mod": b_mod, "g_mix": g_mix,
            "w_in": w_in, "w_sgu": w_sgu, "b_sgu": b_sgu, "g_sgu": g_sgu, "beta_sgu": beta_sgu,
            "conv_w": conv_w, "conv_b": conv_b, "dt_bias": dt_bias, "a_log": a_log, "d_skip": d_skip,
            "g_ssd": g_ssd, "g_q": g_q, "w_uq": w_uq, "g_kv": g_kv, "w_ukv": w_ukv, "w_out": w_out,
            "g_ffn": g_ffn, "w_gate": w_gate, "w_up": w_up, "w_down": w_down, "w_router": w_router,
            "w_gate_e": w_gate_e, "w_up_e": w_up_e, "w_down_e": w_down_e, "g_final": g_final}


def reference(x, c, ctx, c_ctx, w_mod, b_mod, g_mix, w_in, w_sgu, b_sgu, g_sgu, beta_sgu, conv_w, conv_b,
              dt_bias, a_log, d_skip, g_ssd, g_q, w_uq, g_kv, w_ukv, w_out, g_ffn, w_gate, w_up, w_down,
              w_router, w_gate_e, w_up_e, w_down_e, g_final):
    bsz, n, d = x.shape
    cos, sin = axial_rope_tables(n)
    cond = jax.nn.silu(c)
    cond_ctx = jax.nn.silu(c_ctx)
    xc = ctx
    for layer in range(DEPTH):
        need_ctx = layer < DEPTH - 1
        mod = (cond @ w_mod[layer] + b_mod[layer]).reshape(bsz, 1, N_MOD, d)
        modc = (cond_ctx @ w_mod[layer] + b_mod[layer]).reshape(N_MOD, d)
        h = modulate(x, g_mix[layer], mod[:, :, 0], mod[:, :, 1])
        hc = modulate(xc, g_mix[layer], modc[0], modc[1])
        p = h @ w_in[layer]
        pc = hc @ w_in[layer]
        s0, s1 = IN_SGU, IN_SGU + IN_SSD
        o_sgu = sgu_mixer(p[..., :s0], w_sgu[layer], b_sgu[layer], g_sgu[layer], beta_sgu[layer])
        o_ssd, oc_ssd = ssd_mixer(p[..., s0:s1], pc[..., s0:s1], conv_w[layer], conv_b[layer], dt_bias[layer],
                                  a_log[layer], d_skip[layer], g_ssd[layer], need_ctx)
        o_mla, oc_mla = mla_mixer(p[..., s1:], pc[..., s1:], g_q[layer], w_uq[layer], g_kv[layer],
                                  w_ukv[layer], cos, sin, need_ctx)
        x = x + mod[:, :, 2] * (jnp.concatenate([o_sgu, o_ssd, o_mla], axis=-1) @ w_out[layer])
        if need_ctx:
            oc_sgu = sgu_mixer(pc[..., :s0], w_sgu[layer], b_sgu[layer], g_sgu[layer], beta_sgu[layer])
            xc = xc + modc[2] * (jnp.concatenate([oc_sgu, oc_ssd, oc_mla], axis=-1) @ w_out[layer])
        if layer % 2 == 0:
            i = layer // 2
            ffn = lambda t: swiglu(t, w_gate[i], w_up[i], w_down[i])
        else:
            i = layer // 2
            ffn = lambda t: moe_ffn(t, w_router[i], w_gate_e[i], w_up_e[i], w_down_e[i])
        x = x + mod[:, :, 5] * ffn(modulate(x, g_ffn[layer], mod[:, :, 3], mod[:, :, 4]))
        if need_ctx:
            xc = xc + modc[5] * ffn(modulate(xc, g_ffn[layer], modc[3], modc[4]))
    return rmsnorm(x, g_final)
```

```python
import functools

import jax
import jax.numpy as jnp
from jax import lax
from jax.experimental import pallas as pl
from jax.experimental.pallas import tpu as pltpu

F32 = jnp.float32
BF16 = jnp.bfloat16
HIGHEST = lax.Precision.HIGHEST

D_MODEL = 1024
EPS = 1e-6
N_MOD = 6
GRID_W = 64
CHUNK = 128

SGU_HEADS = 4
SGU_HEAD_DIM = 64
SGU_WIDTH = 256

SSD_HEADS = 6
SSD_HEAD_DIM = 64
SSD_WIDTH = 384
SSD_GROUPS = 2
SSD_HPG = 3
SSD_STATE = 128
SSD_CONV = 5
SSD_CONV_CH = 896
SSD_GW = SSD_HPG * SSD_HEAD_DIM

MLA_HEADS = 6
MLA_NOPE = 64
MLA_ROPE = 32
MLA_V = 64
MLA_WIDTH = 384
Q_LORA = 384
KV_LORA = 256
ROPE_THETA = 10000.0
MLA_SCALE = (MLA_NOPE + MLA_ROPE) ** -0.5
HEAD_SLOT = 128
ROPE_LANE0 = MLA_NOPE

D_FF = 2816
N_EXPERTS = 8
D_FF_EXPERT = 1408

LANES = 128
MOD_ROWS = 16
CTX_MOD_ROW = 8

IN_COLS = (("sgu", 0, 512), ("z", 512, 896), ("xbc", 896, 1792), ("dt", 1792, 1920), ("cq", 1920, 2304),
           ("ckv", 2304, 2560), ("kr", 2560, 2688))
IN_PAD_WIDTH = 2688

VMEM_LIMIT = 56 * 1024 * 1024


def _sigmoid(x):
    return 1.0 / (1.0 + jnp.exp(-x))


def _silu(x):
    return x * _sigmoid(x)


def _rms(x):
    return x * lax.rsqrt(jnp.mean(x * x, axis=-1, keepdims=True) + EPS)


def _aligned(v, m):
    return v if isinstance(v, int) else pl.multiple_of(v, m)


def _row_select(mod_ref, b, tile, rows, ctx_len, col0):
    row = tile * rows + lax.broadcasted_iota(jnp.int32, (rows, 1), 0)
    is_ctx = row < ctx_len
    mb = mod_ref[pl.ds(b, 1), col0:col0 + D_MODEL]
    mc = mod_ref[CTX_MOD_ROW:CTX_MOD_ROW + 1, col0:col0 + D_MODEL]
    return jnp.where(is_ctx, mc, mb)


def _mod_kernel(cond_ref, w_ref, b_ref, o_ref):
    s = _silu(cond_ref[...])
    o_ref[0] = jnp.dot(s, w_ref[0], precision=HIGHEST, preferred_element_type=F32) + b_ref[0]


def _mod_table(cond, w_mod, b_mod):
    n_layers, d, width = w_mod.shape
    cb = 1536
    return pl.pallas_call(
        _mod_kernel,
        out_shape=jax.ShapeDtypeStruct((n_layers, MOD_ROWS, width), F32),
        grid=(n_layers, width // cb),
        in_specs=[pl.BlockSpec((MOD_ROWS, d), lambda l, j: (0, 0)),
                  pl.BlockSpec((1, d, cb), lambda l, j: (l, 0, j)),
                  pl.BlockSpec((1, 1, cb), lambda l, j: (l, 0, j))],
        out_specs=pl.BlockSpec((1, MOD_ROWS, cb), lambda l, j: (l, 0, j)),
        compiler_params=pltpu.CompilerParams(dimension_semantics=("arbitrary", "arbitrary"),
                                             vmem_limit_bytes=VMEM_LIMIT),
        name="mod_table",
    )(cond, w_mod, b_mod.reshape(n_layers, 1, width))


def _in_kernel(x_ref, mod_ref, g_ref, w_ref, *out_refs, rows, ctx_len):
    b = pl.program_id(0)
    t = pl.program_id(1)
    xn = _rms(x_ref[0]) * g_ref[...]
    shift = _row_select(mod_ref, b, t, rows, ctx_len, 0)
    scale = _row_select(mod_ref, b, t, rows, ctx_len, D_MODEL)
    h = (xn * (1.0 + scale) + shift).astype(BF16)
    for o_ref, (_, c0, c1) in zip(out_refs, IN_COLS):
        o_ref[0] = jnp.dot(h, w_ref[:, c0:c1], preferred_element_type=F32)


def _in_proj(xa, mod, g, w_pad, ctx_len, rows):
    bsz, t_all, d = xa.shape
    kern = functools.partial(_in_kernel, rows=rows, ctx_len=ctx_len)
    return pl.pallas_call(
        kern,
        out_shape=[jax.ShapeDtypeStruct((bsz, t_all, c1 - c0), F32) for _, c0, c1 in IN_COLS],
        grid=(bsz, t_all // rows),
        in_specs=[pl.BlockSpec((1, rows, d), lambda b, t: (b, t, 0)),
                  pl.BlockSpec(mod.shape, lambda b, t: (0, 0)),
                  pl.BlockSpec((1, d), lambda b, t: (0, 0)),
                  pl.BlockSpec(w_pad.shape, lambda b, t: (0, 0))],
        out_specs=[pl.BlockSpec((1, rows, c1 - c0), lambda b, t: (b, t, 0)) for _, c0, c1 in IN_COLS],
        compiler_params=pltpu.CompilerParams(dimension_semantics=("arbitrary", "arbitrary"),
                                             vmem_limit_bytes=VMEM_LIMIT),
        name="in_proj",
    )(xa, mod, g, w_pad)


def _sgu_kernel(p_ref, w_ref, bias_ref, g_ref, beta_ref, o_ref, *, n_chunks):
    lane = lax.broadcasted_iota(jnp.int32, (1, SGU_WIDTH), 1)
    head_of_lane = lane // SGU_HEAD_DIM
    w = w_ref[...]
    bias = bias_ref[...]
    c0 = 0.7978845608028654
    for c in range(n_chunks):
        p = p_ref[0, c * CHUNK:(c + 1) * CHUNK, :]
        ge = 0.5 * p * (1.0 + jnp.tanh(c0 * (p + 0.044715 * (p * p * p))))
        u = ge[:, :SGU_WIDTH]
        v = ge[:, SGU_WIDTH:]
        mu = jnp.mean(v, axis=-1, keepdims=True)
        vc = v - mu
        vn = vc * lax.rsqrt(jnp.mean(vc * vc, axis=-1, keepdims=True) + EPS) * g_ref[...] + beta_ref[...]
        stacked = jnp.concatenate(
            [jnp.where(head_of_lane == h, vn, 0.0).astype(BF16) for h in range(SGU_HEADS)], axis=0)
        mixed = jnp.dot(w, stacked, preferred_element_type=F32) + bias
        o_ref[0, c * CHUNK:(c + 1) * CHUNK, :] = (u * mixed).astype(BF16)


def _sgu(p_sgu, w_cat, bias, g, beta, rows):
    bsz, t_all, width = p_sgu.shape
    kern = functools.partial(_sgu_kernel, n_chunks=rows // CHUNK)
    const = lambda b, t: (0, 0)
    return pl.pallas_call(
        kern,
        out_shape=jax.ShapeDtypeStruct((bsz, t_all, SGU_WIDTH), BF16),
        grid=(bsz, t_all // rows),
        in_specs=[pl.BlockSpec((1, rows, width), lambda b, t: (b, t, 0)),
                  pl.BlockSpec(w_cat.shape, const), pl.BlockSpec(bias.shape, const),
                  pl.BlockSpec(g.shape, const), pl.BlockSpec(beta.shape, const)],
        out_specs=pl.BlockSpec((1, rows, SGU_WIDTH), lambda b, t: (b, t, 0)),
        compiler_params=pltpu.CompilerParams(dimension_semantics=("arbitrary", "arbitrary"),
                                             vmem_limit_bytes=VMEM_LIMIT),
        name="sgu",
    )(p_sgu, w_cat, bias, g, beta)


def _ssd_kernel(z_ref, xbc_ref, dt_ref, cw_ref, cb_ref, dtb_ref, alog_ref, skip_ref, g_ref, o_ref,
                xc_ref, yacc_ref, st_ref, *, n_blk, n_ctx_blk):
    def conv_block(blk, seg_start, seg_end):
        r0 = _aligned(blk * CHUNK, CHUNK)
        for cg in range(SSD_CONV_CH // LANES):
            cols = slice(cg * LANES, (cg + 1) * LANES)
            zeros = jnp.zeros((8, LANES), F32)
            top = zeros if seg_start else xbc_ref[0, pl.ds(_aligned(r0 - 8, 8), 8), cols]
            bot = zeros if seg_end else xbc_ref[0, pl.ds(_aligned(r0 + CHUNK, 8), 8), cols]
            xw = jnp.concatenate([top, xbc_ref[0, pl.ds(r0, CHUNK), cols], bot], axis=0)
            acc = cb_ref[:, cols] + cw_ref[0:1, cols] * xw[6:6 + CHUNK]
            for k in range(1, SSD_CONV):
                acc = acc + cw_ref[k:k + 1, cols] * xw[6 + k:6 + k + CHUNK]
            xc_ref[pl.ds(r0, CHUNK), cols] = _silu(acc)

    static_blocks = sorted(set(list(range(n_ctx_blk)) + [n_ctx_blk, n_blk - 1]))
    for blk in static_blocks:
        conv_block(blk, blk == 0 or blk == n_ctx_blk, blk == n_ctx_blk - 1 or blk == n_blk - 1)
    if n_blk - 1 > n_ctx_blk + 1:
        def conv_body(blk, carry):
            conv_block(blk, False, False)
            return carry
        lax.fori_loop(n_ctx_blk + 1, n_blk - 1, conv_body, 0)

    li = lax.broadcasted_iota(jnp.int32, (CHUNK, CHUNK), 0)
    si = lax.broadcasted_iota(jnp.int32, (CHUNK, CHUNK), 1)
    a_neg = -jnp.exp(alog_ref[...])

    def chunk_step(c, direction):
        r0 = pl.multiple_of(c * CHUNK, CHUNK)
        rows = pl.ds(r0, CHUNK)
        mask = (si <= li) if direction == 0 else (si >= li)
        tri = jnp.where(mask, 1.0, 0.0).astype(F32)
        dtr = dt_ref[0, rows, :] + dtb_ref[...]
        dt = jnp.maximum(dtr, 0.0) + jnp.log1p(jnp.exp(-jnp.abs(dtr)))
        acs = jnp.dot(tri, dt * a_neg, precision=HIGHEST, preferred_element_type=F32)
        acs_t = acs.T
        end = CHUNK - 1 if direction == 0 else 0
        tot = acs[end:end + 1, :]
        to_end_dt = jnp.exp(tot - acs) * dt
        exp_acs = jnp.exp(acs)
        chunk_decay = jnp.exp(tot)
        ys = []
        for g in range(SSD_GROUPS):
            bm = xc_ref[rows, SSD_WIDTH + g * SSD_STATE:SSD_WIDTH + (g + 1) * SSD_STATE]
            cm = xc_ref[rows, SSD_WIDTH + SSD_GROUPS * SSD_STATE + g * SSD_STATE:
                        SSD_WIDTH + SSD_GROUPS * SSD_STATE + (g + 1) * SSD_STATE]
            cm16 = cm.astype(BF16)
            cb = lax.dot_general(cm16, bm.astype(BF16), (((1,), (1,)), ((), ())), preferred_element_type=F32)
            bm_t = bm.T.astype(BF16)
            state = st_ref[g]
            y_off = jnp.dot(cm16, state.astype(BF16), preferred_element_type=F32)
            w_cols, dec_cols = [], []
            for j in range(SSD_HPG):
                hh = g * SSD_HPG + j
                col = direction * SSD_HEADS + hh
                xs = xc_ref[rows, hh * SSD_HEAD_DIM:(hh + 1) * SSD_HEAD_DIM]
                seg = acs[:, col:col + 1] - acs_t[col:col + 1, :]
                decay = jnp.exp(jnp.where(mask, seg, -1e30))
                xdt = (xs * dt[:, col:col + 1]).astype(BF16)
                y_diag = jnp.dot((cb * decay).astype(BF16), xdt, preferred_element_type=F32)
                ys.append(y_diag + exp_acs[:, col:col + 1] * y_off[:, j * SSD_HEAD_DIM:(j + 1) * SSD_HEAD_DIM])
                w_cols.append((xs * to_end_dt[:, col:col + 1]).astype(BF16))
                dec_cols.append(jnp.broadcast_to(chunk_decay[:, col:col + 1], (SSD_STATE, SSD_HEAD_DIM)))
            contrib = jnp.dot(bm_t, jnp.concatenate(w_cols, axis=1), preferred_element_type=F32)
            st_ref[g] = jnp.concatenate(dec_cols, axis=1) * state + contrib
        y = jnp.concatenate(ys, axis=1)
        if direction == 0:
            yacc_ref[rows, :] = skip_ref[...] * xc_ref[rows, 0:SSD_WIDTH] + y
        else:
            gated = (yacc_ref[rows, :] + y) * _silu(z_ref[0, rows, :])
            o_ref[0, rows, :] = (_rms(gated) * g_ref[...]).astype(BF16)

    def fwd_body(step, carry):
        chunk_step(step, 0)
        return carry

    def bwd_body(step, carry):
        c = jnp.where(step < n_ctx_blk, n_ctx_blk - 1 - step, n_blk - 1 - (step - n_ctx_blk))
        chunk_step(c, 1)
        return carry

    st_ref[...] = jnp.zeros(st_ref.shape, F32)
    lax.fori_loop(0, n_blk, fwd_body, 0)
    st_ref[...] = jnp.zeros(st_ref.shape, F32)
    lax.fori_loop(0, n_blk, bwd_body, 0)


def _ssd(p_z, p_xbc, p_dt, conv_w, conv_b, dt_bias, a_log, skip, g, ctx_len):
    bsz, t_all, _ = p_z.shape
    kern = functools.partial(_ssd_kernel, n_blk=t_all // CHUNK, n_ctx_blk=ctx_len // CHUNK)
    const = lambda b: (0, 0)
    per_b = lambda w: pl.BlockSpec((1, t_all, w), lambda b: (b, 0, 0))
    return pl.pallas_call(
        kern,
        out_shape=jax.ShapeDtypeStruct((bsz, t_all, SSD_WIDTH), BF16),
        grid=(bsz,),
        in_specs=[per_b(SSD_WIDTH), per_b(SSD_CONV_CH), per_b(LANES),
                  pl.BlockSpec(conv_w.shape, const), pl.BlockSpec(conv_b.shape, const),
                  pl.BlockSpec(dt_bias.shape, const), pl.BlockSpec(a_log.shape, const),
                  pl.BlockSpec(skip.shape, const), pl.BlockSpec(g.shape, const)],
        out_specs=per_b(SSD_WIDTH),
        scratch_shapes=[pltpu.VMEM((t_all, SSD_CONV_CH), F32), pltpu.VMEM((t_all, SSD_WIDTH), F32),
                        pltpu.VMEM((SSD_GROUPS, SSD_STATE, SSD_GW), F32)],
        compiler_params=pltpu.CompilerParams(dimension_semantics=("arbitrary",), vmem_limit_bytes=VMEM_LIMIT),
        name="ssd",
    )(p_z, p_xbc, p_dt, conv_w, conv_b, dt_bias, a_log, skip, g)


def _mla_proj_kernel(pq_ref, pkv_ref, pkr_ref, gq_ref, wq_ref, gkv_ref, wk_ref, wv_ref, cos_ref, sa_ref, sb_ref,
                     q_ref, k_ref, v_ref):
    cos = cos_ref[...]
    sin_a = sa_ref[...]
    sin_b = sb_ref[...]

    def rope(t):
        return t * cos + pltpu.roll(t, 8, 1) * sin_a + pltpu.roll(t, LANES - 8, 1) * sin_b

    qn = (_rms(pq_ref[0]) * gq_ref[...]).astype(BF16)
    q = jnp.dot(qn, wq_ref[...], preferred_element_type=F32)
    kvn = (_rms(pkv_ref[0]) * gkv_ref[...]).astype(BF16)
    k = jnp.dot(kvn, wk_ref[...], preferred_element_type=F32)
    v_ref[0] = jnp.dot(kvn, wv_ref[...], preferred_element_type=F32).astype(BF16)
    kr = rope(pkr_ref[0])
    for h in range(MLA_HEADS):
        slot = slice(h * HEAD_SLOT, (h + 1) * HEAD_SLOT)
        q_ref[0, :, slot] = (rope(q[:, slot]) * MLA_SCALE).astype(BF16)
        k_ref[0, :, slot] = (k[:, slot] + kr).astype(BF16)


def _mla_proj(p_q, p_kv, p_kr, g_q, w_q, g_kv, w_k, w_v, cos, sin_a, sin_b, rows):
    bsz, t_all, _ = p_q.shape
    width = MLA_HEADS * HEAD_SLOT
    const = lambda b, t: (0, 0)
    tok = lambda w: pl.BlockSpec((1, rows, w), lambda b, t: (b, t, 0))
    tab = pl.BlockSpec((rows, LANES), lambda b, t: (t, 0))
    out = jax.ShapeDtypeStruct((bsz, t_all, width), BF16)
    return pl.pallas_call(
        _mla_proj_kernel,
        out_shape=[out, out, out],
        grid=(bsz, t_all // rows),
        in_specs=[tok(Q_LORA), tok(KV_LORA), tok(LANES),
                  pl.BlockSpec(g_q.shape, const), pl.BlockSpec(w_q.shape, const),
                  pl.BlockSpec(g_kv.shape, const), pl.BlockSpec(w_k.shape, const), pl.BlockSpec(w_v.shape, const),
                  tab, tab, tab],
        out_specs=[tok(width), tok(width), tok(width)],
        compiler_params=pltpu.CompilerParams(dimension_semantics=("arbitrary", "arbitrary"),
                                             vmem_limit_bytes=VMEM_LIMIT),
        name="mla_proj",
    )(p_q, p_kv, p_kr, g_q, w_q, g_kv, w_k, w_v, cos, sin_a, sin_b)


def _attn_kernel(q_ref, k_ref, v_ref, o_ref, *, t_all, ctx_len, ctx_block):
    def run(n_keys):
        outs = []
        for h in range(MLA_HEADS):
            slot = slice(h * HEAD_SLOT, (h + 1) * HEAD_SLOT)
            s = lax.dot_general(q_ref[0, :, slot], k_ref[0, 0:n_keys, slot], (((1,), (1,)), ((), ())),
                                preferred_element_type=F32)
            p = jnp.exp(s - jnp.max(s, axis=-1, keepdims=True))
            denom = jnp.sum(p, axis=-1, keepdims=True)
            o = jnp.dot(p.astype(BF16), v_ref[0, 0:n_keys, slot], preferred_element_type=F32)
            outs.append(o[:, :MLA_V] / denom)
        o_ref[0] = jnp.concatenate(outs, axis=1).astype(BF16)

    if ctx_block:
        is_ctx = pl.program_id(1) == 0
        pl.when(is_ctx)(lambda: run(ctx_len))
        pl.when(jnp.logical_not(is_ctx))(lambda: run(t_all))
    else:
        run(t_all)


def _attention(q, k, v, ctx_len, need_ctx):
    bsz, t_all, width = q.shape
    tq = ctx_len
    first = 0 if need_ctx else 1
    kern = functools.partial(_attn_kernel, t_all=t_all, ctx_len=ctx_len, ctx_block=need_ctx)
    kv_spec = pl.BlockSpec((1, t_all, width), lambda b, i: (b, 0, 0))
    return pl.pallas_call(
        kern,
        out_shape=jax.ShapeDtypeStruct((bsz, t_all, MLA_WIDTH), BF16),
        grid=(bsz, t_all // tq - first),
        in_specs=[pl.BlockSpec((1, tq, width), lambda b, i: (b, i + first, 0)), kv_spec, kv_spec],
        out_specs=pl.BlockSpec((1, tq, MLA_WIDTH), lambda b, i: (b, i + first, 0)),
        compiler_params=pltpu.CompilerParams(dimension_semantics=("arbitrary", "arbitrary"),
                                             vmem_limit_bytes=VMEM_LIMIT),
        name="attention",
    )(q, k, v)


def _out_kernel(x_ref, o1_ref, o2_ref, o3_ref, mod_ref, w1_ref, w2_ref, w3_ref, g_ref, *rest,
                rows, ctx_len, tile0, route):
    if route:
        wr_ref, xo_ref, h_ref, gate_ref = rest
    else:
        xo_ref, h_ref = rest
    b = pl.program_id(0)
    t = pl.program_id(1) + tile0
    mix = (jnp.dot(o1_ref[0], w1_ref[...], preferred_element_type=F32)
           + jnp.dot(o2_ref[0], w2_ref[...], preferred_element_type=F32)
           + jnp.dot(o3_ref[0], w3_ref[...], preferred_element_type=F32))
    x = x_ref[0] + _row_select(mod_ref, b, t, rows, ctx_len, 2 * D_MODEL) * mix
    xo_ref[0] = x
    shift = _row_select(mod_ref, b, t, rows, ctx_len, 3 * D_MODEL)
    scale = _row_select(mod_ref, b, t, rows, ctx_len, 4 * D_MODEL)
    h = _rms(x) * g_ref[...] * (1.0 + scale) + shift
    h_ref[0] = h.astype(BF16)
    if route:
        logits = jnp.dot(h, wr_ref[...], precision=HIGHEST, preferred_element_type=F32)
        lane = lax.broadcasted_iota(jnp.int32, logits.shape, 1)
        lane_f = lane.astype(F32)
        lg = jnp.where(lane < N_EXPERTS, logits, -jnp.inf)
        m1 = jnp.max(lg, axis=-1, keepdims=True)
        i1 = jnp.min(jnp.where(lg == m1, lane_f, float(LANES)), axis=-1, keepdims=True)
        lg2 = jnp.where(lane_f == i1, -jnp.inf, lg)
        m2 = jnp.max(lg2, axis=-1, keepdims=True)
        i2 = jnp.min(jnp.where(lg2 == m2, lane_f, float(LANES)), axis=-1, keepdims=True)
        e2 = jnp.exp(m2 - m1)
        w_top = 1.0 / (1.0 + e2)
        gate_ref[0] = jnp.where(lane_f == i1, w_top, 0.0) + jnp.where(lane_f == i2, e2 * w_top, 0.0)


def _out_proj(xa, o_sgu, o_ssd, o_mla, mod, w1, w2, w3, g, ctx_len, rows, latent_only, w_router=None):
    bsz, t_all, d = xa.shape
    route = w_router is not None
    tile0 = ctx_len // rows if latent_only else 0
    n_out = t_all - (ctx_len if latent_only else 0)
    kern = functools.partial(_out_kernel, rows=rows, ctx_len=ctx_len, tile0=tile0, route=route)
    const = lambda b, t: (0, 0)
    tok = lambda w: pl.BlockSpec((1, rows, w), lambda b, t: (b, t + tile0, 0))
    out_tok = lambda w: pl.BlockSpec((1, rows, w), lambda b, t: (b, t, 0))
    in_specs = [tok(d), tok(SGU_WIDTH), tok(SSD_WIDTH), tok(MLA_WIDTH), pl.BlockSpec(mod.shape, const),
                pl.BlockSpec(w1.shape, const), pl.BlockSpec(w2.shape, const), pl.BlockSpec(w3.shape, const),
                pl.BlockSpec(g.shape, const)]
    out_shape = [jax.ShapeDtypeStruct((bsz, n_out, d), F32), jax.ShapeDtypeStruct((bsz, n_out, d), BF16)]
    out_specs = [out_tok(d), out_tok(d)]
    args = [xa, o_sgu, o_ssd, o_mla, mod, w1, w2, w3, g]
    if route:
        in_specs.append(pl.BlockSpec(w_router.shape, const))
        out_shape.append(jax.ShapeDtypeStruct((bsz, n_out, LANES), F32))
        out_specs.append(out_tok(LANES))
        args.append(w_router)
    return pl.pallas_call(
        kern, out_shape=out_shape, grid=(bsz, n_out // rows), in_specs=in_specs, out_specs=out_specs,
        compiler_params=pltpu.CompilerParams(dimension_semantics=("arbitrary", "arbitrary"),
                                             vmem_limit_bytes=VMEM_LIMIT),
        name="out_proj",
    )(*args)


FF_SPLITS = ((0, 1536), (1536, D_FF))


def _ffn_kernel(x_ref, h_ref, mod_ref, wg_ref, wu_ref, wd_ref, o_ref, *, rows, ctx_len):
    b = pl.program_id(0)
    t = pl.program_id(1)
    h = h_ref[0]
    acc = None
    for c0, c1 in FF_SPLITS:
        gate = jnp.dot(h, wg_ref[:, c0:c1], preferred_element_type=F32)
        up = jnp.dot(h, wu_ref[:, c0:c1], preferred_element_type=F32)
        part = jnp.dot((_silu(gate) * up).astype(BF16), wd_ref[c0:c1, :], preferred_element_type=F32)
        acc = part if acc is None else acc + part
    o_ref[0] = x_ref[0] + _row_select(mod_ref, b, t, rows, ctx_len, 5 * D_MODEL) * acc


def _ffn_dense(xa, h, mod, wg, wu, wd, ctx_len, rows):
    bsz, t_all, d = xa.shape
    kern = functools.partial(_ffn_kernel, rows=rows, ctx_len=ctx_len)
    const = lambda b, t: (0, 0)
    tok = pl.BlockSpec((1, rows, d), lambda b, t: (b, t, 0))
    return pl.pallas_call(
        kern,
        out_shape=jax.ShapeDtypeStruct((bsz, t_all, d), F32),
        grid=(bsz, t_all // rows),
        in_specs=[tok, tok, pl.BlockSpec(mod.shape, const),
                  pl.BlockSpec(wg.shape, const), pl.BlockSpec(wu.shape, const), pl.BlockSpec(wd.shape, const)],
        out_specs=tok,
        compiler_params=pltpu.CompilerParams(dimension_semantics=("arbitrary", "arbitrary"),
                                             vmem_limit_bytes=VMEM_LIMIT),
        name="ffn_dense",
    )(xa, h, mod, wg, wu, wd)


def _moe_kernel(x_ref, h_ref, gate_ref, mod_ref, wg_ref, wu_ref, wd_ref, gf_ref, o_ref, acc_ref):
    b = pl.program_id(0)
    e = pl.program_id(2)
    lane = lax.broadcasted_iota(jnp.int32, gate_ref.shape[1:], 1)
    ge = jnp.sum(jnp.where(lane == e, gate_ref[0], 0.0), axis=-1, keepdims=True)
    h = h_ref[0]
    gate = jnp.dot(h, wg_ref[0], preferred_element_type=F32)
    up = jnp.dot(h, wu_ref[0], preferred_element_type=F32)
    part = jnp.dot((_silu(gate) * up * ge).astype(BF16), wd_ref[0], preferred_element_type=F32)

    @pl.when(e == 0)
    def _():
        acc_ref[...] = part

    @pl.when(e > 0)
    def _():
        acc_ref[...] = acc_ref[...] + part

    @pl.when(e == N_EXPERTS - 1)
    def _():
        gate5 = mod_ref[pl.ds(b, 1), 5 * D_MODEL:6 * D_MODEL]
        o_ref[0] = _rms(x_ref[0] + gate5 * acc_ref[...]) * gf_ref[...]


def _moe_final(x, h, gates, mod, wg, wu, wd, g_final, rows):
    bsz, n, d = x.shape
    const = lambda b, t, e: (0, 0)
    tok = lambda w: pl.BlockSpec((1, rows, w), lambda b, t, e: (b, t, 0))
    return pl.pallas_call(
        _moe_kernel,
        out_shape=jax.ShapeDtypeStruct((bsz, n, d), F32),
        grid=(bsz, n // rows, N_EXPERTS),
        in_specs=[tok(d), tok(d), tok(LANES), pl.BlockSpec(mod.shape, const),
                  pl.BlockSpec((1, d, D_FF_EXPERT), lambda b, t, e: (e, 0, 0)),
                  pl.BlockSpec((1, d, D_FF_EXPERT), lambda b, t, e: (e, 0, 0)),
                  pl.BlockSpec((1, D_FF_EXPERT, d), lambda b, t, e: (e, 0, 0)),
                  pl.BlockSpec(g_final.shape, const)],
        out_specs=tok(d),
        scratch_shapes=[pltpu.VMEM((rows, d), F32)],
        compiler_params=pltpu.CompilerParams(dimension_semantics=("arbitrary", "arbitrary", "arbitrary"),
                                             vmem_limit_bytes=VMEM_LIMIT),
        name="moe_final",
    )(x, h, gates, mod, wg, wu, wd, g_final)


def _pad_in_weight(w):
    d = w.shape[0]
    z = lambda n: jnp.zeros((d, n), w.dtype)
    kr0 = 512 + 1292 + Q_LORA + KV_LORA
    return jnp.concatenate([w[:, :1804], z(LANES - 2 * SSD_HEADS), w[:, 1804:kr0], z(ROPE_LANE0), w[:, kr0:],
                            z(LANES - ROPE_LANE0 - MLA_ROPE)], axis=1).astype(BF16)


def _head_slots(w, per_head, take0, take1):
    k = w.shape[0]
    w = w.reshape(k, MLA_HEADS, per_head)[:, :, take0:take1]
    w = jnp.pad(w, ((0, 0), (0, 0), (0, HEAD_SLOT - (take1 - take0))))
    return w.reshape(k, MLA_HEADS * HEAD_SLOT).astype(BF16)


def _rope_tables(n_latent, ctx_len):
    t = jnp.arange(n_latent)
    pos = jnp.stack([(t // GRID_W).astype(F32), (t % GRID_W).astype(F32)], axis=1)
    n_freq = MLA_ROPE // 4
    inv_freq = ROPE_THETA ** (-jnp.arange(n_freq, dtype=F32) / n_freq)
    ang = pos[:, :, None] * inv_freq
    cos, sin = jnp.cos(ang), jnp.sin(ang)
    zero = jnp.zeros_like(sin)
    cos_r = jnp.stack([cos, cos], axis=2).reshape(n_latent, MLA_ROPE)
    sa_r = jnp.stack([zero, sin], axis=2).reshape(n_latent, MLA_ROPE)
    sb_r = jnp.stack([-sin, zero], axis=2).reshape(n_latent, MLA_ROPE)

    def slot(r, fill):
        lat = jnp.concatenate([jnp.full((n_latent, ROPE_LANE0), fill, F32), r,
                               jnp.full((n_latent, LANES - ROPE_LANE0 - MLA_ROPE), fill, F32)], axis=1)
        return jnp.concatenate([jnp.full((ctx_len, LANES), fill, F32), lat], axis=0)

    return slot(cos_r, 1.0), slot(sa_r, 0.0), slot(sb_r, 0.0)


def _pad_lanes(v, fill=0.0):
    v = v.reshape(1, -1)
    return jnp.pad(v, ((0, 0), (0, LANES - v.shape[1])), constant_values=fill)


def kernel(x, c, ctx, c_ctx, w_mod, b_mod, g_mix, w_in, w_sgu, b_sgu, g_sgu, beta_sgu, conv_w, conv_b, dt_bias,
           a_log, d_skip, g_ssd, g_q, w_uq, g_kv, w_ukv, w_out, g_ffn, w_gate, w_up, w_down, w_router, w_gate_e,
           w_up_e, w_down_e, g_final):
    bsz, n, d = x.shape
    ctx_len = ctx.shape[1]
    depth = w_in.shape[0]
    assert bsz <= CTX_MOD_ROW and ctx_len % CHUNK == 0 and n % CHUNK == 0

    cond = jnp.concatenate([c, jnp.zeros((CTX_MOD_ROW - bsz, d), F32), c_ctx[None],
                            jnp.zeros((MOD_ROWS - CTX_MOD_ROW - 1, d), F32)], axis=0)
    mod_all = _mod_table(cond, w_mod, b_mod)
    cos, sin_a, sin_b = _rope_tables(n, ctx_len)
    xa = jnp.concatenate([ctx, x], axis=1)
    t_all = ctx_len + n
    rows_in = 768 if t_all % 768 == 0 else ctx_len

    out = None
    for layer in range(depth):
        last = layer == depth - 1
        mod = mod_all[layer]
        row = lambda v: v.reshape(1, -1)
        p_sgu, p_z, p_xbc, p_dt, p_q, p_kv, p_kr = _in_proj(xa, mod, row(g_mix[layer]), _pad_in_weight(w_in[layer]),
                                                            ctx_len, rows_in)
        w_cat = jnp.transpose(w_sgu[layer], (1, 0, 2)).reshape(CHUNK, SGU_HEADS * CHUNK).astype(BF16)
        bias = jnp.repeat(b_sgu[layer].T, SGU_HEAD_DIM, axis=1)
        o_sgu = _sgu(p_sgu, w_cat, bias, row(g_sgu[layer]), row(beta_sgu[layer]), rows_in)
        o_ssd = _ssd(p_z, p_xbc, p_dt, conv_w[layer], row(conv_b[layer]), _pad_lanes(dt_bias[layer]),
                     _pad_lanes(a_log[layer]), row(jnp.repeat(d_skip[layer], SSD_HEAD_DIM)), row(g_ssd[layer]),
                     ctx_len)
        per_q = MLA_NOPE + MLA_ROPE
        per_kv = MLA_NOPE + MLA_V
        q, k, v = _mla_proj(p_q, p_kv, p_kr, row(g_q[layer]), _head_slots(w_uq[layer], per_q, 0, per_q),
                            row(g_kv[layer]), _head_slots(w_ukv[layer], per_kv, 0, MLA_NOPE),
                            _head_slots(w_ukv[layer], per_kv, MLA_NOPE, per_kv), cos, sin_a, sin_b, rows_in)
        o_mla = _attention(q, k, v, ctx_len, need_ctx=not last)
        wo = w_out[layer].astype(BF16)
        w1, w2, w3 = wo[:SGU_WIDTH], wo[SGU_WIDTH:SGU_WIDTH + SSD_WIDTH], wo[SGU_WIDTH + SSD_WIDTH:]
        i = layer // 2
        if layer % 2 == 0:
            x_mid, h_ffn = _out_proj(xa, o_sgu, o_ssd, o_mla, mod, w1, w2, w3, row(g_ffn[layer]), ctx_len,
                                     ctx_len, latent_only=last)
            if last:
                raise NotImplementedError("final dense channel mixer")
            xa = _ffn_dense(x_mid, h_ffn, mod, w_gate[i].astype(BF16), w_up[i].astype(BF16),
                            w_down[i].astype(BF16), ctx_len, ctx_len)
        else:
            if not last:
                raise NotImplementedError("expert channel mixer on a non-final layer")
            wr = jnp.pad(w_router[i], ((0, 0), (0, LANES - N_EXPERTS)))
            x_mid, h_ffn, gates = _out_proj(xa, o_sgu, o_ssd, o_mla, mod, w1, w2, w3, row(g_ffn[layer]), ctx_len,
                                            ctx_len, latent_only=True, w_router=wr)
            out = _moe_final(x_mid, h_ffn, gates, mod, w_gate_e[i].astype(BF16), w_up_e[i].astype(BF16),
                             w_down_e[i].astype(BF16), row(g_final), 512)
    return out
```

```python
import functools

import jax
import jax.numpy as jnp
from jax import lax
from jax.experimental import pallas as pl
from jax.experimental.pallas import tpu as pltpu

F32 = jnp.float32
BF16 = jnp.bfloat16
HIGHEST = lax.Precision.HIGHEST

D_MODEL = 1024
EPS = 1e-6
N_MOD = 6
GRID_W = 64
CHUNK = 128

SGU_HEADS = 4
SGU_HEAD_DIM = 64
SGU_WIDTH = 256

SSD_HEADS = 6
SSD_HEAD_DIM = 64
SSD_WIDTH = 384
SSD_GROUPS = 2
SSD_HPG = 3
SSD_STATE = 128
SSD_CONV = 5
SSD_CONV_CH = 896
SSD_GW = SSD_HPG * SSD_HEAD_DIM
DT_REP = 16

MLA_HEADS = 6
MLA_NOPE = 64
MLA_ROPE = 32
MLA_V = 64
MLA_WIDTH = 384
Q_LORA = 384
KV_LORA = 256
ROPE_THETA = 10000.0
MLA_SCALE = (MLA_NOPE + MLA_ROPE) ** -0.5
HEAD_SLOT = 128
ROPE_LANE0 = MLA_NOPE

D_FF = 2816
N_EXPERTS = 8
D_FF_EXPERT = 1408

LANES = 128
MOD_ROWS = 16
CTX_MOD_ROW = 8

IN_COLS = (("sgu", 0, 512), ("z", 512, 896), ("xbc", 896, 1792), ("dt", 1792, 1920), ("cq", 1920, 2304),
           ("ckv", 2304, 2560), ("kr", 2560, 2688))
IN_PAD_WIDTH = 2688

VMEM_LIMIT = 56 * 1024 * 1024


def _sigmoid(x):
    return 1.0 / (1.0 + jnp.exp(-x))


def _silu(x):
    return x * _sigmoid(x)


def _rms(x):
    return x * lax.rsqrt(jnp.mean(x * x, axis=-1, keepdims=True) + EPS)


def _aligned(v, m):
    return v if isinstance(v, int) else pl.multiple_of(v, m)


def _row_select(mod_ref, b, tile, rows, ctx_len, col0):
    row = tile * rows + lax.broadcasted_iota(jnp.int32, (rows, 1), 0)
    is_ctx = row < ctx_len
    mb = mod_ref[pl.ds(b, 1), col0:col0 + D_MODEL]
    mc = mod_ref[CTX_MOD_ROW:CTX_MOD_ROW + 1, col0:col0 + D_MODEL]
    return jnp.where(is_ctx, mc, mb)


def _mod_kernel(cond_ref, w_ref, b_ref, o_ref):
    s = _silu(cond_ref[...])
    o_ref[0] = jnp.dot(s, w_ref[0], precision=HIGHEST, preferred_element_type=F32) + b_ref[0]


def _mod_table(cond, w_mod, b_mod):
    n_layers, d, width = w_mod.shape
    cb = 1536
    return pl.pallas_call(
        _mod_kernel,
        out_shape=jax.ShapeDtypeStruct((n_layers, MOD_ROWS, width), F32),
        grid=(n_layers, width // cb),
        in_specs=[pl.BlockSpec((MOD_ROWS, d), lambda l, j: (0, 0)),
                  pl.BlockSpec((1, d, cb), lambda l, j: (l, 0, j)),
                  pl.BlockSpec((1, 1, cb), lambda l, j: (l, 0, j))],
        out_specs=pl.BlockSpec((1, MOD_ROWS, cb), lambda l, j: (l, 0, j)),
        compiler_params=pltpu.CompilerParams(dimension_semantics=("arbitrary", "arbitrary"),
                                             vmem_limit_bytes=VMEM_LIMIT),
        name="mod_table",
    )(cond, w_mod, b_mod.reshape(n_layers, 1, width))


def _in_kernel(x_ref, mod_ref, g_ref, w_ref, *out_refs, rows, ctx_len):
    b = pl.program_id(0)
    t = pl.program_id(1)
    xn = _rms(x_ref[0]) * g_ref[...]
    shift = _row_select(mod_ref, b, t, rows, ctx_len, 0)
    scale = _row_select(mod_ref, b, t, rows, ctx_len, D_MODEL)
    h = (xn * (1.0 + scale) + shift).astype(BF16)
    for o_ref, (_, c0, c1) in zip(out_refs, IN_COLS):
        o_ref[0] = jnp.dot(h, w_ref[:, c0:c1], preferred_element_type=F32)


def _in_proj(xa, mod, g, w_pad, ctx_len, rows):
    bsz, t_all, d = xa.shape
    kern = functools.partial(_in_kernel, rows=rows, ctx_len=ctx_len)
    return pl.pallas_call(
        kern,
        out_shape=[jax.ShapeDtypeStruct((bsz, t_all, c1 - c0), F32) for _, c0, c1 in IN_COLS],
        grid=(bsz, t_all // rows),
        in_specs=[pl.BlockSpec((1, rows, d), lambda b, t: (b, t, 0)),
                  pl.BlockSpec(mod.shape, lambda b, t: (0, 0)),
                  pl.BlockSpec((1, d), lambda b, t: (0, 0)),
                  pl.BlockSpec(w_pad.shape, lambda b, t: (0, 0))],
        out_specs=[pl.BlockSpec((1, rows, c1 - c0), lambda b, t: (b, t, 0)) for _, c0, c1 in IN_COLS],
        compiler_params=pltpu.CompilerParams(dimension_semantics=("arbitrary", "arbitrary"),
                                             vmem_limit_bytes=VMEM_LIMIT),
        name="in_proj",
    )(xa, mod, g, w_pad)


def _sgu_kernel(p_ref, w_ref, bias_ref, g_ref, beta_ref, o_ref, *, n_chunks):
    lane = lax.broadcasted_iota(jnp.int32, (1, SGU_WIDTH), 1)
    head_of_lane = lane // SGU_HEAD_DIM
    w = w_ref[...]
    bias = bias_ref[...]
    c0 = 0.7978845608028654
    for c in range(n_chunks):
        p = p_ref[0, c * CHUNK:(c + 1) * CHUNK, :]
        ge = 0.5 * p * (1.0 + jnp.tanh(c0 * (p + 0.044715 * (p * p * p))))
        u = ge[:, :SGU_WIDTH]
        v = ge[:, SGU_WIDTH:]
        mu = jnp.mean(v, axis=-1, keepdims=True)
        vc = v - mu
        vn = vc * lax.rsqrt(jnp.mean(vc * vc, axis=-1, keepdims=True) + EPS) * g_ref[...] + beta_ref[...]
        stacked = jnp.concatenate(
            [jnp.where(head_of_lane == h, vn, 0.0).astype(BF16) for h in range(SGU_HEADS)], axis=0)
        mixed = jnp.dot(w, stacked, preferred_element_type=F32) + bias
        o_ref[0, c * CHUNK:(c + 1) * CHUNK, :] = (u * mixed).astype(BF16)


def _sgu(p_sgu, w_cat, bias, g, beta, rows):
    bsz, t_all, width = p_sgu.shape
    kern = functools.partial(_sgu_kernel, n_chunks=rows // CHUNK)
    const = lambda b, t: (0, 0)
    return pl.pallas_call(
        kern,
        out_shape=jax.ShapeDtypeStruct((bsz, t_all, SGU_WIDTH), BF16),
        grid=(bsz, t_all // rows),
        in_specs=[pl.BlockSpec((1, rows, width), lambda b, t: (b, t, 0)),
                  pl.BlockSpec(w_cat.shape, const), pl.BlockSpec(bias.shape, const),
                  pl.BlockSpec(g.shape, const), pl.BlockSpec(beta.shape, const)],
        out_specs=pl.BlockSpec((1, rows, SGU_WIDTH), lambda b, t: (b, t, 0)),
        compiler_params=pltpu.CompilerParams(dimension_semantics=("arbitrary", "arbitrary"),
                                             vmem_limit_bytes=VMEM_LIMIT),
        name="sgu",
    )(p_sgu, w_cat, bias, g, beta)


def _ssd_kernel(z_ref, xbc_ref, dt_ref, cw_ref, cb_ref, dtb_ref, alog_ref, skip_ref, g_ref, o_ref,
                xc_ref, yf_ref, yb_ref, st_ref, *, n_blk, n_ctx_blk):
    def conv_block(blk, seg_start, seg_end):
        r0 = _aligned(blk * CHUNK, CHUNK)
        for cg in range(SSD_CONV_CH // LANES):
            cols = slice(cg * LANES, (cg + 1) * LANES)
            zeros = jnp.zeros((8, LANES), F32)
            top = zeros if seg_start else xbc_ref[0, pl.ds(_aligned(r0 - 8, 8), 8), cols]
            bot = zeros if seg_end else xbc_ref[0, pl.ds(_aligned(r0 + CHUNK, 8), 8), cols]
            xw = jnp.concatenate([top, xbc_ref[0, pl.ds(r0, CHUNK), cols], bot], axis=0)
            acc = cb_ref[:, cols] + cw_ref[0:1, cols] * xw[6:6 + CHUNK]
            for k in range(1, SSD_CONV):
                acc = acc + cw_ref[k:k + 1, cols] * xw[6 + k:6 + k + CHUNK]
            xc_ref[pl.ds(r0, CHUNK), cols] = _silu(acc)

    static_blocks = sorted(set(list(range(n_ctx_blk)) + [n_ctx_blk, n_blk - 1]))
    for blk in static_blocks:
        conv_block(blk, blk == 0 or blk == n_ctx_blk, blk == n_ctx_blk - 1 or blk == n_blk - 1)
    if n_blk - 1 > n_ctx_blk + 1:
        def conv_body(blk, carry):
            conv_block(blk, False, False)
            return carry
        lax.fori_loop(n_ctx_blk + 1, n_blk - 1, conv_body, 0)

    li = lax.broadcasted_iota(jnp.int32, (CHUNK, CHUNK), 0)
    si = lax.broadcasted_iota(jnp.int32, (CHUNK, CHUNK), 1)
    a_neg = -jnp.exp(alog_ref[...])

    def chunk_step(c, direction):
        r0 = pl.multiple_of(c * CHUNK, CHUNK)
        rows = pl.ds(r0, CHUNK)
        mask = (si <= li) if direction == 0 else (si >= li)
        tri = jnp.where(mask, 1.0, 0.0).astype(BF16)
        dtr = dt_ref[0, rows, :] + dtb_ref[...]
        dt = jnp.maximum(dtr, 0.0) + jnp.log1p(jnp.exp(-jnp.abs(dtr)))
        adt = dt * a_neg
        p1 = adt.astype(BF16)
        r1 = adt - p1.astype(F32)
        p2 = r1.astype(BF16)
        p3 = (r1 - p2.astype(F32)).astype(BF16)
        parts = jnp.dot(tri, jnp.concatenate([p1, p2, p3], axis=1), preferred_element_type=F32)
        acs = parts[:, :LANES] + parts[:, LANES:2 * LANES] + parts[:, 2 * LANES:]
        end = CHUNK - 1 if direction == 0 else 0
        tot = acs[end:end + 1, :]
        to_end_dt = jnp.exp(tot - acs) * dt
        chunk_decay = jnp.exp(tot)
        per_head_rows = jnp.where(si < DT_REP, acs, jnp.where(si < 2 * DT_REP, dt, to_end_dt)).T
        first_half = si < SSD_HEAD_DIM
        cbs, bm_ts, y_offs = [], [], []
        state = st_ref[direction]
        state16 = state.astype(BF16)
        for g in range(SSD_GROUPS):
            bm = xc_ref[rows, SSD_WIDTH + g * SSD_STATE:SSD_WIDTH + (g + 1) * SSD_STATE]
            cm = xc_ref[rows, SSD_WIDTH + SSD_GROUPS * SSD_STATE + g * SSD_STATE:
                        SSD_WIDTH + SSD_GROUPS * SSD_STATE + (g + 1) * SSD_STATE]
            cm16 = cm.astype(BF16)
            cbs.append(lax.dot_general(cm16, bm.astype(BF16), (((1,), (1,)), ((), ())),
                                       preferred_element_type=F32))
            bm_ts.append(bm.T)
            y_offs.append(jnp.dot(cm16, state16, preferred_element_type=F32))
        y_pairs, state_pairs = [], []
        for k in range(SSD_HEADS // 2):
            pair = slice(k * LANES, (k + 1) * LANES)
            xs16 = xc_ref[rows, pair].astype(BF16)
            y_diag, contrib, exp_a, dec, y_off = [], [], [], [], []
            for hh in (2 * k, 2 * k + 1):
                g = hh // SSD_HPG
                col = direction * SSD_HEADS + hh
                a_col = jnp.broadcast_to(acs[:, col:col + 1], (CHUNK, CHUNK))
                seg = a_col - per_head_rows[col:col + 1, :]
                decay = jnp.exp(jnp.where(mask, seg, -1e30))
                m = (cbs[g] * decay * per_head_rows[DT_REP + col:DT_REP + col + 1, :]).astype(BF16)
                y_diag.append(jnp.dot(m, xs16, preferred_element_type=F32))
                exp_a.append(jnp.exp(a_col))
                y_off.append(y_offs[g][:, pair])
                lhs = (bm_ts[g] * per_head_rows[2 * DT_REP + col:2 * DT_REP + col + 1, :]).astype(BF16)
                contrib.append(jnp.dot(lhs, xs16, preferred_element_type=F32))
                dec.append(jnp.broadcast_to(chunk_decay[:, col:col + 1], (SSD_STATE, LANES)))
            pick = lambda ab: jnp.where(first_half, ab[0], ab[1])
            y_pairs.append(pick(y_diag) + pick(exp_a) * pick(y_off))
            state_pairs.append(pick(dec) * state[:, pair] + pick(contrib))
        st_ref[direction] = jnp.concatenate(state_pairs, axis=1)
        y = jnp.concatenate(y_pairs, axis=1)
        if direction == 0:
            yf_ref[rows, :] = skip_ref[...] * xc_ref[rows, 0:SSD_WIDTH] + y
        else:
            yb_ref[rows, :] = y

    def scan_body(step, carry):
        chunk_step(step, 0)
        chunk_step(jnp.where(step < n_ctx_blk, n_ctx_blk - 1 - step, n_blk - 1 - (step - n_ctx_blk)), 1)
        return carry

    st_ref[...] = jnp.zeros(st_ref.shape, F32)
    lax.fori_loop(0, n_blk, scan_body, 0)

    def out_body(c, carry):
        rows = pl.ds(pl.multiple_of(c * CHUNK, CHUNK), CHUNK)
        gated = (yf_ref[rows, :] + yb_ref[rows, :]) * _silu(z_ref[0, rows, :])
        o_ref[0, rows, :] = (_rms(gated) * g_ref[...]).astype(BF16)
        return carry

    lax.fori_loop(0, n_blk, out_body, 0)


def _ssd(p_z, p_xbc, p_dt, conv_w, conv_b, dt_bias, a_log, skip, g, ctx_len):
    bsz, t_all, _ = p_z.shape
    kern = functools.partial(_ssd_kernel, n_blk=t_all // CHUNK, n_ctx_blk=ctx_len // CHUNK)
    const = lambda b: (0, 0)
    per_b = lambda w: pl.BlockSpec((1, t_all, w), lambda b: (b, 0, 0))
    return pl.pallas_call(
        kern,
        out_shape=jax.ShapeDtypeStruct((bsz, t_all, SSD_WIDTH), BF16),
        grid=(bsz,),
        in_specs=[per_b(SSD_WIDTH), per_b(SSD_CONV_CH), per_b(LANES),
                  pl.BlockSpec(conv_w.shape, const), pl.BlockSpec(conv_b.shape, const),
                  pl.BlockSpec(dt_bias.shape, const), pl.BlockSpec(a_log.shape, const),
                  pl.BlockSpec(skip.shape, const), pl.BlockSpec(g.shape, const)],
        out_specs=per_b(SSD_WIDTH),
        scratch_shapes=[pltpu.VMEM((t_all, SSD_CONV_CH), F32), pltpu.VMEM((t_all, SSD_WIDTH), F32),
                        pltpu.VMEM((t_all, SSD_WIDTH), F32),
                        pltpu.VMEM((2, SSD_STATE, SSD_WIDTH), F32)],
        compiler_params=pltpu.CompilerParams(dimension_semantics=("arbitrary",), vmem_limit_bytes=VMEM_LIMIT),
        name="ssd",
    )(p_z, p_xbc, p_dt, conv_w, conv_b, dt_bias, a_log, skip, g)


def _mla_proj_kernel(pq_ref, pkv_ref, pkr_ref, gq_ref, wq_ref, gkv_ref, wk_ref, wv_ref, cos_ref, sa_ref, sb_ref,
                     q_ref, k_ref, v_ref):
    cos = cos_ref[...]
    sin_a = sa_ref[...]
    sin_b = sb_ref[...]

    def rope(t):
        return t * cos + pltpu.roll(t, 8, 1) * sin_a + pltpu.roll(t, LANES - 8, 1) * sin_b

    qn = (_rms(pq_ref[0]) * gq_ref[...]).astype(BF16)
    q = jnp.dot(qn, wq_ref[...], preferred_element_type=F32)
    kvn = (_rms(pkv_ref[0]) * gkv_ref[...]).astype(BF16)
    k = jnp.dot(kvn, wk_ref[...], preferred_element_type=F32)
    v_ref[0] = jnp.dot(kvn, wv_ref[...], preferred_element_type=F32).astype(BF16)
    kr = rope(pkr_ref[0])
    for h in range(MLA_HEADS):
        slot = slice(h * HEAD_SLOT, (h + 1) * HEAD_SLOT)
        q_ref[0, :, slot] = (rope(q[:, slot]) * MLA_SCALE).astype(BF16)
        k_ref[0, :, slot] = (k[:, slot] + kr).astype(BF16)


def _mla_proj(p_q, p_kv, p_kr, g_q, w_q, g_kv, w_k, w_v, cos, sin_a, sin_b, rows):
    bsz, t_all, _ = p_q.shape
    width = MLA_HEADS * HEAD_SLOT
    const = lambda b, t: (0, 0)
    tok = lambda w: pl.BlockSpec((1, rows, w), lambda b, t: (b, t, 0))
    tab = pl.BlockSpec((rows, LANES), lambda b, t: (t, 0))
    out = jax.ShapeDtypeStruct((bsz, t_all, width), BF16)
    return pl.pallas_call(
        _mla_proj_kernel,
        out_shape=[out, out, out],
        grid=(bsz, t_all // rows),
        in_specs=[tok(Q_LORA), tok(KV_LORA), tok(LANES),
                  pl.BlockSpec(g_q.shape, const), pl.BlockSpec(w_q.shape, const),
                  pl.BlockSpec(g_kv.shape, const), pl.BlockSpec(w_k.shape, const), pl.BlockSpec(w_v.shape, const),
                  tab, tab, tab],
        out_specs=[tok(width), tok(width), tok(width)],
        compiler_params=pltpu.CompilerParams(dimension_semantics=("arbitrary", "arbitrary"),
                                             vmem_limit_bytes=VMEM_LIMIT),
        name="mla_proj",
    )(p_q, p_kv, p_kr, g_q, w_q, g_kv, w_k, w_v, cos, sin_a, sin_b)


def _attn_kernel(q_ref, k_ref, v_ref, o_ref, *, t_all, ctx_len, ctx_block):
    def run(n_keys):
        outs = []
        for h in range(MLA_HEADS):
            slot = slice(h * HEAD_SLOT, (h + 1) * HEAD_SLOT)
            s = lax.dot_general(q_ref[0, :, slot], k_ref[0, 0:n_keys, slot], (((1,), (1,)), ((), ())),
                                preferred_element_type=F32)
            p = jnp.exp(s - jnp.max(s, axis=-1, keepdims=True))
            denom = jnp.sum(p, axis=-1, keepdims=True)
            o = jnp.dot(p.astype(BF16), v_ref[0, 0:n_keys, slot], preferred_element_type=F32)
            outs.append(o[:, :MLA_V] / denom)
        o_ref[0] = jnp.concatenate(outs, axis=1).astype(BF16)

    if ctx_block:
        is_ctx = pl.program_id(1) == 0
        pl.when(is_ctx)(lambda: run(ctx_len))
        pl.when(jnp.logical_not(is_ctx))(lambda: run(t_all))
    else:
        run(t_all)


def _attention(q, k, v, ctx_len, need_ctx):
    bsz, t_all, width = q.shape
    tq = ctx_len
    first = 0 if need_ctx else 1
    kern = functools.partial(_attn_kernel, t_all=t_all, ctx_len=ctx_len, ctx_block=need_ctx)
    kv_spec = pl.BlockSpec((1, t_all, width), lambda b, i: (b, 0, 0))
    return pl.pallas_call(
        kern,
        out_shape=jax.ShapeDtypeStruct((bsz, t_all, MLA_WIDTH), BF16),
        grid=(bsz, t_all // tq - first),
        in_specs=[pl.BlockSpec((1, tq, width), lambda b, i: (b, i + first, 0)), kv_spec, kv_spec],
        out_specs=pl.BlockSpec((1, tq, MLA_WIDTH), lambda b, i: (b, i + first, 0)),
        compiler_params=pltpu.CompilerParams(dimension_semantics=("arbitrary", "arbitrary"),
                                             vmem_limit_bytes=VMEM_LIMIT),
        name="attention",
    )(q, k, v)


def _out_kernel(x_ref, o1_ref, o2_ref, o3_ref, mod_ref, w1_ref, w2_ref, w3_ref, g_ref, *rest,
                rows, ctx_len, tile0, route):
    if route:
        wr_ref, xo_ref, h_ref, gate_ref = rest
    else:
        xo_ref, h_ref = rest
    b = pl.program_id(0)
    t = pl.program_id(1) + tile0
    mix = (jnp.dot(o1_ref[0], w1_ref[...], preferred_element_type=F32)
           + jnp.dot(o2_ref[0], w2_ref[...], preferred_element_type=F32)
           + jnp.dot(o3_ref[0], w3_ref[...], preferred_element_type=F32))
    x = x_ref[0] + _row_select(mod_ref, b, t, rows, ctx_len, 2 * D_MODEL) * mix
    xo_ref[0] = x
    shift = _row_select(mod_ref, b, t, rows, ctx_len, 3 * D_MODEL)
    scale = _row_select(mod_ref, b, t, rows, ctx_len, 4 * D_MODEL)
    h = _rms(x) * g_ref[...] * (1.0 + scale) + shift
    h_ref[0] = h.astype(BF16)
    if route:
        h_hi = h.astype(BF16)
        h_lo = (h - h_hi.astype(F32)).astype(BF16)
        by_hi = jnp.dot(h_hi, wr_ref[...], preferred_element_type=F32)
        logits = (by_hi[:, :LANES] + by_hi[:, LANES:]
                  + jnp.dot(h_lo, wr_ref[:, :LANES], preferred_element_type=F32))
        lane = lax.broadcasted_iota(jnp.int32, logits.shape, 1)
        lane_f = lane.astype(F32)
        lg = jnp.where(lane < N_EXPERTS, logits, -jnp.inf)
        m1 = jnp.max(lg, axis=-1, keepdims=True)
        i1 = jnp.min(jnp.where(lg == m1, lane_f, float(LANES)), axis=-1, keepdims=True)
        lg2 = jnp.where(lane_f == i1, -jnp.inf, lg)
        m2 = jnp.max(lg2, axis=-1, keepdims=True)
        i2 = jnp.min(jnp.where(lg2 == m2, lane_f, float(LANES)), axis=-1, keepdims=True)
        e2 = jnp.exp(m2 - m1)
        w_top = 1.0 / (1.0 + e2)
        gate_ref[0] = jnp.where(lane_f == i1, w_top, 0.0) + jnp.where(lane_f == i2, e2 * w_top, 0.0)


def _out_proj(xa, o_sgu, o_ssd, o_mla, mod, w1, w2, w3, g, ctx_len, rows, latent_only, w_router=None):
    bsz, t_all, d = xa.shape
    route = w_router is not None
    tile0 = ctx_len // rows if latent_only else 0
    n_out = t_all - (ctx_len if latent_only else 0)
    kern = functools.partial(_out_kernel, rows=rows, ctx_len=ctx_len, tile0=tile0, route=route)
    const = lambda b, t: (0, 0)
    tok = lambda w: pl.BlockSpec((1, rows, w), lambda b, t: (b, t + tile0, 0))
    out_tok = lambda w: pl.BlockSpec((1, rows, w), lambda b, t: (b, t, 0))
    in_specs = [tok(d), tok(SGU_WIDTH), tok(SSD_WIDTH), tok(MLA_WIDTH), pl.BlockSpec(mod.shape, const),
                pl.BlockSpec(w1.shape, const), pl.BlockSpec(w2.shape, const), pl.BlockSpec(w3.shape, const),
                pl.BlockSpec(g.shape, const)]
    out_shape = [jax.ShapeDtypeStruct((bsz, n_out, d), F32), jax.ShapeDtypeStruct((bsz, n_out, d), BF16)]
    out_specs = [out_tok(d), out_tok(d)]
    args = [xa, o_sgu, o_ssd, o_mla, mod, w1, w2, w3, g]
    if route:
        in_specs.append(pl.BlockSpec(w_router.shape, const))
        out_shape.append(jax.ShapeDtypeStruct((bsz, n_out, LANES), F32))
        out_specs.append(out_tok(LANES))
        args.append(w_router)
    return pl.pallas_call(
        kern, out_shape=out_shape, grid=(bsz, n_out // rows), in_specs=in_specs, out_specs=out_specs,
        compiler_params=pltpu.CompilerParams(dimension_semantics=("arbitrary", "arbitrary"),
                                             vmem_limit_bytes=VMEM_LIMIT),
        name="out_proj",
    )(*args)


FF_SPLITS = ((0, 1536), (1536, D_FF))


def _ffn_kernel(x_ref, h_ref, mod_ref, wg_ref, wu_ref, wd_ref, o_ref, *, rows, ctx_len):
    b = pl.program_id(0)
    t = pl.program_id(1)
    h = h_ref[0]
    acc = None
    for c0, c1 in FF_SPLITS:
        gate = jnp.dot(h, wg_ref[:, c0:c1], preferred_element_type=F32)
        up = jnp.dot(h, wu_ref[:, c0:c1], preferred_element_type=F32)
        part = jnp.dot((_silu(gate) * up).astype(BF16), wd_ref[c0:c1, :], preferred_element_type=F32)
        acc = part if acc is None else acc + part
    o_ref[0] = x_ref[0] + _row_select(mod_ref, b, t, rows, ctx_len, 5 * D_MODEL) * acc


def _ffn_dense(xa, h, mod, wg, wu, wd, ctx_len, rows):
    bsz, t_all, d = xa.shape
    kern = functools.partial(_ffn_kernel, rows=rows, ctx_len=ctx_len)
    const = lambda b, t: (0, 0)
    tok = pl.BlockSpec((1, rows, d), lambda b, t: (b, t, 0))
    return pl.pallas_call(
        kern,
        out_shape=jax.ShapeDtypeStruct((bsz, t_all, d), F32),
        grid=(bsz, t_all // rows),
        in_specs=[tok, tok, pl.BlockSpec(mod.shape, const),
                  pl.BlockSpec(wg.shape, const), pl.BlockSpec(wu.shape, const), pl.BlockSpec(wd.shape, const)],
        out_specs=tok,
        compiler_params=pltpu.CompilerParams(dimension_semantics=("arbitrary", "arbitrary"),
                                             vmem_limit_bytes=VMEM_LIMIT),
        name="ffn_dense",
    )(xa, h, mod, wg, wu, wd)


MOE_TILE = 1024
MOE_BLOCK = 128


def _plan_kernel(gate_ref, rank_ref, rank_rows_ref, count_ref):
    rows = gate_ref.shape[0]
    routed = gate_ref[...] > 0.0
    ti = lax.broadcasted_iota(jnp.int32, (rows, rows), 0)
    tj = lax.broadcasted_iota(jnp.int32, (rows, rows), 1)
    earlier = jnp.where(tj < ti, 1.0, 0.0).astype(BF16)
    ones = jnp.where(routed, 1.0, 0.0)
    before = jnp.dot(earlier, ones.astype(BF16), preferred_element_type=F32)
    rank = jnp.where(routed, before, -1.0)
    rank_ref[...] = rank
    rank_rows_ref[0] = rank.T[0:N_EXPERTS, :]
    count_ref[0] = jnp.broadcast_to(jnp.sum(ones, axis=0, keepdims=True), (8, LANES)).astype(jnp.int32)


def _route_plan(gates):
    n_tok = gates.shape[0]
    n_tiles = n_tok // MOE_TILE
    return pl.pallas_call(
        _plan_kernel,
        out_shape=[jax.ShapeDtypeStruct((n_tok, LANES), F32),
                   jax.ShapeDtypeStruct((n_tiles, N_EXPERTS, MOE_TILE), F32),
                   jax.ShapeDtypeStruct((n_tiles, 8, LANES), jnp.int32)],
        grid=(n_tiles,),
        in_specs=[pl.BlockSpec((MOE_TILE, LANES), lambda i: (i, 0))],
        out_specs=[pl.BlockSpec((MOE_TILE, LANES), lambda i: (i, 0)),
                   pl.BlockSpec((1, N_EXPERTS, MOE_TILE), lambda i: (i, 0, 0)),
                   pl.BlockSpec((1, 8, LANES), lambda i: (i, 0, 0))],
        compiler_params=pltpu.CompilerParams(dimension_semantics=("arbitrary",), vmem_limit_bytes=VMEM_LIMIT),
        name="route_plan",
    )(gates)


def _moe_kernel(count_ref, x_ref, h_ref, gate_ref, rank_ref, rank_rows_ref, mod_ref, wg_ref, wu_ref, wd_ref, gf_ref,
                o_ref, *, tiles_per_batch):
    i = pl.program_id(0)
    e = pl.program_id(1)
    rows = h_ref.shape[0]
    lane = lax.broadcasted_iota(jnp.int32, (rows, LANES), 1)
    ge = jnp.sum(jnp.where(lane == e, gate_ref[...], 0.0), axis=-1, keepdims=True)
    rank_col = jnp.sum(jnp.where(lane == e, rank_ref[...], 0.0), axis=-1, keepdims=True)
    rank_row = rank_rows_ref[0, pl.ds(e, 1), :]

    @pl.when(e == 0)
    def _():
        o_ref[...] = jnp.zeros(o_ref.shape, F32)

    slot_col = lax.broadcasted_iota(jnp.int32, (MOE_BLOCK, 1), 0)
    slot_row = lax.broadcasted_iota(jnp.int32, (1, MOE_BLOCK), 1)

    def block(j, carry):
        base = j * MOE_BLOCK
        pick = jnp.where(rank_row == (slot_col + base).astype(F32), 1.0, 0.0).astype(BF16)
        hg = jnp.dot(pick, h_ref[...], preferred_element_type=F32).astype(BF16)
        gate = jnp.dot(hg, wg_ref[0], preferred_element_type=F32)
        up = jnp.dot(hg, wu_ref[0], preferred_element_type=F32)
        y = jnp.dot((_silu(gate) * up).astype(BF16), wd_ref[0], preferred_element_type=F32).astype(BF16)
        place = jnp.where(rank_col == (slot_row + base).astype(F32), 1.0, 0.0).astype(BF16)
        o_ref[...] += ge * jnp.dot(place, y, preferred_element_type=F32)
        return carry

    n_blocks = (count_ref[i * N_EXPERTS + e] + (MOE_BLOCK - 1)) // MOE_BLOCK
    lax.fori_loop(0, n_blocks, block, 0)

    @pl.when(e == N_EXPERTS - 1)
    def _():
        gate5 = mod_ref[pl.ds(i // tiles_per_batch, 1), 5 * D_MODEL:6 * D_MODEL]
        o_ref[...] = _rms(x_ref[...] + gate5 * o_ref[...]) * gf_ref[...]


def _moe_final(x, h, gates, mod, wg, wu, wd, g_final):
    bsz, n, d = x.shape
    assert n % MOE_TILE == 0
    n_tok = bsz * n
    gates = gates.reshape(n_tok, LANES)
    rank, rank_rows, counts = _route_plan(gates)
    counts = counts[:, 0, :N_EXPERTS].reshape(-1)
    kern = functools.partial(_moe_kernel, tiles_per_batch=n // MOE_TILE)
    const = lambda i, e, cnt: (0, 0)
    tok = lambda w: pl.BlockSpec((MOE_TILE, w), lambda i, e, cnt: (i, 0))
    per_expert = lambda a, b: pl.BlockSpec((1, a, b), lambda i, e, cnt: (e, 0, 0))
    out = pl.pallas_call(
        kern,
        out_shape=jax.ShapeDtypeStruct((n_tok, d), F32),
        grid_spec=pltpu.PrefetchScalarGridSpec(
            num_scalar_prefetch=1,
            grid=(n_tok // MOE_TILE, N_EXPERTS),
            in_specs=[tok(d), tok(d), tok(LANES), tok(LANES),
                      pl.BlockSpec((1, N_EXPERTS, MOE_TILE), lambda i, e, cnt: (i, 0, 0)),
                      pl.BlockSpec(mod.shape, const),
                      per_expert(d, D_FF_EXPERT), per_expert(d, D_FF_EXPERT), per_expert(D_FF_EXPERT, d),
                      pl.BlockSpec(g_final.shape, const)],
            out_specs=tok(d)),
        compiler_params=pltpu.CompilerParams(dimension_semantics=("arbitrary", "arbitrary"),
                                             vmem_limit_bytes=VMEM_LIMIT),
        name="moe_final",
    )(counts, x.reshape(n_tok, d), h.reshape(n_tok, d), gates, rank, rank_rows, mod, wg, wu, wd, g_final)
    return out.reshape(bsz, n, d)


def _pad_in_weight(w):
    d = w.shape[0]
    z = lambda n: jnp.zeros((d, n), w.dtype)
    dt0, dt1 = 1792, 1804
    kr0 = dt1 + Q_LORA + KV_LORA
    dt_slot = _dt_lanes(w[:, dt0:dt1])
    return jnp.concatenate([w[:, :dt0], dt_slot, w[:, dt1:kr0], z(ROPE_LANE0), w[:, kr0:],
                            z(LANES - ROPE_LANE0 - MLA_ROPE)], axis=1).astype(BF16)


def _dt_lanes(v):
    gap = jnp.zeros(v.shape[:-1] + (DT_REP - v.shape[-1],), v.dtype)
    tail = jnp.zeros(v.shape[:-1] + (LANES - 3 * DT_REP,), v.dtype)
    return jnp.concatenate([v, gap, v, gap, v, gap, tail], axis=-1)


def _head_slots(w, per_head, take0, take1):
    k = w.shape[0]
    w = w.reshape(k, MLA_HEADS, per_head)[:, :, take0:take1]
    w = jnp.pad(w, ((0, 0), (0, 0), (0, HEAD_SLOT - (take1 - take0))))
    return w.reshape(k, MLA_HEADS * HEAD_SLOT).astype(BF16)


def _rope_tables(n_latent, ctx_len):
    t = jnp.arange(n_latent)
    pos = jnp.stack([(t // GRID_W).astype(F32), (t % GRID_W).astype(F32)], axis=1)
    n_freq = MLA_ROPE // 4
    inv_freq = ROPE_THETA ** (-jnp.arange(n_freq, dtype=F32) / n_freq)
    ang = pos[:, :, None] * inv_freq
    cos, sin = jnp.cos(ang), jnp.sin(ang)
    zero = jnp.zeros_like(sin)
    cos_r = jnp.stack([cos, cos], axis=2).reshape(n_latent, MLA_ROPE)
    sa_r = jnp.stack([zero, sin], axis=2).reshape(n_latent, MLA_ROPE)
    sb_r = jnp.stack([-sin, zero], axis=2).reshape(n_latent, MLA_ROPE)

    def slot(r, fill):
        lat = jnp.concatenate([jnp.full((n_latent, ROPE_LANE0), fill, F32), r,
                               jnp.full((n_latent, LANES - ROPE_LANE0 - MLA_ROPE), fill, F32)], axis=1)
        return jnp.concatenate([jnp.full((ctx_len, LANES), fill, F32), lat], axis=0)

    return slot(cos_r, 1.0), slot(sa_r, 0.0), slot(sb_r, 0.0)


def kernel(x, c, ctx, c_ctx, w_mod, b_mod, g_mix, w_in, w_sgu, b_sgu, g_sgu, beta_sgu, conv_w, conv_b, dt_bias,
           a_log, d_skip, g_ssd, g_q, w_uq, g_kv, w_ukv, w_out, g_ffn, w_gate, w_up, w_down, w_router, w_gate_e,
           w_up_e, w_down_e, g_final):
    bsz, n, d = x.shape
    ctx_len = ctx.shape[1]
    depth = w_in.shape[0]
    assert bsz <= CTX_MOD_ROW and ctx_len % CHUNK == 0 and n % CHUNK == 0

    cond = jnp.concatenate([c, jnp.zeros((CTX_MOD_ROW - bsz, d), F32), c_ctx[None],
                            jnp.zeros((MOD_ROWS - CTX_MOD_ROW - 1, d), F32)], axis=0)
    mod_all = _mod_table(cond, w_mod, b_mod)
    cos, sin_a, sin_b = _rope_tables(n, ctx_len)
    xa = jnp.concatenate([ctx, x], axis=1)
    t_all = ctx_len + n
    rows_in = 768 if t_all % 768 == 0 else ctx_len

    out = None
    for layer in range(depth):
        last = layer == depth - 1
        mod = mod_all[layer]
        row = lambda v: v.reshape(1, -1)
        p_sgu, p_z, p_xbc, p_dt, p_q, p_kv, p_kr = _in_proj(xa, mod, row(g_mix[layer]), _pad_in_weight(w_in[layer]),
                                                            ctx_len, rows_in)
        w_cat = jnp.transpose(w_sgu[layer], (1, 0, 2)).reshape(CHUNK, SGU_HEADS * CHUNK).astype(BF16)
        bias = jnp.repeat(b_sgu[layer].T, SGU_HEAD_DIM, axis=1)
        o_sgu = _sgu(p_sgu, w_cat, bias, row(g_sgu[layer]), row(beta_sgu[layer]), rows_in)
        o_ssd = _ssd(p_z, p_xbc, p_dt, conv_w[layer], row(conv_b[layer]), _dt_lanes(dt_bias[layer].reshape(1, -1)),
                     _dt_lanes(a_log[layer].reshape(1, -1)), row(jnp.repeat(d_skip[layer], SSD_HEAD_DIM)), row(g_ssd[layer]),
                     ctx_len)
        per_q = MLA_NOPE + MLA_ROPE
        per_kv = MLA_NOPE + MLA_V
        q, k, v = _mla_proj(p_q, p_kv, p_kr, row(g_q[layer]), _head_slots(w_uq[layer], per_q, 0, per_q),
                            row(g_kv[layer]), _head_slots(w_ukv[layer], per_kv, 0, MLA_NOPE),
                            _head_slots(w_ukv[layer], per_kv, MLA_NOPE, per_kv), cos, sin_a, sin_b, rows_in)
        o_mla = _attention(q, k, v, ctx_len, need_ctx=not last)
        wo = w_out[layer].astype(BF16)
        w1, w2, w3 = wo[:SGU_WIDTH], wo[SGU_WIDTH:SGU_WIDTH + SSD_WIDTH], wo[SGU_WIDTH + SSD_WIDTH:]
        i = layer // 2
        if layer % 2 == 0:
            x_mid, h_ffn = _out_proj(xa, o_sgu, o_ssd, o_mla, mod, w1, w2, w3, row(g_ffn[layer]), ctx_len,
                                     ctx_len, latent_only=last)
            if last:
                raise NotImplementedError("final dense channel mixer")
            xa = _ffn_dense(x_mid, h_ffn, mod, w_gate[i].astype(BF16), w_up[i].astype(BF16),
                            w_down[i].astype(BF16), ctx_len, ctx_len)
        else:
            if not last:
                raise NotImplementedError("expert channel mixer on a non-final layer")
            wr_hi = w_router[i].astype(BF16)
            wr_lo = (w_router[i] - wr_hi.astype(F32)).astype(BF16)
            pad = lambda w: jnp.pad(w, ((0, 0), (0, LANES - N_EXPERTS)))
            wr = jnp.concatenate([pad(wr_hi), pad(wr_lo)], axis=1)
            x_mid, h_ffn, gates = _out_proj(xa, o_sgu, o_ssd, o_mla, mod, w1, w2, w3, row(g_ffn[layer]), ctx_len,
                                            ctx_len, latent_only=True, w_router=wr)
            out = _moe_final(x_mid, h_ffn, gates, mod, w_gate_e[i].astype(BF16), w_up_e[i].astype(BF16),
                             w_down_e[i].astype(BF16), row(g_final))
    return out
```

```python
import functools

import jax
import jax.numpy as jnp
from jax import lax
from jax.experimental import pallas as pl
from jax.experimental.pallas import tpu as pltpu

F32 = jnp.float32
BF16 = jnp.bfloat16
HIGHEST = lax.Precision.HIGHEST

D_MODEL = 1024
EPS = 1e-6
N_MOD = 6
GRID_W = 64
CHUNK = 128

SGU_HEADS = 4
SGU_HEAD_DIM = 64
SGU_WIDTH = 256

SSD_HEADS = 6
SSD_HEAD_DIM = 64
SSD_WIDTH = 384
SSD_GROUPS = 2
SSD_HPG = 3
SSD_STATE = 128
SSD_CONV = 5
SSD_CONV_CH = 896
SSD_GW = SSD_HPG * SSD_HEAD_DIM
DT_REP = 16

MLA_HEADS = 6
MLA_NOPE = 64
MLA_ROPE = 32
MLA_V = 64
MLA_WIDTH = 384
Q_LORA = 384
KV_LORA = 256
ROPE_THETA = 10000.0
MLA_SCALE = (MLA_NOPE + MLA_ROPE) ** -0.5
LOG2_E = 1.4426950408889634
HEAD_SLOT = 128
ROPE_LANE0 = MLA_NOPE

D_FF = 2816
N_EXPERTS = 8
D_FF_EXPERT = 1408

LANES = 128
MOD_ROWS = 16
CTX_MOD_ROW = 8

IN_COLS = (("sgu", 0, 512), ("z", 512, 896), ("xbc", 896, 1792), ("dt", 1792, 1920), ("cq", 1920, 2304),
           ("ckv", 2304, 2560), ("kr", 2560, 2688))
IN_PAD_WIDTH = 2688

VMEM_LIMIT = 56 * 1024 * 1024


def _sigmoid(x):
    return 1.0 / (1.0 + jnp.exp(-x))


def _silu(x):
    return x * _sigmoid(x)


def _rms(x):
    return x * lax.rsqrt(jnp.mean(x * x, axis=-1, keepdims=True) + EPS)


def _aligned(v, m):
    return v if isinstance(v, int) else pl.multiple_of(v, m)


def _row_select(mod_ref, b, tile, rows, ctx_len, col0):
    row = tile * rows + lax.broadcasted_iota(jnp.int32, (rows, 1), 0)
    is_ctx = row < ctx_len
    mb = mod_ref[pl.ds(b, 1), col0:col0 + D_MODEL]
    mc = mod_ref[CTX_MOD_ROW:CTX_MOD_ROW + 1, col0:col0 + D_MODEL]
    return jnp.where(is_ctx, mc, mb)


def _mod_kernel(cond_ref, w_ref, b_ref, o_ref):
    s = _silu(cond_ref[...])
    o_ref[0] = jnp.dot(s, w_ref[0], precision=HIGHEST, preferred_element_type=F32) + b_ref[0]


def _mod_table(cond, w_mod, b_mod):
    n_layers, d, width = w_mod.shape
    cb = 1536
    return pl.pallas_call(
        _mod_kernel,
        out_shape=jax.ShapeDtypeStruct((n_layers, MOD_ROWS, width), F32),
        grid=(n_layers, width // cb),
        in_specs=[pl.BlockSpec((MOD_ROWS, d), lambda l, j: (0, 0)),
                  pl.BlockSpec((1, d, cb), lambda l, j: (l, 0, j)),
                  pl.BlockSpec((1, 1, cb), lambda l, j: (l, 0, j))],
        out_specs=pl.BlockSpec((1, MOD_ROWS, cb), lambda l, j: (l, 0, j)),
        compiler_params=pltpu.CompilerParams(dimension_semantics=("arbitrary", "arbitrary"),
                                             vmem_limit_bytes=VMEM_LIMIT),
        name="mod_table",
    )(cond, w_mod, b_mod.reshape(n_layers, 1, width))


def _in_kernel(x_ref, mod_ref, g_ref, w_ref, *out_refs, rows, ctx_len):
    b = pl.program_id(0)
    t = pl.program_id(1)
    xn = _rms(x_ref[0]) * g_ref[...]
    shift = _row_select(mod_ref, b, t, rows, ctx_len, 0)
    scale = _row_select(mod_ref, b, t, rows, ctx_len, D_MODEL)
    h = (xn * (1.0 + scale) + shift).astype(BF16)
    for o_ref, (_, c0, c1) in zip(out_refs, IN_COLS):
        o_ref[0] = jnp.dot(h, w_ref[:, c0:c1], preferred_element_type=F32)


def _in_proj(xa, mod, g, w_pad, ctx_len, rows):
    bsz, t_all, d = xa.shape
    kern = functools.partial(_in_kernel, rows=rows, ctx_len=ctx_len)
    return pl.pallas_call(
        kern,
        out_shape=[jax.ShapeDtypeStruct((bsz, t_all, c1 - c0), F32) for _, c0, c1 in IN_COLS],
        grid=(bsz, t_all // rows),
        in_specs=[pl.BlockSpec((1, rows, d), lambda b, t: (b, t, 0)),
                  pl.BlockSpec(mod.shape, lambda b, t: (0, 0)),
                  pl.BlockSpec((1, d), lambda b, t: (0, 0)),
                  pl.BlockSpec(w_pad.shape, lambda b, t: (0, 0))],
        out_specs=[pl.BlockSpec((1, rows, c1 - c0), lambda b, t: (b, t, 0)) for _, c0, c1 in IN_COLS],
        compiler_params=pltpu.CompilerParams(dimension_semantics=("arbitrary", "arbitrary"),
                                             vmem_limit_bytes=VMEM_LIMIT),
        name="in_proj",
    )(xa, mod, g, w_pad)


def _sgu_kernel(p_ref, w_ref, bias_ref, g_ref, beta_ref, o_ref, *, n_chunks):
    lane = lax.broadcasted_iota(jnp.int32, (1, SGU_WIDTH), 1)
    head_of_lane = lane // SGU_HEAD_DIM
    w = w_ref[...]
    bias = bias_ref[...]
    c0 = 0.7978845608028654
    for c in range(n_chunks):
        p = p_ref[0, c * CHUNK:(c + 1) * CHUNK, :]
        ge = 0.5 * p * (1.0 + jnp.tanh(c0 * (p + 0.044715 * (p * p * p))))
        u = ge[:, :SGU_WIDTH]
        v = ge[:, SGU_WIDTH:]
        mu = jnp.mean(v, axis=-1, keepdims=True)
        vc = v - mu
        vn = vc * lax.rsqrt(jnp.mean(vc * vc, axis=-1, keepdims=True) + EPS) * g_ref[...] + beta_ref[...]
        stacked = jnp.concatenate(
            [jnp.where(head_of_lane == h, vn, 0.0).astype(BF16) for h in range(SGU_HEADS)], axis=0)
        mixed = jnp.dot(w, stacked, preferred_element_type=F32) + bias
        o_ref[0, c * CHUNK:(c + 1) * CHUNK, :] = (u * mixed).astype(BF16)


def _sgu(p_sgu, w_cat, bias, g, beta, rows):
    bsz, t_all, width = p_sgu.shape
    kern = functools.partial(_sgu_kernel, n_chunks=rows // CHUNK)
    const = lambda b, t: (0, 0)
    return pl.pallas_call(
        kern,
        out_shape=jax.ShapeDtypeStruct((bsz, t_all, SGU_WIDTH), BF16),
        grid=(bsz, t_all // rows),
        in_specs=[pl.BlockSpec((1, rows, width), lambda b, t: (b, t, 0)),
                  pl.BlockSpec(w_cat.shape, const), pl.BlockSpec(bias.shape, const),
                  pl.BlockSpec(g.shape, const), pl.BlockSpec(beta.shape, const)],
        out_specs=pl.BlockSpec((1, rows, SGU_WIDTH), lambda b, t: (b, t, 0)),
        compiler_params=pltpu.CompilerParams(dimension_semantics=("arbitrary", "arbitrary"),
                                             vmem_limit_bytes=VMEM_LIMIT),
        name="sgu",
    )(p_sgu, w_cat, bias, g, beta)


def _ssd_kernel(z_ref, xbc_ref, dt_ref, cw_ref, cb_ref, dtb_ref, alog_ref, skip_ref, g_ref, o_ref,
                xc_ref, yf_ref, yb_ref, st_ref, *, n_blk, n_ctx_blk):
    def conv_block(blk, seg_start, seg_end):
        r0 = _aligned(blk * CHUNK, CHUNK)
        for cg in range(SSD_CONV_CH // LANES):
            cols = slice(cg * LANES, (cg + 1) * LANES)
            zeros = jnp.zeros((8, LANES), F32)
            top = zeros if seg_start else xbc_ref[0, pl.ds(_aligned(r0 - 8, 8), 8), cols]
            bot = zeros if seg_end else xbc_ref[0, pl.ds(_aligned(r0 + CHUNK, 8), 8), cols]
            xw = jnp.concatenate([top, xbc_ref[0, pl.ds(r0, CHUNK), cols], bot], axis=0)
            acc = cb_ref[:, cols] + cw_ref[0:1, cols] * xw[6:6 + CHUNK]
            for k in range(1, SSD_CONV):
                acc = acc + cw_ref[k:k + 1, cols] * xw[6 + k:6 + k + CHUNK]
            xc_ref[pl.ds(r0, CHUNK), cols] = _silu(acc)

    static_blocks = sorted(set(list(range(n_ctx_blk)) + [n_ctx_blk, n_blk - 1]))
    for blk in static_blocks:
        conv_block(blk, blk == 0 or blk == n_ctx_blk, blk == n_ctx_blk - 1 or blk == n_blk - 1)
    if n_blk - 1 > n_ctx_blk + 1:
        def conv_body(blk, carry):
            conv_block(blk, False, False)
            return carry
        lax.fori_loop(n_ctx_blk + 1, n_blk - 1, conv_body, 0)

    li = lax.broadcasted_iota(jnp.int32, (CHUNK, CHUNK), 0)
    si = lax.broadcasted_iota(jnp.int32, (CHUNK, CHUNK), 1)
    a_neg = -jnp.exp(alog_ref[...])

    def chunk_step(c, direction):
        r0 = pl.multiple_of(c * CHUNK, CHUNK)
        rows = pl.ds(r0, CHUNK)
        mask = (si <= li) if direction == 0 else (si >= li)
        tri = jnp.where(mask, 1.0, 0.0).astype(BF16)
        dtr = dt_ref[0, rows, :] + dtb_ref[...]
        dt = jnp.maximum(dtr, 0.0) + jnp.log1p(jnp.exp(-jnp.abs(dtr)))
        adt = dt * a_neg
        p1 = adt.astype(BF16)
        r1 = adt - p1.astype(F32)
        p2 = r1.astype(BF16)
        p3 = (r1 - p2.astype(F32)).astype(BF16)
        parts = jnp.dot(tri, jnp.concatenate([p1, p2, p3], axis=1), preferred_element_type=F32)
        acs = parts[:, :LANES] + parts[:, LANES:2 * LANES] + parts[:, 2 * LANES:]
        end = CHUNK - 1 if direction == 0 else 0
        tot = acs[end:end + 1, :]
        to_end_dt = jnp.exp(tot - acs) * dt
        chunk_decay = jnp.exp(tot)
        per_head_rows = jnp.where(si < DT_REP, acs, jnp.where(si < 2 * DT_REP, dt, to_end_dt)).T
        first_half = si < SSD_HEAD_DIM
        cbs, bm_ts, y_offs = [], [], []
        state = st_ref[direction]
        state16 = state.astype(BF16)
        for g in range(SSD_GROUPS):
            bm = xc_ref[rows, SSD_WIDTH + g * SSD_STATE:SSD_WIDTH + (g + 1) * SSD_STATE]
            cm = xc_ref[rows, SSD_WIDTH + SSD_GROUPS * SSD_STATE + g * SSD_STATE:
                        SSD_WIDTH + SSD_GROUPS * SSD_STATE + (g + 1) * SSD_STATE]
            cm16 = cm.astype(BF16)
            cbs.append(lax.dot_general(cm16, bm.astype(BF16), (((1,), (1,)), ((), ())),
                                       preferred_element_type=F32))
            bm_ts.append(bm.T)
            y_offs.append(jnp.dot(cm16, state16, preferred_element_type=F32))
        y_pairs, state_pairs = [], []
        for k in range(SSD_HEADS // 2):
            pair = slice(k * LANES, (k + 1) * LANES)
            xs16 = xc_ref[rows, pair].astype(BF16)
            y_diag, contrib, exp_a, dec, y_off = [], [], [], [], []
            for hh in (2 * k, 2 * k + 1):
                g = hh // SSD_HPG
                col = direction * SSD_HEADS + hh
                a_col = jnp.broadcast_to(acs[:, col:col + 1], (CHUNK, CHUNK))
                seg = a_col - per_head_rows[col:col + 1, :]
                decay = jnp.exp(jnp.where(mask, seg, -1e30))
                m = (cbs[g] * decay * per_head_rows[DT_REP + col:DT_REP + col + 1, :]).astype(BF16)
                y_diag.append(jnp.dot(m, xs16, preferred_element_type=F32))
                exp_a.append(jnp.exp(a_col))
                y_off.append(y_offs[g][:, pair])
                lhs = (bm_ts[g] * per_head_rows[2 * DT_REP + col:2 * DT_REP + col + 1, :]).astype(BF16)
                contrib.append(jnp.dot(lhs, xs16, preferred_element_type=F32))
                dec.append(jnp.broadcast_to(chunk_decay[:, col:col + 1], (SSD_STATE, LANES)))
            pick = lambda ab: jnp.where(first_half, ab[0], ab[1])
            y_pairs.append(pick(y_diag) + pick(exp_a) * pick(y_off))
            state_pairs.append(pick(dec) * state[:, pair] + pick(contrib))
        st_ref[direction] = jnp.concatenate(state_pairs, axis=1)
        y = jnp.concatenate(y_pairs, axis=1)
        if direction == 0:
            yf_ref[rows, :] = skip_ref[...] * xc_ref[rows, 0:SSD_WIDTH] + y
        else:
            yb_ref[rows, :] = y

    def scan_body(step, carry):
        chunk_step(step, 0)
        chunk_step(jnp.where(step < n_ctx_blk, n_ctx_blk - 1 - step, n_blk - 1 - (step - n_ctx_blk)), 1)
        return carry

    st_ref[...] = jnp.zeros(st_ref.shape, F32)
    lax.fori_loop(0, n_blk, scan_body, 0)

    def out_body(c, carry):
        rows = pl.ds(pl.multiple_of(c * CHUNK, CHUNK), CHUNK)
        gated = (yf_ref[rows, :] + yb_ref[rows, :]) * _silu(z_ref[0, rows, :])
        o_ref[0, rows, :] = (_rms(gated) * g_ref[...]).astype(BF16)
        return carry

    lax.fori_loop(0, n_blk, out_body, 0)


def _ssd(p_z, p_xbc, p_dt, conv_w, conv_b, dt_bias, a_log, skip, g, ctx_len):
    bsz, t_all, _ = p_z.shape
    kern = functools.partial(_ssd_kernel, n_blk=t_all // CHUNK, n_ctx_blk=ctx_len // CHUNK)
    const = lambda b: (0, 0)
    per_b = lambda w: pl.BlockSpec((1, t_all, w), lambda b: (b, 0, 0))
    return pl.pallas_call(
        kern,
        out_shape=jax.ShapeDtypeStruct((bsz, t_all, SSD_WIDTH), BF16),
        grid=(bsz,),
        in_specs=[per_b(SSD_WIDTH), per_b(SSD_CONV_CH), per_b(LANES),
                  pl.BlockSpec(conv_w.shape, const), pl.BlockSpec(conv_b.shape, const),
                  pl.BlockSpec(dt_bias.shape, const), pl.BlockSpec(a_log.shape, const),
                  pl.BlockSpec(skip.shape, const), pl.BlockSpec(g.shape, const)],
        out_specs=per_b(SSD_WIDTH),
        scratch_shapes=[pltpu.VMEM((t_all, SSD_CONV_CH), F32), pltpu.VMEM((t_all, SSD_WIDTH), F32),
                        pltpu.VMEM((t_all, SSD_WIDTH), F32),
                        pltpu.VMEM((2, SSD_STATE, SSD_WIDTH), F32)],
        compiler_params=pltpu.CompilerParams(dimension_semantics=("arbitrary",), vmem_limit_bytes=VMEM_LIMIT),
        name="ssd",
    )(p_z, p_xbc, p_dt, conv_w, conv_b, dt_bias, a_log, skip, g)


def _mla_proj_kernel(pq_ref, pkv_ref, pkr_ref, gq_ref, wq_ref, gkv_ref, wk_ref, wv_ref, cos_ref, sa_ref, sb_ref,
                     q_ref, k_ref, v_ref):
    cos = cos_ref[...]
    sin_a = sa_ref[...]
    sin_b = sb_ref[...]

    def rope(t):
        return t * cos + pltpu.roll(t, 8, 1) * sin_a + pltpu.roll(t, LANES - 8, 1) * sin_b

    qn = (_rms(pq_ref[0]) * gq_ref[...]).astype(BF16)
    q = jnp.dot(qn, wq_ref[...], preferred_element_type=F32)
    kvn = (_rms(pkv_ref[0]) * gkv_ref[...]).astype(BF16)
    k = jnp.dot(kvn, wk_ref[...], preferred_element_type=F32)
    v_ref[0] = lax.dot_general(wv_ref[...], kvn, (((1,), (1,)), ((), ())),
                               preferred_element_type=F32).astype(BF16)
    kr = rope(pkr_ref[0])
    for h in range(MLA_HEADS):
        slot = slice(h * HEAD_SLOT, (h + 1) * HEAD_SLOT)
        q_ref[0, :, slot] = (rope(q[:, slot]) * (MLA_SCALE * LOG2_E)).astype(BF16)
        k_ref[0, :, slot] = (k[:, slot] + kr).astype(BF16)


def _mla_proj(p_q, p_kv, p_kr, g_q, w_q, g_kv, w_k, w_v, cos, sin_a, sin_b, rows):
    bsz, t_all, _ = p_q.shape
    width = MLA_HEADS * HEAD_SLOT
    const = lambda b, t: (0, 0)
    tok = lambda w: pl.BlockSpec((1, rows, w), lambda b, t: (b, t, 0))
    tab = pl.BlockSpec((rows, LANES), lambda b, t: (t, 0))
    out = jax.ShapeDtypeStruct((bsz, t_all, width), BF16)
    out_vt = jax.ShapeDtypeStruct((bsz, MLA_WIDTH, t_all), BF16)
    return pl.pallas_call(
        _mla_proj_kernel,
        out_shape=[out, out, out_vt],
        grid=(bsz, t_all // rows),
        in_specs=[tok(Q_LORA), tok(KV_LORA), tok(LANES),
                  pl.BlockSpec(g_q.shape, const), pl.BlockSpec(w_q.shape, const),
                  pl.BlockSpec(g_kv.shape, const), pl.BlockSpec(w_k.shape, const), pl.BlockSpec(w_v.shape, const),
                  tab, tab, tab],
        out_specs=[tok(width), tok(width), pl.BlockSpec((1, MLA_WIDTH, rows), lambda b, t: (b, 0, t))],
        compiler_params=pltpu.CompilerParams(dimension_semantics=("arbitrary", "arbitrary"),
                                             vmem_limit_bytes=VMEM_LIMIT),
        name="mla_proj",
    )(p_q, p_kv, p_kr, g_q, w_q, g_kv, w_k, w_v, cos, sin_a, sin_b)


ATTN_Q = 512


def _attn_kernel(*refs, n_q_blocks):
    q_refs = refs[:n_q_blocks]
    k_ref, vt_ref, o_ref = refs[n_q_blocks:]
    outs = []
    for h in range(MLA_HEADS):
        slot = slice(h * HEAD_SLOT, (h + 1) * HEAD_SLOT)
        qh = jnp.concatenate([r[0, :, slot] for r in q_refs], axis=0)
        s_t = lax.dot_general(k_ref[0, :, slot], qh, (((1,), (1,)), ((), ())),
                              preferred_element_type=F32)
        p_t = jnp.exp2(s_t - jnp.max(s_t, axis=0, keepdims=True))
        denom = jnp.sum(p_t, axis=0, keepdims=True)
        o_t = jnp.dot(vt_ref[0, h * MLA_V:(h + 1) * MLA_V, :], p_t.astype(BF16),
                      preferred_element_type=F32)
        outs.append(o_t / denom)
    o_ref[0] = jnp.concatenate(outs, axis=0).T.astype(BF16)


def _attention(q, k, vt, ctx_len, latent):
    bsz, t_all, width = q.shape
    if latent:
        n_q_blocks = ATTN_Q // ctx_len
        n_rows, n_keys, tq = t_all - ctx_len, t_all, ATTN_Q
        q_specs = [pl.BlockSpec((1, ctx_len, width),
                                functools.partial(lambda b, i, j: (b, n_q_blocks * i + 1 + j, 0), j=j))
                   for j in range(n_q_blocks)]
    else:
        n_q_blocks = 1
        n_rows, n_keys, tq = ctx_len, ctx_len, ctx_len
        q_specs = [pl.BlockSpec((1, ctx_len, width), lambda b, i: (b, 0, 0))]
    assert n_rows % tq == 0
    return pl.pallas_call(
        functools.partial(_attn_kernel, n_q_blocks=n_q_blocks),
        out_shape=jax.ShapeDtypeStruct((bsz, n_rows, MLA_WIDTH), BF16),
        grid=(bsz, n_rows // tq),
        in_specs=q_specs + [pl.BlockSpec((1, n_keys, width), lambda b, i: (b, 0, 0)),
                            pl.BlockSpec((1, MLA_WIDTH, n_keys), lambda b, i: (b, 0, 0))],
        out_specs=pl.BlockSpec((1, tq, MLA_WIDTH), lambda b, i: (b, i, 0)),
        compiler_params=pltpu.CompilerParams(dimension_semantics=("arbitrary", "arbitrary"),
                                             vmem_limit_bytes=VMEM_LIMIT),
        name="attention",
    )(*([q] * n_q_blocks), k, vt)


def _out_kernel(x_ref, o1_ref, o2_ref, o3_ref, *rest, rows, ctx_len, tile0, route, ctx_attn):
    if ctx_attn:
        o3c_ref, *rest = rest
    mod_ref, w1_ref, w2_ref, w3_ref, g_ref, *rest = rest
    if route:
        wr_ref, xo_ref, h_ref, gate_ref = rest
    else:
        xo_ref, h_ref = rest
    b = pl.program_id(0)
    t = pl.program_id(1) + tile0
    o3 = o3_ref[0]
    if ctx_attn:
        o3 = jnp.where(t == 0, o3c_ref[0], o3)
    mix = (jnp.dot(o1_ref[0], w1_ref[...], preferred_element_type=F32)
           + jnp.dot(o2_ref[0], w2_ref[...], preferred_element_type=F32)
           + jnp.dot(o3, w3_ref[...], preferred_element_type=F32))
    x = x_ref[0] + _row_select(mod_ref, b, t, rows, ctx_len, 2 * D_MODEL) * mix
    xo_ref[0] = x
    shift = _row_select(mod_ref, b, t, rows, ctx_len, 3 * D_MODEL)
    scale = _row_select(mod_ref, b, t, rows, ctx_len, 4 * D_MODEL)
    h = _rms(x) * g_ref[...] * (1.0 + scale) + shift
    h_ref[0] = h.astype(BF16)
    if route:
        h_hi = h.astype(BF16)
        h_lo = (h - h_hi.astype(F32)).astype(BF16)
        by_hi = jnp.dot(h_hi, wr_ref[...], preferred_element_type=F32)
        logits = (by_hi[:, :LANES] + by_hi[:, LANES:]
                  + jnp.dot(h_lo, wr_ref[:, :LANES], preferred_element_type=F32))
        lane = lax.broadcasted_iota(jnp.int32, logits.shape, 1)
        lane_f = lane.astype(F32)
        lg = jnp.where(lane < N_EXPERTS, logits, -jnp.inf)
        m1 = jnp.max(lg, axis=-1, keepdims=True)
        i1 = jnp.min(jnp.where(lg == m1, lane_f, float(LANES)), axis=-1, keepdims=True)
        lg2 = jnp.where(lane_f == i1, -jnp.inf, lg)
        m2 = jnp.max(lg2, axis=-1, keepdims=True)
        i2 = jnp.min(jnp.where(lg2 == m2, lane_f, float(LANES)), axis=-1, keepdims=True)
        e2 = jnp.exp(m2 - m1)
        w_top = 1.0 / (1.0 + e2)
        gate_ref[0] = jnp.where(lane_f == i1, w_top, 0.0) + jnp.where(lane_f == i2, e2 * w_top, 0.0)


def _out_proj(xa, o_sgu, o_ssd, o_mla, o_mla_ctx, mod, w1, w2, w3, g, ctx_len, rows, w_router=None):
    bsz, t_all, d = xa.shape
    assert rows == ctx_len
    route = w_router is not None
    latent_only = o_mla_ctx is None
    tile0 = 1 if latent_only else 0
    n_out = t_all - (ctx_len if latent_only else 0)
    kern = functools.partial(_out_kernel, rows=rows, ctx_len=ctx_len, tile0=tile0, route=route,
                             ctx_attn=not latent_only)
    const = lambda b, t: (0, 0)
    tok = lambda w: pl.BlockSpec((1, rows, w), lambda b, t: (b, t + tile0, 0))
    out_tok = lambda w: pl.BlockSpec((1, rows, w), lambda b, t: (b, t, 0))
    lat_tok = pl.BlockSpec((1, rows, MLA_WIDTH), lambda b, t: (b, jnp.maximum(t + tile0 - 1, 0), 0))
    in_specs = [tok(d), tok(SGU_WIDTH), tok(SSD_WIDTH), lat_tok]
    args = [xa, o_sgu, o_ssd, o_mla]
    if not latent_only:
        in_specs.append(pl.BlockSpec((1, rows, MLA_WIDTH), lambda b, t: (b, 0, 0)))
        args.append(o_mla_ctx)
    in_specs += [pl.BlockSpec(mod.shape, const), pl.BlockSpec(w1.shape, const), pl.BlockSpec(w2.shape, const),
                 pl.BlockSpec(w3.shape, const), pl.BlockSpec(g.shape, const)]
    args += [mod, w1, w2, w3, g]
    out_shape = [jax.ShapeDtypeStruct((bsz, n_out, d), F32), jax.ShapeDtypeStruct((bsz, n_out, d), BF16)]
    out_specs = [out_tok(d), out_tok(d)]
    if route:
        in_specs.append(pl.BlockSpec(w_router.shape, const))
        out_shape.append(jax.ShapeDtypeStruct((bsz, n_out, LANES), F32))
        out_specs.append(out_tok(LANES))
        args.append(w_router)
    return pl.pallas_call(
        kern, out_shape=out_shape, grid=(bsz, n_out // rows), in_specs=in_specs, out_specs=out_specs,
        compiler_params=pltpu.CompilerParams(dimension_semantics=("arbitrary", "arbitrary"),
                                             vmem_limit_bytes=VMEM_LIMIT),
        name="out_proj",
    )(*args)


FF_SPLITS = ((0, 1536), (1536, D_FF))


def _ffn_kernel(x_ref, h_ref, mod_ref, wg_ref, wu_ref, wd_ref, o_ref, *, rows, ctx_len):
    b = pl.program_id(0)
    t = pl.program_id(1)
    h = h_ref[0]
    acc = None
    for c0, c1 in FF_SPLITS:
        gate = jnp.dot(h, wg_ref[:, c0:c1], preferred_element_type=F32)
        up = jnp.dot(h, wu_ref[:, c0:c1], preferred_element_type=F32)
        part = jnp.dot((_silu(gate) * up).astype(BF16), wd_ref[c0:c1, :], preferred_element_type=F32)
        acc = part if acc is None else acc + part
    o_ref[0] = x_ref[0] + _row_select(mod_ref, b, t, rows, ctx_len, 5 * D_MODEL) * acc


def _ffn_dense(xa, h, mod, wg, wu, wd, ctx_len, rows):
    bsz, t_all, d = xa.shape
    kern = functools.partial(_ffn_kernel, rows=rows, ctx_len=ctx_len)
    const = lambda b, t: (0, 0)
    tok = pl.BlockSpec((1, rows, d), lambda b, t: (b, t, 0))
    return pl.pallas_call(
        kern,
        out_shape=jax.ShapeDtypeStruct((bsz, t_all, d), F32),
        grid=(bsz, t_all // rows),
        in_specs=[tok, tok, pl.BlockSpec(mod.shape, const),
                  pl.BlockSpec(wg.shape, const), pl.BlockSpec(wu.shape, const), pl.BlockSpec(wd.shape, const)],
        out_specs=tok,
        compiler_params=pltpu.CompilerParams(dimension_semantics=("arbitrary", "arbitrary"),
                                             vmem_limit_bytes=VMEM_LIMIT),
        name="ffn_dense",
    )(xa, h, mod, wg, wu, wd)


MOE_TILE = 1024
MOE_BLOCK = 144
MOE_SLOTS = 256


def _plan_kernel(gate_ref, rank_ref, rank_rows_ref, count_ref):
    rows = gate_ref.shape[0]
    routed = gate_ref[...] > 0.0
    ti = lax.broadcasted_iota(jnp.int32, (rows, rows), 0)
    tj = lax.broadcasted_iota(jnp.int32, (rows, rows), 1)
    earlier = jnp.where(tj < ti, 1.0, 0.0).astype(BF16)
    ones = jnp.where(routed, 1.0, 0.0)
    before = jnp.dot(earlier, ones.astype(BF16), preferred_element_type=F32)
    rank = jnp.where(routed, before, -1.0)
    rank_ref[...] = rank
    rank_rows_ref[0] = rank.T[0:N_EXPERTS, :]
    count_ref[0] = jnp.broadcast_to(jnp.sum(ones, axis=0, keepdims=True), (8, LANES)).astype(jnp.int32)


def _route_plan(gates):
    n_tok = gates.shape[0]
    n_tiles = n_tok // MOE_TILE
    return pl.pallas_call(
        _plan_kernel,
        out_shape=[jax.ShapeDtypeStruct((n_tok, LANES), F32),
                   jax.ShapeDtypeStruct((n_tiles, N_EXPERTS, MOE_TILE), F32),
                   jax.ShapeDtypeStruct((n_tiles, 8, LANES), jnp.int32)],
        grid=(n_tiles,),
        in_specs=[pl.BlockSpec((MOE_TILE, LANES), lambda i: (i, 0))],
        out_specs=[pl.BlockSpec((MOE_TILE, LANES), lambda i: (i, 0)),
                   pl.BlockSpec((1, N_EXPERTS, MOE_TILE), lambda i: (i, 0, 0)),
                   pl.BlockSpec((1, 8, LANES), lambda i: (i, 0, 0))],
        compiler_params=pltpu.CompilerParams(dimension_semantics=("arbitrary",), vmem_limit_bytes=VMEM_LIMIT),
        name="route_plan",
    )(gates)


def _moe_kernel(count_ref, x_ref, h_ref, gate_ref, rank_ref, rank_rows_ref, mod_ref, wg_ref, wu_ref, wd_ref, gf_ref,
                o_ref, *, tiles_per_batch):
    i = pl.program_id(0)
    e = pl.program_id(1)
    rows = h_ref.shape[0]
    d = h_ref.shape[1]
    lane = lax.broadcasted_iota(jnp.int32, (rows, LANES), 1)
    ge = jnp.sum(jnp.where(lane == e, gate_ref[...], 0.0), axis=-1, keepdims=True)
    rank_lanes = jnp.broadcast_to(jnp.sum(jnp.where(lane == e, rank_ref[...], 0.0), axis=-1, keepdims=True),
                                  (rows, MOE_SLOTS))
    rank_row = rank_rows_ref[0, pl.ds(e, 1), :]

    @pl.when(e == 0)
    def _():
        o_ref[...] = jnp.zeros(o_ref.shape, F32)

    slot_sub = lax.broadcasted_iota(jnp.int32, (MOE_BLOCK, rows), 0)
    slot_lane = lax.broadcasted_iota(jnp.int32, (1, MOE_SLOTS), 1)

    def block(j, carry):
        lo = j * MOE_BLOCK
        pick = jnp.where(rank_row == (slot_sub + lo).astype(F32), 1.0, 0.0).astype(BF16)
        hg = jnp.dot(pick, h_ref[...], preferred_element_type=F32).astype(BF16)
        gate = jnp.dot(hg, wg_ref[0], preferred_element_type=F32)
        up = jnp.dot(hg, wu_ref[0], preferred_element_type=F32)
        y = jnp.dot((_silu(gate) * up).astype(BF16), wd_ref[0], preferred_element_type=F32).astype(BF16)
        y = jnp.concatenate([y, jnp.zeros((MOE_SLOTS - MOE_BLOCK, d), BF16)], axis=0)
        target = jnp.where(slot_lane < MOE_BLOCK, slot_lane + lo, -2).astype(F32)
        place = jnp.where(rank_lanes == target, 1.0, 0.0).astype(BF16)
        o_ref[...] += ge * jnp.dot(place, y, preferred_element_type=F32)
        return carry

    n_blocks = (count_ref[i * N_EXPERTS + e] + (MOE_BLOCK - 1)) // MOE_BLOCK
    lax.fori_loop(0, n_blocks, block, 0)

    @pl.when(e == N_EXPERTS - 1)
    def _():
        gate5 = mod_ref[pl.ds(i // tiles_per_batch, 1), 5 * D_MODEL:6 * D_MODEL]
        o_ref[...] = _rms(x_ref[...] + gate5 * o_ref[...]) * gf_ref[...]


def _moe_final(x, h, gates, mod, wg, wu, wd, g_final):
    bsz, n, d = x.shape
    assert n % MOE_TILE == 0
    n_tok = bsz * n
    gates = gates.reshape(n_tok, LANES)
    rank, rank_rows, counts = _route_plan(gates)
    counts = counts[:, 0, :N_EXPERTS].reshape(-1)
    kern = functools.partial(_moe_kernel, tiles_per_batch=n // MOE_TILE)
    const = lambda i, e, bnd: (0, 0)
    tok = lambda w: pl.BlockSpec((MOE_TILE, w), lambda i, e, bnd: (i, 0))
    per_expert = lambda a, b: pl.BlockSpec((1, a, b), lambda i, e, bnd: (e, 0, 0))
    out = pl.pallas_call(
        kern,
        out_shape=jax.ShapeDtypeStruct((n_tok, d), F32),
        grid_spec=pltpu.PrefetchScalarGridSpec(
            num_scalar_prefetch=1,
            grid=(n_tok // MOE_TILE, N_EXPERTS),
            in_specs=[tok(d), tok(d), tok(LANES), tok(LANES),
                      pl.BlockSpec((1, N_EXPERTS, MOE_TILE), lambda i, e, bnd: (i, 0, 0)),
                      pl.BlockSpec(mod.shape, const),
                      per_expert(d, D_FF_EXPERT), per_expert(d, D_FF_EXPERT), per_expert(D_FF_EXPERT, d),
                      pl.BlockSpec(g_final.shape, const)],
            out_specs=tok(d)),
        compiler_params=pltpu.CompilerParams(dimension_semantics=("arbitrary", "arbitrary"),
                                             vmem_limit_bytes=VMEM_LIMIT),
        name="moe_final",
    )(counts, x.reshape(n_tok, d), h.reshape(n_tok, d), gates, rank, rank_rows, mod, wg, wu, wd, g_final)
    return out.reshape(bsz, n, d)


def _pad_in_weight(w):
    d = w.shape[0]
    z = lambda n: jnp.zeros((d, n), w.dtype)
    dt0, dt1 = 1792, 1804
    kr0 = dt1 + Q_LORA + KV_LORA
    dt_slot = _dt_lanes(w[:, dt0:dt1])
    return jnp.concatenate([w[:, :dt0], dt_slot, w[:, dt1:kr0], z(ROPE_LANE0), w[:, kr0:],
                            z(LANES - ROPE_LANE0 - MLA_ROPE)], axis=1).astype(BF16)


def _dt_lanes(v):
    gap = jnp.zeros(v.shape[:-1] + (DT_REP - v.shape[-1],), v.dtype)
    tail = jnp.zeros(v.shape[:-1] + (LANES - 3 * DT_REP,), v.dtype)
    return jnp.concatenate([v, gap, v, gap, v, gap, tail], axis=-1)


def _head_slots(w, per_head, take0, take1, slot=HEAD_SLOT):
    k = w.shape[0]
    w = w.reshape(k, MLA_HEADS, per_head)[:, :, take0:take1]
    w = jnp.pad(w, ((0, 0), (0, 0), (0, slot - (take1 - take0))))
    return w.reshape(k, MLA_HEADS * slot).astype(BF16)


def _rope_tables(n_latent, ctx_len):
    t = jnp.arange(n_latent)
    pos = jnp.stack([(t // GRID_W).astype(F32), (t % GRID_W).astype(F32)], axis=1)
    n_freq = MLA_ROPE // 4
    inv_freq = ROPE_THETA ** (-jnp.arange(n_freq, dtype=F32) / n_freq)
    ang = pos[:, :, None] * inv_freq
    cos, sin = jnp.cos(ang), jnp.sin(ang)
    zero = jnp.zeros_like(sin)
    cos_r = jnp.stack([cos, cos], axis=2).reshape(n_latent, MLA_ROPE)
    sa_r = jnp.stack([zero, sin], axis=2).reshape(n_latent, MLA_ROPE)
    sb_r = jnp.stack([-sin, zero], axis=2).reshape(n_latent, MLA_ROPE)

    def slot(r, fill):
        lat = jnp.concatenate([jnp.full((n_latent, ROPE_LANE0), fill, F32), r,
                               jnp.full((n_latent, LANES - ROPE_LANE0 - MLA_ROPE), fill, F32)], axis=1)
        return jnp.concatenate([jnp.full((ctx_len, LANES), fill, F32), lat], axis=0)

    return slot(cos_r, 1.0), slot(sa_r, 0.0), slot(sb_r, 0.0)


def kernel(x, c, ctx, c_ctx, w_mod, b_mod, g_mix, w_in, w_sgu, b_sgu, g_sgu, beta_sgu, conv_w, conv_b, dt_bias,
           a_log, d_skip, g_ssd, g_q, w_uq, g_kv, w_ukv, w_out, g_ffn, w_gate, w_up, w_down, w_router, w_gate_e,
           w_up_e, w_down_e, g_final):
    bsz, n, d = x.shape
    ctx_len = ctx.shape[1]
    depth = w_in.shape[0]
    assert bsz <= CTX_MOD_ROW and ctx_len % CHUNK == 0 and n % CHUNK == 0

    cond = jnp.concatenate([c, jnp.zeros((CTX_MOD_ROW - bsz, d), F32), c_ctx[None],
                            jnp.zeros((MOD_ROWS - CTX_MOD_ROW - 1, d), F32)], axis=0)
    mod_all = _mod_table(cond, w_mod, b_mod)
    cos, sin_a, sin_b = _rope_tables(n, ctx_len)
    xa = jnp.concatenate([ctx, x], axis=1)
    t_all = ctx_len + n
    rows_in = 768 if t_all % 768 == 0 else ctx_len

    out = None
    for layer in range(depth):
        last = layer == depth - 1
        mod = mod_all[layer]
        row = lambda v: v.reshape(1, -1)
        p_sgu, p_z, p_xbc, p_dt, p_q, p_kv, p_kr = _in_proj(xa, mod, row(g_mix[layer]), _pad_in_weight(w_in[layer]),
                                                            ctx_len, rows_in)
        w_cat = jnp.transpose(w_sgu[layer], (1, 0, 2)).reshape(CHUNK, SGU_HEADS * CHUNK).astype(BF16)
        bias = jnp.repeat(b_sgu[layer].T, SGU_HEAD_DIM, axis=1)
        o_sgu = _sgu(p_sgu, w_cat, bias, row(g_sgu[layer]), row(beta_sgu[layer]), rows_in)
        o_ssd = _ssd(p_z, p_xbc, p_dt, conv_w[layer], row(conv_b[layer]), _dt_lanes(dt_bias[layer].reshape(1, -1)),
                     _dt_lanes(a_log[layer].reshape(1, -1)), row(jnp.repeat(d_skip[layer], SSD_HEAD_DIM)), row(g_ssd[layer]),
                     ctx_len)
        per_q = MLA_NOPE + MLA_ROPE
        per_kv = MLA_NOPE + MLA_V
        q, k, v = _mla_proj(p_q, p_kv, p_kr, row(g_q[layer]), _head_slots(w_uq[layer], per_q, 0, per_q),
                            row(g_kv[layer]), _head_slots(w_ukv[layer], per_kv, 0, MLA_NOPE),
                            _head_slots(w_ukv[layer], per_kv, MLA_NOPE, per_kv, MLA_V).T, cos, sin_a, sin_b,
                            rows_in)
        o_mla = _attention(q, k, v, ctx_len, latent=True)
        o_mla_ctx = None if last else _attention(q, k, v, ctx_len, latent=False)
        wo = w_out[layer].astype(BF16)
        w1, w2, w3 = wo[:SGU_WIDTH], wo[SGU_WIDTH:SGU_WIDTH + SSD_WIDTH], wo[SGU_WIDTH + SSD_WIDTH:]
        i = layer // 2
        if layer % 2 == 0:
            if last:
                raise NotImplementedError("final dense channel mixer")
            x_mid, h_ffn = _out_proj(xa, o_sgu, o_ssd, o_mla, o_mla_ctx, mod, w1, w2, w3, row(g_ffn[layer]),
                                     ctx_len, ctx_len)
            xa = _ffn_dense(x_mid, h_ffn, mod, w_gate[i].astype(BF16), w_up[i].astype(BF16),
                            w_down[i].astype(BF16), ctx_len, ctx_len)
        else:
            if not last:
                raise NotImplementedError("expert channel mixer on a non-final layer")
            wr_hi = w_router[i].astype(BF16)
            wr_lo = (w_router[i] - wr_hi.astype(F32)).astype(BF16)
            pad = lambda w: jnp.pad(w, ((0, 0), (0, LANES - N_EXPERTS)))
            wr = jnp.concatenate([pad(wr_hi), pad(wr_lo)], axis=1)
            x_mid, h_ffn, gates = _out_proj(xa, o_sgu, o_ssd, o_mla, None, mod, w1, w2, w3, row(g_ffn[layer]),
                                            ctx_len, ctx_len, w_router=wr)
            out = _moe_final(x_mid, h_ffn, gates, mod, w_gate_e[i].astype(BF16), w_up_e[i].astype(BF16),
                             w_down_e[i].astype(BF16), row(g_final))
    return out
```

```python
import functools

import jax
import jax.numpy as jnp
from jax import lax
from jax.experimental import pallas as pl
from jax.experimental.pallas import tpu as pltpu

F32 = jnp.float32
BF16 = jnp.bfloat16
HIGHEST = lax.Precision.HIGHEST

D_MODEL = 1024
EPS = 1e-6
N_MOD = 6
GRID_W = 64
CHUNK = 128

SGU_HEADS = 4
SGU_HEAD_DIM = 64
SGU_WIDTH = 256

SSD_HEADS = 6
SSD_HEAD_DIM = 64
SSD_WIDTH = 384
SSD_GROUPS = 2
SSD_HPG = 3
SSD_STATE = 128
SSD_CONV = 5
SSD_CONV_CH = 896
SSD_GW = SSD_HPG * SSD_HEAD_DIM
DT_REP = 16

MLA_HEADS = 6
MLA_NOPE = 64
MLA_ROPE = 32
MLA_V = 64
MLA_WIDTH = 384
Q_LORA = 384
KV_LORA = 256
ROPE_THETA = 10000.0
MLA_SCALE = (MLA_NOPE + MLA_ROPE) ** -0.5
LOG2_E = 1.4426950408889634
HEAD_SLOT = 128
ROPE_LANE0 = MLA_NOPE

D_FF = 2816
N_EXPERTS = 8
D_FF_EXPERT = 1408

LANES = 128
MOD_ROWS = 16
CTX_MOD_ROW = 8

IN_COLS = (("sgu", 0, 512), ("z", 512, 896), ("xbc", 896, 1792), ("dt", 1792, 1920), ("cq", 1920, 2304),
           ("ckv", 2304, 2560), ("kr", 2560, 2688))
IN_PAD_WIDTH = 2688

VMEM_LIMIT = 56 * 1024 * 1024


def _sigmoid(x):
    return 1.0 / (1.0 + jnp.exp(-x))


def _silu(x):
    return x * _sigmoid(x)


def _rms(x):
    return x * lax.rsqrt(jnp.mean(x * x, axis=-1, keepdims=True) + EPS)


def _aligned(v, m):
    return v if isinstance(v, int) else pl.multiple_of(v, m)


def _row_select(mod_ref, b, tile, rows, ctx_len, col0):
    row = tile * rows + lax.broadcasted_iota(jnp.int32, (rows, 1), 0)
    is_ctx = row < ctx_len
    mb = mod_ref[pl.ds(b, 1), col0:col0 + D_MODEL]
    mc = mod_ref[CTX_MOD_ROW:CTX_MOD_ROW + 1, col0:col0 + D_MODEL]
    return jnp.where(is_ctx, mc, mb)


def _mod_kernel(cond_ref, w_ref, b_ref, o_ref):
    s = _silu(cond_ref[...])
    o_ref[0] = jnp.dot(s, w_ref[0], precision=HIGHEST, preferred_element_type=F32) + b_ref[0]


def _mod_table(cond, w_mod, b_mod):
    n_layers, d, width = w_mod.shape
    cb = 1536
    return pl.pallas_call(
        _mod_kernel,
        out_shape=jax.ShapeDtypeStruct((n_layers, MOD_ROWS, width), F32),
        grid=(n_layers, width // cb),
        in_specs=[pl.BlockSpec((MOD_ROWS, d), lambda l, j: (0, 0)),
                  pl.BlockSpec((1, d, cb), lambda l, j: (l, 0, j)),
                  pl.BlockSpec((1, 1, cb), lambda l, j: (l, 0, j))],
        out_specs=pl.BlockSpec((1, MOD_ROWS, cb), lambda l, j: (l, 0, j)),
        compiler_params=pltpu.CompilerParams(dimension_semantics=("arbitrary", "arbitrary"),
                                             vmem_limit_bytes=VMEM_LIMIT),
        name="mod_table",
    )(cond, w_mod, b_mod.reshape(n_layers, 1, width))


def _token_specs(tokens, ctx_len, rows, d):
    if not isinstance(tokens, tuple):
        return [pl.BlockSpec((1, rows, d), lambda b, t: (b, t, 0))], [tokens]
    assert rows == ctx_len
    return ([pl.BlockSpec((1, rows, d), lambda b, t: (b, 0, 0)),
             pl.BlockSpec((1, rows, d), lambda b, t: (b, jnp.maximum(t - 1, 0), 0))], list(tokens))


def _tokens_tile(refs, t):
    return refs[0][0] if len(refs) == 1 else jnp.where(t == 0, refs[0][0], refs[1][0])


def _in_kernel(*refs, rows, ctx_len, n_tok_refs):
    tok_refs = refs[:n_tok_refs]
    mod_ref, g_ref, w_ref, *out_refs = refs[n_tok_refs:]
    b = pl.program_id(0)
    t = pl.program_id(1)
    xn = _rms(_tokens_tile(tok_refs, t)) * g_ref[...]
    shift = _row_select(mod_ref, b, t, rows, ctx_len, 0)
    scale = _row_select(mod_ref, b, t, rows, ctx_len, D_MODEL)
    h = (xn * (1.0 + scale) + shift).astype(BF16)
    p = jnp.dot(h, w_ref[...], preferred_element_type=F32)
    for o_ref, (_, c0, c1) in zip(out_refs, IN_COLS):
        o_ref[0] = p[:, c0:c1]


def _in_proj(tokens, t_all, mod, g, w_pad, ctx_len, rows):
    tok_specs, tok_args = _token_specs(tokens, ctx_len, rows, D_MODEL)
    bsz = tok_args[0].shape[0]
    kern = functools.partial(_in_kernel, rows=rows, ctx_len=ctx_len, n_tok_refs=len(tok_args))
    return pl.pallas_call(
        kern,
        out_shape=[jax.ShapeDtypeStruct((bsz, t_all, c1 - c0), F32) for _, c0, c1 in IN_COLS],
        grid=(bsz, t_all // rows),
        in_specs=tok_specs + [pl.BlockSpec(mod.shape, lambda b, t: (0, 0)),
                              pl.BlockSpec((1, D_MODEL), lambda b, t: (0, 0)),
                              pl.BlockSpec(w_pad.shape, lambda b, t: (0, 0))],
        out_specs=[pl.BlockSpec((1, rows, c1 - c0), lambda b, t: (b, t, 0)) for _, c0, c1 in IN_COLS],
        compiler_params=pltpu.CompilerParams(dimension_semantics=("arbitrary", "arbitrary"),
                                             vmem_limit_bytes=VMEM_LIMIT),
        name="in_proj",
    )(*tok_args, mod, g, w_pad)


def _sgu_kernel(p_ref, w_ref, bias_ref, g_ref, beta_ref, o_ref, *, n_chunks):
    lane = lax.broadcasted_iota(jnp.int32, (1, SGU_WIDTH), 1)
    head_of_lane = lane // SGU_HEAD_DIM
    w = w_ref[...]
    bias = bias_ref[...]
    c0 = 0.7978845608028654
    for c in range(n_chunks):
        p = p_ref[0, c * CHUNK:(c + 1) * CHUNK, :]
        ge = 0.5 * p * (1.0 + jnp.tanh(c0 * (p + 0.044715 * (p * p * p))))
        u = ge[:, :SGU_WIDTH]
        v = ge[:, SGU_WIDTH:]
        mu = jnp.mean(v, axis=-1, keepdims=True)
        vc = v - mu
        vn = vc * lax.rsqrt(jnp.mean(vc * vc, axis=-1, keepdims=True) + EPS) * g_ref[...] + beta_ref[...]
        stacked = jnp.concatenate(
            [jnp.where(head_of_lane == h, vn, 0.0).astype(BF16) for h in range(SGU_HEADS)], axis=0)
        mixed = jnp.dot(w, stacked, preferred_element_type=F32) + bias
        o_ref[0, c * CHUNK:(c + 1) * CHUNK, :] = (u * mixed).astype(BF16)


def _sgu(p_sgu, w_cat, bias, g, beta, rows):
    bsz, t_all, width = p_sgu.shape
    kern = functools.partial(_sgu_kernel, n_chunks=rows // CHUNK)
    const = lambda b, t: (0, 0)
    return pl.pallas_call(
        kern,
        out_shape=jax.ShapeDtypeStruct((bsz, t_all, SGU_WIDTH), BF16),
        grid=(bsz, t_all // rows),
        in_specs=[pl.BlockSpec((1, rows, width), lambda b, t: (b, t, 0)),
                  pl.BlockSpec(w_cat.shape, const), pl.BlockSpec(bias.shape, const),
                  pl.BlockSpec(g.shape, const), pl.BlockSpec(beta.shape, const)],
        out_specs=pl.BlockSpec((1, rows, SGU_WIDTH), lambda b, t: (b, t, 0)),
        compiler_params=pltpu.CompilerParams(dimension_semantics=("arbitrary", "arbitrary"),
                                             vmem_limit_bytes=VMEM_LIMIT),
        name="sgu",
    )(p_sgu, w_cat, bias, g, beta)


def _ssd_kernel(z_ref, xbc_ref, dt_ref, cw_ref, cb_ref, dtb_ref, alog_ref, skip_ref, g_ref, o_ref,
                xc_ref, yf_ref, yb_ref, st_ref, *, n_blk, n_ctx_blk):
    def conv_block(blk, seg_start, seg_end):
        r0 = _aligned(blk * CHUNK, CHUNK)
        for cg in range(SSD_CONV_CH // LANES):
            cols = slice(cg * LANES, (cg + 1) * LANES)
            zeros = jnp.zeros((8, LANES), F32)
            top = zeros if seg_start else xbc_ref[0, pl.ds(_aligned(r0 - 8, 8), 8), cols]
            bot = zeros if seg_end else xbc_ref[0, pl.ds(_aligned(r0 + CHUNK, 8), 8), cols]
            xw = jnp.concatenate([top, xbc_ref[0, pl.ds(r0, CHUNK), cols], bot], axis=0)
            acc = cb_ref[:, cols] + cw_ref[0:1, cols] * xw[6:6 + CHUNK]
            for k in range(1, SSD_CONV):
                acc = acc + cw_ref[k:k + 1, cols] * xw[6 + k:6 + k + CHUNK]
            xc_ref[pl.ds(r0, CHUNK), cols] = _silu(acc)

    static_blocks = sorted(set(list(range(n_ctx_blk)) + [n_ctx_blk, n_blk - 1]))
    for blk in static_blocks:
        conv_block(blk, blk == 0 or blk == n_ctx_blk, blk == n_ctx_blk - 1 or blk == n_blk - 1)
    if n_blk - 1 > n_ctx_blk + 1:
        def conv_body(blk, carry):
            conv_block(blk, False, False)
            return carry
        lax.fori_loop(n_ctx_blk + 1, n_blk - 1, conv_body, 0)

    li = lax.broadcasted_iota(jnp.int32, (CHUNK, CHUNK), 0)
    si = lax.broadcasted_iota(jnp.int32, (CHUNK, CHUNK), 1)
    a_neg = -jnp.exp(alog_ref[...])

    def chunk_step(c, direction):
        r0 = pl.multiple_of(c * CHUNK, CHUNK)
        rows = pl.ds(r0, CHUNK)
        mask = (si <= li) if direction == 0 else (si >= li)
        tri = jnp.where(mask, 1.0, 0.0).astype(BF16)
        dtr = dt_ref[0, rows, :] + dtb_ref[...]
        dt = jnp.maximum(dtr, 0.0) + jnp.log1p(jnp.exp(-jnp.abs(dtr)))
        adt = dt * a_neg
        p1 = adt.astype(BF16)
        r1 = adt - p1.astype(F32)
        p2 = r1.astype(BF16)
        p3 = (r1 - p2.astype(F32)).astype(BF16)
        parts = jnp.dot(tri, jnp.concatenate([p1, p2, p3], axis=1), preferred_element_type=F32)
        acs = parts[:, :LANES] + parts[:, LANES:2 * LANES] + parts[:, 2 * LANES:]
        end = CHUNK - 1 if direction == 0 else 0
        tot = acs[end:end + 1, :]
        to_end_dt = jnp.exp(tot - acs) * dt
        chunk_decay = jnp.exp(tot)
        per_head_rows = jnp.where(si < DT_REP, acs, jnp.where(si < 2 * DT_REP, dt, to_end_dt)).T
        first_half = si < SSD_HEAD_DIM
        cbs, bm_ts, y_offs = [], [], []
        state = st_ref[direction]
        state16 = state.astype(BF16)
        for g in range(SSD_GROUPS):
            bm = xc_ref[rows, SSD_WIDTH + g * SSD_STATE:SSD_WIDTH + (g + 1) * SSD_STATE]
            cm = xc_ref[rows, SSD_WIDTH + SSD_GROUPS * SSD_STATE + g * SSD_STATE:
                        SSD_WIDTH + SSD_GROUPS * SSD_STATE + (g + 1) * SSD_STATE]
            cm16 = cm.astype(BF16)
            cbs.append(lax.dot_general(cm16, bm.astype(BF16), (((1,), (1,)), ((), ())),
                                       preferred_element_type=F32))
            bm_ts.append(bm.T)
            y_offs.append(jnp.dot(cm16, state16, preferred_element_type=F32))
        y_pairs, state_pairs = [], []
        for k in range(SSD_HEADS // 2):
            pair = slice(k * LANES, (k + 1) * LANES)
            xs16 = xc_ref[rows, pair].astype(BF16)
            y_diag, contrib, exp_a, dec, y_off = [], [], [], [], []
            for hh in (2 * k, 2 * k + 1):
                g = hh // SSD_HPG
                col = direction * SSD_HEADS + hh
                a_col = jnp.broadcast_to(acs[:, col:col + 1], (CHUNK, CHUNK))
                seg = a_col - per_head_rows[col:col + 1, :]
                decay = jnp.exp(jnp.where(mask, seg, -1e30))
                m = (cbs[g] * decay * per_head_rows[DT_REP + col:DT_REP + col + 1, :]).astype(BF16)
                y_diag.append(jnp.dot(m, xs16, preferred_element_type=F32))
                exp_a.append(jnp.exp(a_col))
                y_off.append(y_offs[g][:, pair])
                lhs = (bm_ts[g] * per_head_rows[2 * DT_REP + col:2 * DT_REP + col + 1, :]).astype(BF16)
                contrib.append(jnp.dot(lhs, xs16, preferred_element_type=F32))
                dec.append(jnp.broadcast_to(chunk_decay[:, col:col + 1], (SSD_STATE, LANES)))
            pick = lambda ab: jnp.where(first_half, ab[0], ab[1])
            y_pairs.append(pick(y_diag) + pick(exp_a) * pick(y_off))
            state_pairs.append(pick(dec) * state[:, pair] + pick(contrib))
        st_ref[direction] = jnp.concatenate(state_pairs, axis=1)
        y = jnp.concatenate(y_pairs, axis=1)
        if direction == 0:
            yf_ref[rows, :] = skip_ref[...] * xc_ref[rows, 0:SSD_WIDTH] + y
        else:
            yb_ref[rows, :] = y

    def scan_body(step, carry):
        chunk_step(step, 0)
        chunk_step(jnp.where(step < n_ctx_blk, n_ctx_blk - 1 - step, n_blk - 1 - (step - n_ctx_blk)), 1)
        return carry

    st_ref[...] = jnp.zeros(st_ref.shape, F32)
    lax.fori_loop(0, n_blk, scan_body, 0)

    def out_body(c, carry):
        rows = pl.ds(pl.multiple_of(c * CHUNK, CHUNK), CHUNK)
        gated = (yf_ref[rows, :] + yb_ref[rows, :]) * _silu(z_ref[0, rows, :])
        o_ref[0, rows, :] = (_rms(gated) * g_ref[...]).astype(BF16)
        return carry

    lax.fori_loop(0, n_blk, out_body, 0)


def _ssd(p_z, p_xbc, p_dt, conv_w, conv_b, dt_bias, a_log, skip, g, ctx_len):
    bsz, t_all, _ = p_z.shape
    kern = functools.partial(_ssd_kernel, n_blk=t_all // CHUNK, n_ctx_blk=ctx_len // CHUNK)
    const = lambda b: (0, 0)
    per_b = lambda w: pl.BlockSpec((1, t_all, w), lambda b: (b, 0, 0))
    return pl.pallas_call(
        kern,
        out_shape=jax.ShapeDtypeStruct((bsz, t_all, SSD_WIDTH), BF16),
        grid=(bsz,),
        in_specs=[per_b(SSD_WIDTH), per_b(SSD_CONV_CH), per_b(LANES),
                  pl.BlockSpec(conv_w.shape, const), pl.BlockSpec(conv_b.shape, const),
                  pl.BlockSpec(dt_bias.shape, const), pl.BlockSpec(a_log.shape, const),
                  pl.BlockSpec(skip.shape, const), pl.BlockSpec(g.shape, const)],
        out_specs=per_b(SSD_WIDTH),
        scratch_shapes=[pltpu.VMEM((t_all, SSD_CONV_CH), F32), pltpu.VMEM((t_all, SSD_WIDTH), F32),
                        pltpu.VMEM((t_all, SSD_WIDTH), F32),
                        pltpu.VMEM((2, SSD_STATE, SSD_WIDTH), F32)],
        compiler_params=pltpu.CompilerParams(dimension_semantics=("arbitrary",), vmem_limit_bytes=VMEM_LIMIT),
        name="ssd",
    )(p_z, p_xbc, p_dt, conv_w, conv_b, dt_bias, a_log, skip, g)


def _mla_proj_kernel(pq_ref, pkv_ref, pkr_ref, gq_ref, wq_ref, gkv_ref, wk_ref, wv_ref, cos_ref, sa_ref, sb_ref,
                     q_ref, k_ref, v_ref):
    cos = cos_ref[...]
    sin_a = sa_ref[...]
    sin_b = sb_ref[...]

    def rope(t):
        return t * cos + pltpu.roll(t, 8, 1) * sin_a + pltpu.roll(t, LANES - 8, 1) * sin_b

    qn = (_rms(pq_ref[0]) * gq_ref[...]).astype(BF16)
    q = jnp.dot(qn, wq_ref[...], preferred_element_type=F32)
    kvn = (_rms(pkv_ref[0]) * gkv_ref[...]).astype(BF16)
    k = jnp.dot(kvn, wk_ref[...], preferred_element_type=F32)
    v_ref[0] = lax.dot_general(wv_ref[...], kvn, (((1,), (1,)), ((), ())),
                               preferred_element_type=F32).astype(BF16)
    kr = rope(pkr_ref[0])
    for h in range(MLA_HEADS):
        slot = slice(h * HEAD_SLOT, (h + 1) * HEAD_SLOT)
        q_ref[0, :, slot] = (rope(q[:, slot]) * (MLA_SCALE * LOG2_E)).astype(BF16)
        k_ref[0, :, slot] = (k[:, slot] + kr).astype(BF16)


def _mla_proj(p_q, p_kv, p_kr, g_q, w_q, g_kv, w_k, w_v, cos, sin_a, sin_b, rows):
    bsz, t_all, _ = p_q.shape
    width = MLA_HEADS * HEAD_SLOT
    const = lambda b, t: (0, 0)
    tok = lambda w: pl.BlockSpec((1, rows, w), lambda b, t: (b, t, 0))
    tab = pl.BlockSpec((rows, LANES), lambda b, t: (t, 0))
    out = jax.ShapeDtypeStruct((bsz, t_all, width), BF16)
    out_vt = jax.ShapeDtypeStruct((bsz, MLA_WIDTH, t_all), BF16)
    return pl.pallas_call(
        _mla_proj_kernel,
        out_shape=[out, out, out_vt],
        grid=(bsz, t_all // rows),
        in_specs=[tok(Q_LORA), tok(KV_LORA), tok(LANES),
                  pl.BlockSpec(g_q.shape, const), pl.BlockSpec(w_q.shape, const),
                  pl.BlockSpec(g_kv.shape, const), pl.BlockSpec(w_k.shape, const), pl.BlockSpec(w_v.shape, const),
                  tab, tab, tab],
        out_specs=[tok(width), tok(width), pl.BlockSpec((1, MLA_WIDTH, rows), lambda b, t: (b, 0, t))],
        compiler_params=pltpu.CompilerParams(dimension_semantics=("arbitrary", "arbitrary"),
                                             vmem_limit_bytes=VMEM_LIMIT),
        name="mla_proj",
    )(p_q, p_kv, p_kr, g_q, w_q, g_kv, w_k, w_v, cos, sin_a, sin_b)


ATTN_Q = 512


def _attn_kernel(*refs, n_q_blocks):
    q_refs = refs[:n_q_blocks]
    k_ref, vt_ref, o_ref = refs[n_q_blocks:]
    outs = []
    for h in range(MLA_HEADS):
        slot = slice(h * HEAD_SLOT, (h + 1) * HEAD_SLOT)
        qh = jnp.concatenate([r[0, :, slot] for r in q_refs], axis=0)
        s_t = lax.dot_general(k_ref[0, :, slot], qh, (((1,), (1,)), ((), ())),
                              preferred_element_type=F32)
        p_t = jnp.exp2(s_t - jnp.max(s_t, axis=0, keepdims=True))
        denom = jnp.sum(p_t, axis=0, keepdims=True)
        o_t = jnp.dot(vt_ref[0, h * MLA_V:(h + 1) * MLA_V, :], p_t.astype(BF16),
                      preferred_element_type=F32)
        outs.append(o_t / denom)
    o_ref[0] = jnp.concatenate(outs, axis=0).T.astype(BF16)


def _attention(q, k, vt, ctx_len, latent):
    bsz, t_all, width = q.shape
    if latent:
        n_q_blocks = ATTN_Q // ctx_len
        n_rows, n_keys, tq = t_all - ctx_len, t_all, ATTN_Q
        q_specs = [pl.BlockSpec((1, ctx_len, width),
                                functools.partial(lambda b, i, j: (b, n_q_blocks * i + 1 + j, 0), j=j))
                   for j in range(n_q_blocks)]
    else:
        n_q_blocks = 1
        n_rows, n_keys, tq = ctx_len, ctx_len, ctx_len
        q_specs = [pl.BlockSpec((1, ctx_len, width), lambda b, i: (b, 0, 0))]
    assert n_rows % tq == 0
    return pl.pallas_call(
        functools.partial(_attn_kernel, n_q_blocks=n_q_blocks),
        out_shape=jax.ShapeDtypeStruct((bsz, n_rows, MLA_WIDTH), BF16),
        grid=(bsz, n_rows // tq),
        in_specs=q_specs + [pl.BlockSpec((1, n_keys, width), lambda b, i: (b, 0, 0)),
                            pl.BlockSpec((1, MLA_WIDTH, n_keys), lambda b, i: (b, 0, 0))],
        out_specs=pl.BlockSpec((1, tq, MLA_WIDTH), lambda b, i: (b, i, 0)),
        compiler_params=pltpu.CompilerParams(dimension_semantics=("arbitrary", "arbitrary"),
                                             vmem_limit_bytes=VMEM_LIMIT),
        name="attention",
    )(*([q] * n_q_blocks), k, vt)


def _mix_residual(x_in, o1_ref, o2_ref, o3, mod_ref, w1_ref, w2_ref, w3_ref, g_ref, b, t, rows, ctx_len):
    mix = (jnp.dot(o1_ref[0], w1_ref[...], preferred_element_type=F32)
           + jnp.dot(o2_ref[0], w2_ref[...], preferred_element_type=F32)
           + jnp.dot(o3, w3_ref[...], preferred_element_type=F32))
    x = x_in + _row_select(mod_ref, b, t, rows, ctx_len, 2 * D_MODEL) * mix
    shift = _row_select(mod_ref, b, t, rows, ctx_len, 3 * D_MODEL)
    scale = _row_select(mod_ref, b, t, rows, ctx_len, 4 * D_MODEL)
    return x, _rms(x) * g_ref[...] * (1.0 + scale) + shift


def _out_kernel(x_ref, o1_ref, o2_ref, o3_ref, mod_ref, w1_ref, w2_ref, w3_ref, g_ref, wr_ref,
                xo_ref, h_ref, gate_ref, *, rows, ctx_len):
    b = pl.program_id(0)
    t = pl.program_id(1) + 1
    x, h = _mix_residual(x_ref[0], o1_ref, o2_ref, o3_ref[0], mod_ref, w1_ref, w2_ref, w3_ref, g_ref, b, t, rows,
                         ctx_len)
    xo_ref[0] = x
    h_ref[0] = h.astype(BF16)
    h_hi = h.astype(BF16)
    h_lo = (h - h_hi.astype(F32)).astype(BF16)
    by_hi = jnp.dot(h_hi, wr_ref[...], preferred_element_type=F32)
    logits = by_hi[:, :LANES] + by_hi[:, LANES:] + jnp.dot(h_lo, wr_ref[:, :LANES], preferred_element_type=F32)
    lane = lax.broadcasted_iota(jnp.int32, logits.shape, 1)
    lane_f = lane.astype(F32)
    lg = jnp.where(lane < N_EXPERTS, logits, -jnp.inf)
    m1 = jnp.max(lg, axis=-1, keepdims=True)
    i1 = jnp.min(jnp.where(lg == m1, lane_f, float(LANES)), axis=-1, keepdims=True)
    lg2 = jnp.where(lane_f == i1, -jnp.inf, lg)
    m2 = jnp.max(lg2, axis=-1, keepdims=True)
    i2 = jnp.min(jnp.where(lg2 == m2, lane_f, float(LANES)), axis=-1, keepdims=True)
    e2 = jnp.exp(m2 - m1)
    w_top = 1.0 / (1.0 + e2)
    gate_ref[0] = jnp.where(lane_f == i1, w_top, 0.0) + jnp.where(lane_f == i2, e2 * w_top, 0.0)


def _out_proj_route(xa, o_sgu, o_ssd, o_mla, mod, w1, w2, w3, g, w_router, ctx_len):
    bsz, t_all, d = xa.shape
    rows = ctx_len
    n_out = t_all - ctx_len
    kern = functools.partial(_out_kernel, rows=rows, ctx_len=ctx_len)
    const = lambda b, t: (0, 0)
    tok = lambda w: pl.BlockSpec((1, rows, w), lambda b, t: (b, t + 1, 0))
    out_tok = lambda w: pl.BlockSpec((1, rows, w), lambda b, t: (b, t, 0))
    return pl.pallas_call(
        kern,
        out_shape=[jax.ShapeDtypeStruct((bsz, n_out, d), F32), jax.ShapeDtypeStruct((bsz, n_out, d), BF16),
                   jax.ShapeDtypeStruct((bsz, n_out, LANES), F32)],
        grid=(bsz, n_out // rows),
        in_specs=[tok(d), tok(SGU_WIDTH), tok(SSD_WIDTH), out_tok(MLA_WIDTH), pl.BlockSpec(mod.shape, const),
                  pl.BlockSpec(w1.shape, const), pl.BlockSpec(w2.shape, const), pl.BlockSpec(w3.shape, const),
                  pl.BlockSpec(g.shape, const), pl.BlockSpec(w_router.shape, const)],
        out_specs=[out_tok(d), out_tok(d), out_tok(LANES)],
        compiler_params=pltpu.CompilerParams(dimension_semantics=("arbitrary", "arbitrary"),
                                             vmem_limit_bytes=VMEM_LIMIT),
        name="out_proj",
    )(xa, o_sgu, o_ssd, o_mla, mod, w1, w2, w3, g, w_router)


FF_SPLITS = ((0, 1536), (1536, D_FF))


def _mix_ffn_kernel(*refs, rows, ctx_len, n_tok_refs):
    tok_refs = refs[:n_tok_refs]
    (o1_ref, o2_ref, o3_ref, o3c_ref, mod_ref, w1_ref, w2_ref, w3_ref, g_ref, wg_ref, wu_ref, wd_ref,
     o_ref) = refs[n_tok_refs:]
    b = pl.program_id(0)
    t = pl.program_id(1)
    o3 = jnp.where(t == 0, o3c_ref[0], o3_ref[0])
    x, h = _mix_residual(_tokens_tile(tok_refs, t), o1_ref, o2_ref, o3, mod_ref, w1_ref, w2_ref, w3_ref, g_ref,
                         b, t, rows, ctx_len)
    h = h.astype(BF16)
    acc = None
    for c0, c1 in FF_SPLITS:
        gate = jnp.dot(h, wg_ref[:, c0:c1], preferred_element_type=F32)
        up = jnp.dot(h, wu_ref[:, c0:c1], preferred_element_type=F32)
        part = jnp.dot((_silu(gate) * up).astype(BF16), wd_ref[c0:c1, :], preferred_element_type=F32)
        acc = part if acc is None else acc + part
    o_ref[0] = x + _row_select(mod_ref, b, t, rows, ctx_len, 5 * D_MODEL) * acc


def _mix_ffn(tokens, o_sgu, o_ssd, o_mla, o_mla_ctx, mod, w1, w2, w3, g, wg, wu, wd, ctx_len):
    bsz, t_all, _ = o_sgu.shape
    d = D_MODEL
    rows = ctx_len
    tok_specs, tok_args = _token_specs(tokens, ctx_len, rows, d)
    kern = functools.partial(_mix_ffn_kernel, rows=rows, ctx_len=ctx_len, n_tok_refs=len(tok_args))
    const = lambda b, t: (0, 0)
    tok = lambda w: pl.BlockSpec((1, rows, w), lambda b, t: (b, t, 0))
    whole = lambda a: pl.BlockSpec(a.shape, const)
    return pl.pallas_call(
        kern,
        out_shape=jax.ShapeDtypeStruct((bsz, t_all, d), F32),
        grid=(bsz, t_all // rows),
        in_specs=tok_specs + [tok(SGU_WIDTH), tok(SSD_WIDTH),
                              pl.BlockSpec((1, rows, MLA_WIDTH), lambda b, t: (b, jnp.maximum(t - 1, 0), 0)),
                              pl.BlockSpec((1, rows, MLA_WIDTH), lambda b, t: (b, 0, 0)),
                              whole(mod), whole(w1), whole(w2), whole(w3), whole(g), whole(wg), whole(wu),
                              whole(wd)],
        out_specs=tok(d),
        compiler_params=pltpu.CompilerParams(dimension_semantics=("arbitrary", "arbitrary"),
                                             vmem_limit_bytes=VMEM_LIMIT),
        name="mix_ffn",
    )(*tok_args, o_sgu, o_ssd, o_mla, o_mla_ctx, mod, w1, w2, w3, g, wg, wu, wd)


MOE_TILE = 1024
MOE_BLOCK = 144
MOE_SLOTS = 256


def _plan_kernel(gate_ref, rank_ref, rank_rows_ref, count_ref):
    rows = gate_ref.shape[0]
    routed = gate_ref[...] > 0.0
    ti = lax.broadcasted_iota(jnp.int32, (rows, rows), 0)
    tj = lax.broadcasted_iota(jnp.int32, (rows, rows), 1)
    earlier = jnp.where(tj < ti, 1.0, 0.0).astype(BF16)
    ones = jnp.where(routed, 1.0, 0.0)
    before = jnp.dot(earlier, ones.astype(BF16), preferred_element_type=F32)
    rank = jnp.where(routed, before, -1.0)
    rank_ref[...] = rank
    rank_rows_ref[0] = rank.T[0:N_EXPERTS, :]
    count_ref[0] = jnp.broadcast_to(jnp.sum(ones, axis=0, keepdims=True), (8, LANES)).astype(jnp.int32)


def _route_plan(gates):
    n_tok = gates.shape[0]
    n_tiles = n_tok // MOE_TILE
    return pl.pallas_call(
        _plan_kernel,
        out_shape=[jax.ShapeDtypeStruct((n_tok, LANES), F32),
                   jax.ShapeDtypeStruct((n_tiles, N_EXPERTS, MOE_TILE), F32),
                   jax.ShapeDtypeStruct((n_tiles, 8, LANES), jnp.int32)],
        grid=(n_tiles,),
        in_specs=[pl.BlockSpec((MOE_TILE, LANES), lambda i: (i, 0))],
        out_specs=[pl.BlockSpec((MOE_TILE, LANES), lambda i: (i, 0)),
                   pl.BlockSpec((1, N_EXPERTS, MOE_TILE), lambda i: (i, 0, 0)),
                   pl.BlockSpec((1, 8, LANES), lambda i: (i, 0, 0))],
        compiler_params=pltpu.CompilerParams(dimension_semantics=("arbitrary",), vmem_limit_bytes=VMEM_LIMIT),
        name="route_plan",
    )(gates)


def _moe_kernel(count_ref, x_ref, h_ref, gate_ref, rank_ref, rank_rows_ref, mod_ref, wg_ref, wu_ref, wd_ref, gf_ref,
                o_ref, *, tiles_per_batch):
    i = pl.program_id(0)
    e = pl.program_id(1)
    rows = h_ref.shape[0]
    d = h_ref.shape[1]
    lane = lax.broadcasted_iota(jnp.int32, (rows, LANES), 1)
    ge = jnp.sum(jnp.where(lane == e, gate_ref[...], 0.0), axis=-1, keepdims=True)
    rank_lanes = jnp.broadcast_to(jnp.sum(jnp.where(lane == e, rank_ref[...], 0.0), axis=-1, keepdims=True),
                                  (rows, MOE_SLOTS))
    rank_row = rank_rows_ref[0, pl.ds(e, 1), :]

    @pl.when(e == 0)
    def _():
        o_ref[...] = jnp.zeros(o_ref.shape, F32)

    slot_sub = lax.broadcasted_iota(jnp.int32, (MOE_BLOCK, rows), 0)
    slot_lane = lax.broadcasted_iota(jnp.int32, (1, MOE_SLOTS), 1)

    def block(j, carry):
        lo = j * MOE_BLOCK
        pick = jnp.where(rank_row == (slot_sub + lo).astype(F32), 1.0, 0.0).astype(BF16)
        hg = jnp.dot(pick, h_ref[...], preferred_element_type=F32).astype(BF16)
        gate = jnp.dot(hg, wg_ref[0], preferred_element_type=F32)
        up = jnp.dot(hg, wu_ref[0], preferred_element_type=F32)
        y = jnp.dot((_silu(gate) * up).astype(BF16), wd_ref[0], preferred_element_type=F32).astype(BF16)
        y = jnp.concatenate([y, jnp.zeros((MOE_SLOTS - MOE_BLOCK, d), BF16)], axis=0)
        target = jnp.where(slot_lane < MOE_BLOCK, slot_lane + lo, -2).astype(F32)
        place = jnp.where(rank_lanes == target, 1.0, 0.0).astype(BF16)
        o_ref[...] += ge * jnp.dot(place, y, preferred_element_type=F32)
        return carry

    n_blocks = (count_ref[i * N_EXPERTS + e] + (MOE_BLOCK - 1)) // MOE_BLOCK
    lax.fori_loop(0, n_blocks, block, 0)

    @pl.when(e == N_EXPERTS - 1)
    def _():
        gate5 = mod_ref[pl.ds(i // tiles_per_batch, 1), 5 * D_MODEL:6 * D_MODEL]
        o_ref[...] = _rms(x_ref[...] + gate5 * o_ref[...]) * gf_ref[...]


def _moe_final(x, h, gates, mod, wg, wu, wd, g_final):
    bsz, n, d = x.shape
    assert n % MOE_TILE == 0
    n_tok = bsz * n
    gates = gates.reshape(n_tok, LANES)
    rank, rank_rows, counts = _route_plan(gates)
    counts = counts[:, 0, :N_EXPERTS].reshape(-1)
    kern = functools.partial(_moe_kernel, tiles_per_batch=n // MOE_TILE)
    const = lambda i, e, bnd: (0, 0)
    tok = lambda w: pl.BlockSpec((MOE_TILE, w), lambda i, e, bnd: (i, 0))
    per_expert = lambda a, b: pl.BlockSpec((1, a, b), lambda i, e, bnd: (e, 0, 0))
    out = pl.pallas_call(
        kern,
        out_shape=jax.ShapeDtypeStruct((n_tok, d), F32),
        grid_spec=pltpu.PrefetchScalarGridSpec(
            num_scalar_prefetch=1,
            grid=(n_tok // MOE_TILE, N_EXPERTS),
            in_specs=[tok(d), tok(d), tok(LANES), tok(LANES),
                      pl.BlockSpec((1, N_EXPERTS, MOE_TILE), lambda i, e, bnd: (i, 0, 0)),
                      pl.BlockSpec(mod.shape, const),
                      per_expert(d, D_FF_EXPERT), per_expert(d, D_FF_EXPERT), per_expert(D_FF_EXPERT, d),
                      pl.BlockSpec(g_final.shape, const)],
            out_specs=tok(d)),
        compiler_params=pltpu.CompilerParams(dimension_semantics=("arbitrary", "arbitrary"),
                                             vmem_limit_bytes=VMEM_LIMIT),
        name="moe_final",
    )(counts, x.reshape(n_tok, d), h.reshape(n_tok, d), gates, rank, rank_rows, mod, wg, wu, wd, g_final)
    return out.reshape(bsz, n, d)


def _pad_in_weight(w):
    d = w.shape[0]
    z = lambda n: jnp.zeros((d, n), w.dtype)
    dt0, dt1 = 1792, 1804
    kr0 = dt1 + Q_LORA + KV_LORA
    dt_slot = _dt_lanes(w[:, dt0:dt1])
    return jnp.concatenate([w[:, :dt0], dt_slot, w[:, dt1:kr0], z(ROPE_LANE0), w[:, kr0:],
                            z(LANES - ROPE_LANE0 - MLA_ROPE)], axis=1).astype(BF16)


def _dt_lanes(v):
    gap = jnp.zeros(v.shape[:-1] + (DT_REP - v.shape[-1],), v.dtype)
    tail = jnp.zeros(v.shape[:-1] + (LANES - 3 * DT_REP,), v.dtype)
    return jnp.concatenate([v, gap, v, gap, v, gap, tail], axis=-1)


def _head_slots(w, per_head, take0, take1, slot=HEAD_SLOT):
    k = w.shape[0]
    w = w.reshape(k, MLA_HEADS, per_head)[:, :, take0:take1]
    w = jnp.pad(w, ((0, 0), (0, 0), (0, slot - (take1 - take0))))
    return w.reshape(k, MLA_HEADS * slot).astype(BF16)


def _rope_tables(n_latent, ctx_len):
    t = jnp.arange(n_latent)
    pos = jnp.stack([(t // GRID_W).astype(F32), (t % GRID_W).astype(F32)], axis=1)
    n_freq = MLA_ROPE // 4
    inv_freq = ROPE_THETA ** (-jnp.arange(n_freq, dtype=F32) / n_freq)
    ang = pos[:, :, None] * inv_freq
    cos, sin = jnp.cos(ang), jnp.sin(ang)
    zero = jnp.zeros_like(sin)
    cos_r = jnp.stack([cos, cos], axis=2).reshape(n_latent, MLA_ROPE)
    sa_r = jnp.stack([zero, sin], axis=2).reshape(n_latent, MLA_ROPE)
    sb_r = jnp.stack([-sin, zero], axis=2).reshape(n_latent, MLA_ROPE)

    def slot(r, fill):
        lat = jnp.concatenate([jnp.full((n_latent, ROPE_LANE0), fill, F32), r,
                               jnp.full((n_latent, LANES - ROPE_LANE0 - MLA_ROPE), fill, F32)], axis=1)
        return jnp.concatenate([jnp.full((ctx_len, LANES), fill, F32), lat], axis=0)

    return slot(cos_r, 1.0), slot(sa_r, 0.0), slot(sb_r, 0.0)


def kernel(x, c, ctx, c_ctx, w_mod, b_mod, g_mix, w_in, w_sgu, b_sgu, g_sgu, beta_sgu, conv_w, conv_b, dt_bias,
           a_log, d_skip, g_ssd, g_q, w_uq, g_kv, w_ukv, w_out, g_ffn, w_gate, w_up, w_down, w_router, w_gate_e,
           w_up_e, w_down_e, g_final):
    bsz, n, d = x.shape
    ctx_len = ctx.shape[1]
    depth = w_in.shape[0]
    assert bsz <= CTX_MOD_ROW and ctx_len % CHUNK == 0 and n % CHUNK == 0

    cond = jnp.concatenate([c, jnp.zeros((CTX_MOD_ROW - bsz, d), F32), c_ctx[None],
                            jnp.zeros((MOD_ROWS - CTX_MOD_ROW - 1, d), F32)], axis=0)
    mod_all = _mod_table(cond, w_mod, b_mod)
    cos, sin_a, sin_b = _rope_tables(n, ctx_len)
    xa = (ctx, x)
    t_all = ctx_len + n
    rows_in = 768 if t_all % 768 == 0 else ctx_len

    out = None
    for layer in range(depth):
        last = layer == depth - 1
        mod = mod_all[layer]
        row = lambda v: v.reshape(1, -1)
        p_sgu, p_z, p_xbc, p_dt, p_q, p_kv, p_kr = _in_proj(
            xa, t_all, mod, row(g_mix[layer]), _pad_in_weight(w_in[layer]), ctx_len,
            ctx_len if isinstance(xa, tuple) else rows_in)
        w_cat = jnp.transpose(w_sgu[layer], (1, 0, 2)).reshape(CHUNK, SGU_HEADS * CHUNK).astype(BF16)
        bias = jnp.repeat(b_sgu[layer].T, SGU_HEAD_DIM, axis=1)
        o_sgu = _sgu(p_sgu, w_cat, bias, row(g_sgu[layer]), row(beta_sgu[layer]), rows_in)
        o_ssd = _ssd(p_z, p_xbc, p_dt, conv_w[layer], row(conv_b[layer]), _dt_lanes(dt_bias[layer].reshape(1, -1)),
                     _dt_lanes(a_log[layer].reshape(1, -1)), row(jnp.repeat(d_skip[layer], SSD_HEAD_DIM)), row(g_ssd[layer]),
                     ctx_len)
        per_q = MLA_NOPE + MLA_ROPE
        per_kv = MLA_NOPE + MLA_V
        q, k, v = _mla_proj(p_q, p_kv, p_kr, row(g_q[layer]), _head_slots(w_uq[layer], per_q, 0, per_q),
                            row(g_kv[layer]), _head_slots(w_ukv[layer], per_kv, 0, MLA_NOPE),
                            _head_slots(w_ukv[layer], per_kv, MLA_NOPE, per_kv, MLA_V).T, cos, sin_a, sin_b,
                            rows_in)
        o_mla = _attention(q, k, v, ctx_len, latent=True)
        o_mla_ctx = None if last else _attention(q, k, v, ctx_len, latent=False)
        wo = w_out[layer].astype(BF16)
        w1, w2, w3 = wo[:SGU_WIDTH], wo[SGU_WIDTH:SGU_WIDTH + SSD_WIDTH], wo[SGU_WIDTH + SSD_WIDTH:]
        i = layer // 2
        if layer % 2 == 0:
            if last:
                raise NotImplementedError("final dense channel mixer")
            xa = _mix_ffn(xa, o_sgu, o_ssd, o_mla, o_mla_ctx, mod, w1, w2, w3, row(g_ffn[layer]),
                          w_gate[i].astype(BF16), w_up[i].astype(BF16), w_down[i].astype(BF16), ctx_len)
        else:
            if not last:
                raise NotImplementedError("expert channel mixer on a non-final layer")
            wr_hi = w_router[i].astype(BF16)
            wr_lo = (w_router[i] - wr_hi.astype(F32)).astype(BF16)
            pad = lambda w: jnp.pad(w, ((0, 0), (0, LANES - N_EXPERTS)))
            wr = jnp.concatenate([pad(wr_hi), pad(wr_lo)], axis=1)
            x_mid, h_ffn, gates = _out_proj_route(xa, o_sgu, o_ssd, o_mla, mod, w1, w2, w3, row(g_ffn[layer]), wr,
                                                  ctx_len)
            out = _moe_final(x_mid, h_ffn, gates, mod, w_gate_e[i].astype(BF16), w_up_e[i].astype(BF16),
                             w_down_e[i].astype(BF16), row(g_final))
    return out
```

```python
import functools

import jax
import jax.numpy as jnp
from jax import lax
from jax.experimental import pallas as pl
from jax.experimental.pallas import tpu as pltpu

F32 = jnp.float32
BF16 = jnp.bfloat16
HIGHEST = lax.Precision.HIGHEST

D_MODEL = 1024
EPS = 1e-6
N_MOD = 6
GRID_W = 64
CHUNK = 128

SGU_HEADS = 4
SGU_HEAD_DIM = 64
SGU_WIDTH = 256

SSD_HEADS = 6
SSD_HEAD_DIM = 64
SSD_WIDTH = 384
SSD_GROUPS = 2
SSD_HPG = 3
SSD_STATE = 128
SSD_CONV = 5
SSD_CONV_CH = 896
SSD_GW = SSD_HPG * SSD_HEAD_DIM
DT_REP = 16

MLA_HEADS = 6
MLA_NOPE = 64
MLA_ROPE = 32
MLA_V = 64
MLA_WIDTH = 384
Q_LORA = 384
KV_LORA = 256
ROPE_THETA = 10000.0
MLA_SCALE = (MLA_NOPE + MLA_ROPE) ** -0.5
LOG2_E = 1.4426950408889634
HEAD_SLOT = 128
ROPE_LANE0 = MLA_NOPE

D_FF = 2816
N_EXPERTS = 8
D_FF_EXPERT = 1408

LANES = 128
MOD_ROWS = 16
CTX_MOD_ROW = 8

IN_COLS = (("sgu", 0, 512), ("z", 512, 896), ("xbc", 896, 1792), ("dt", 1792, 1920), ("cq", 1920, 2304),
           ("ckv", 2304, 2560), ("kr", 2560, 2688))
IN_PAD_WIDTH = 2688

VMEM_LIMIT = 56 * 1024 * 1024


def _sigmoid(x):
    return 1.0 / (1.0 + jnp.exp(-x))


def _silu(x):
    return x * _sigmoid(x)


def _rms(x):
    return x * lax.rsqrt(jnp.mean(x * x, axis=-1, keepdims=True) + EPS)


def _aligned(v, m):
    return v if isinstance(v, int) else pl.multiple_of(v, m)


def _row_select(mod_ref, b, tile, rows, ctx_len, col0):
    row = tile * rows + lax.broadcasted_iota(jnp.int32, (rows, 1), 0)
    is_ctx = row < ctx_len
    mb = mod_ref[pl.ds(b, 1), col0:col0 + D_MODEL]
    mc = mod_ref[CTX_MOD_ROW:CTX_MOD_ROW + 1, col0:col0 + D_MODEL]
    return jnp.where(is_ctx, mc, mb)


def _mod_kernel(cond_ref, w_ref, b_ref, o_ref):
    s = _silu(cond_ref[...])
    o_ref[0] = jnp.dot(s, w_ref[0], precision=HIGHEST, preferred_element_type=F32) + b_ref[0]


def _mod_table(cond, w_mod, b_mod):
    n_layers, d, width = w_mod.shape
    cb = 1536
    return pl.pallas_call(
        _mod_kernel,
        out_shape=jax.ShapeDtypeStruct((n_layers, MOD_ROWS, width), F32),
        grid=(n_layers, width // cb),
        in_specs=[pl.BlockSpec((MOD_ROWS, d), lambda l, j: (0, 0)),
                  pl.BlockSpec((1, d, cb), lambda l, j: (l, 0, j)),
                  pl.BlockSpec((1, 1, cb), lambda l, j: (l, 0, j))],
        out_specs=pl.BlockSpec((1, MOD_ROWS, cb), lambda l, j: (l, 0, j)),
        compiler_params=pltpu.CompilerParams(dimension_semantics=("arbitrary", "arbitrary"),
                                             vmem_limit_bytes=VMEM_LIMIT),
        name="mod_table",
    )(cond, w_mod, b_mod.reshape(n_layers, 1, width))


def _token_specs(tokens, ctx_len, rows, d):
    if not isinstance(tokens, tuple):
        return [pl.BlockSpec((1, rows, d), lambda b, t: (b, t, 0))], [tokens]
    ctx, lat = tokens
    n_sub = rows // ctx_len
    assert rows == n_sub * ctx_len
    piece = lambda j: pl.BlockSpec((1, ctx_len, d), lambda b, t: (b, jnp.maximum(n_sub * t + j - 1, 0), 0))
    return ([pl.BlockSpec((1, ctx_len, d), lambda b, t: (b, 0, 0))] + [piece(j) for j in range(n_sub)],
            [ctx] + [lat] * n_sub)


def _tokens_tile(refs, t):
    if len(refs) == 1:
        return refs[0][0]
    first = jnp.where(t == 0, refs[0][0], refs[1][0])
    return first if len(refs) == 2 else jnp.concatenate([first] + [r[0] for r in refs[2:]], axis=0)


IN_RAW_DT = (1792, 1804)
IN_RAW_KR = IN_RAW_DT[1] + Q_LORA + KV_LORA
IN_RAW_WIDTH = IN_RAW_KR + MLA_ROPE


def _pad_in_weight(wraw_ref, w_ref):
    dt0, dt1 = IN_RAW_DT
    for r0 in range(0, D_MODEL, LANES):
        rows = slice(r0, r0 + LANES)
        z = lambda n: jnp.zeros((LANES, n), F32)
        dt = wraw_ref[0, rows, dt0:dt1]
        gap = z(DT_REP - (dt1 - dt0))
        w_ref[rows, :] = jnp.concatenate(
            [wraw_ref[0, rows, 0:dt0], dt, gap, dt, gap, dt, gap, z(LANES - 3 * DT_REP),
             wraw_ref[0, rows, dt1:IN_RAW_KR], z(ROPE_LANE0), wraw_ref[0, rows, IN_RAW_KR:IN_RAW_WIDTH],
             z(LANES - ROPE_LANE0 - MLA_ROPE)], axis=1).astype(BF16)


def _in_kernel(*refs, rows, ctx_len, n_tok_refs):
    tok_refs = refs[:n_tok_refs]
    mod_ref, g_ref, wraw_ref, *out_refs, w_ref = refs[n_tok_refs:]
    b = pl.program_id(0)
    t = pl.program_id(1)

    @pl.when(jnp.logical_and(b == 0, t == 0))
    def _():
        _pad_in_weight(wraw_ref, w_ref)

    xn = _rms(_tokens_tile(tok_refs, t)) * g_ref[...]
    shift = _row_select(mod_ref, b, t, rows, ctx_len, 0)
    scale = _row_select(mod_ref, b, t, rows, ctx_len, D_MODEL)
    h = (xn * (1.0 + scale) + shift).astype(BF16)
    p = jnp.dot(h, w_ref[...], preferred_element_type=F32)
    for o_ref, (_, c0, c1) in zip(out_refs, IN_COLS):
        o_ref[0] = p[:, c0:c1]


def _in_proj(tokens, t_all, mod, g, w_in, layer, ctx_len, rows):
    tok_specs, tok_args = _token_specs(tokens, ctx_len, rows, D_MODEL)
    bsz = tok_args[0].shape[0]
    assert w_in.shape[1:] == (D_MODEL, IN_RAW_WIDTH)
    kern = functools.partial(_in_kernel, rows=rows, ctx_len=ctx_len, n_tok_refs=len(tok_args))
    return pl.pallas_call(
        kern,
        out_shape=[jax.ShapeDtypeStruct((bsz, t_all, c1 - c0), F32) for _, c0, c1 in IN_COLS],
        grid=(bsz, t_all // rows),
        in_specs=tok_specs + [pl.BlockSpec(mod.shape, lambda b, t: (0, 0)),
                              pl.BlockSpec((1, D_MODEL), lambda b, t: (0, 0)),
                              pl.BlockSpec((1,) + w_in.shape[1:], lambda b, t: (layer, 0, 0))],
        out_specs=[pl.BlockSpec((1, rows, c1 - c0), lambda b, t: (b, t, 0)) for _, c0, c1 in IN_COLS],
        scratch_shapes=[pltpu.VMEM((D_MODEL, IN_PAD_WIDTH), BF16)],
        compiler_params=pltpu.CompilerParams(dimension_semantics=("arbitrary", "arbitrary"),
                                             vmem_limit_bytes=VMEM_LIMIT),
        name="in_proj",
    )(*tok_args, mod, g, w_in)


def _sgu_kernel(p_ref, w_ref, bias_ref, g_ref, beta_ref, o_ref, *, n_chunks):
    lane = lax.broadcasted_iota(jnp.int32, (1, SGU_WIDTH), 1)
    head_of_lane = lane // SGU_HEAD_DIM
    w = w_ref[...]
    bias = bias_ref[...]
    c0 = 0.7978845608028654
    for c in range(n_chunks):
        p = p_ref[0, c * CHUNK:(c + 1) * CHUNK, :]
        ge = 0.5 * p * (1.0 + jnp.tanh(c0 * (p + 0.044715 * (p * p * p))))
        u = ge[:, :SGU_WIDTH]
        v = ge[:, SGU_WIDTH:]
        mu = jnp.mean(v, axis=-1, keepdims=True)
        vc = v - mu
        vn = vc * lax.rsqrt(jnp.mean(vc * vc, axis=-1, keepdims=True) + EPS) * g_ref[...] + beta_ref[...]
        stacked = jnp.concatenate(
            [jnp.where(head_of_lane == h, vn, 0.0).astype(BF16) for h in range(SGU_HEADS)], axis=0)
        mixed = jnp.dot(w, stacked, preferred_element_type=F32) + bias
        o_ref[0, c * CHUNK:(c + 1) * CHUNK, :] = (u * mixed).astype(BF16)


def _sgu(p_sgu, w_cat, bias, g, beta, rows):
    bsz, t_all, width = p_sgu.shape
    kern = functools.partial(_sgu_kernel, n_chunks=rows // CHUNK)
    const = lambda b, t: (0, 0)
    return pl.pallas_call(
        kern,
        out_shape=jax.ShapeDtypeStruct((bsz, t_all, SGU_WIDTH), BF16),
        grid=(bsz, t_all // rows),
        in_specs=[pl.BlockSpec((1, rows, width), lambda b, t: (b, t, 0)),
                  pl.BlockSpec(w_cat.shape, const), pl.BlockSpec(bias.shape, const),
                  pl.BlockSpec(g.shape, const), pl.BlockSpec(beta.shape, const)],
        out_specs=pl.BlockSpec((1, rows, SGU_WIDTH), lambda b, t: (b, t, 0)),
        compiler_params=pltpu.CompilerParams(dimension_semantics=("arbitrary", "arbitrary"),
                                             vmem_limit_bytes=VMEM_LIMIT),
        name="sgu",
    )(p_sgu, w_cat, bias, g, beta)


def _ssd_kernel(z_ref, xbc_ref, dt_ref, cw_ref, cb_ref, dtb_ref, alog_ref, skip_ref, g_ref, o_ref,
                xc_ref, yf_ref, yb_ref, st_ref, *, n_blk, n_ctx_blk):
    def conv_block(blk, seg_start, seg_end):
        r0 = _aligned(blk * CHUNK, CHUNK)
        for cg in range(SSD_CONV_CH // LANES):
            cols = slice(cg * LANES, (cg + 1) * LANES)
            zeros = jnp.zeros((8, LANES), F32)
            top = zeros if seg_start else xbc_ref[0, pl.ds(_aligned(r0 - 8, 8), 8), cols]
            bot = zeros if seg_end else xbc_ref[0, pl.ds(_aligned(r0 + CHUNK, 8), 8), cols]
            xw = jnp.concatenate([top, xbc_ref[0, pl.ds(r0, CHUNK), cols], bot], axis=0)
            acc = cb_ref[:, cols] + cw_ref[0:1, cols] * xw[6:6 + CHUNK]
            for k in range(1, SSD_CONV):
                acc = acc + cw_ref[k:k + 1, cols] * xw[6 + k:6 + k + CHUNK]
            xc_ref[pl.ds(r0, CHUNK), cols] = _silu(acc)

    static_blocks = sorted(set(list(range(n_ctx_blk)) + [n_ctx_blk, n_blk - 1]))
    for blk in static_blocks:
        conv_block(blk, blk == 0 or blk == n_ctx_blk, blk == n_ctx_blk - 1 or blk == n_blk - 1)
    if n_blk - 1 > n_ctx_blk + 1:
        def conv_body(blk, carry):
            conv_block(blk, False, False)
            return carry
        lax.fori_loop(n_ctx_blk + 1, n_blk - 1, conv_body, 0)

    li = lax.broadcasted_iota(jnp.int32, (CHUNK, CHUNK), 0)
    si = lax.broadcasted_iota(jnp.int32, (CHUNK, CHUNK), 1)
    a_neg = -jnp.exp(alog_ref[...])

    def chunk_step(c, direction):
        r0 = pl.multiple_of(c * CHUNK, CHUNK)
        rows = pl.ds(r0, CHUNK)
        mask = (si <= li) if direction == 0 else (si >= li)
        tri = jnp.where(mask, 1.0, 0.0).astype(BF16)
        dtr = dt_ref[0, rows, :] + dtb_ref[...]
        dt = jnp.maximum(dtr, 0.0) + jnp.log1p(jnp.exp(-jnp.abs(dtr)))
        adt = dt * a_neg
        p1 = adt.astype(BF16)
        r1 = adt - p1.astype(F32)
        p2 = r1.astype(BF16)
        p3 = (r1 - p2.astype(F32)).astype(BF16)
        parts = jnp.dot(tri, jnp.concatenate([p1, p2, p3], axis=1), preferred_element_type=F32)
        acs = parts[:, :LANES] + parts[:, LANES:2 * LANES] + parts[:, 2 * LANES:]
        end = CHUNK - 1 if direction == 0 else 0
        tot = acs[end:end + 1, :]
        to_end_dt = jnp.exp(tot - acs) * dt
        chunk_decay = jnp.exp(tot)
        per_head_rows = jnp.where(si < DT_REP, acs, jnp.where(si < 2 * DT_REP, dt, to_end_dt)).T
        first_half = si < SSD_HEAD_DIM
        cbs, bm_ts, y_offs = [], [], []
        state = st_ref[direction]
        state16 = state.astype(BF16)
        for g in range(SSD_GROUPS):
            bm = xc_ref[rows, SSD_WIDTH + g * SSD_STATE:SSD_WIDTH + (g + 1) * SSD_STATE]
            cm = xc_ref[rows, SSD_WIDTH + SSD_GROUPS * SSD_STATE + g * SSD_STATE:
                        SSD_WIDTH + SSD_GROUPS * SSD_STATE + (g + 1) * SSD_STATE]
            cm16 = cm.astype(BF16)
            cbs.append(lax.dot_general(cm16, bm.astype(BF16), (((1,), (1,)), ((), ())),
                                       preferred_element_type=F32))
            bm_ts.append(bm.T)
            y_offs.append(jnp.dot(cm16, state16, preferred_element_type=F32))
        y_pairs, state_pairs = [], []
        for k in range(SSD_HEADS // 2):
            pair = slice(k * LANES, (k + 1) * LANES)
            xs16 = xc_ref[rows, pair].astype(BF16)
            y_diag, contrib, exp_a, dec, y_off = [], [], [], [], []
            for hh in (2 * k, 2 * k + 1):
                g = hh // SSD_HPG
                col = direction * SSD_HEADS + hh
                a_col = jnp.broadcast_to(acs[:, col:col + 1], (CHUNK, CHUNK))
                seg = a_col - per_head_rows[col:col + 1, :]
                decay = jnp.exp(jnp.where(mask, seg, -1e30))
                m = (cbs[g] * decay * per_head_rows[DT_REP + col:DT_REP + col + 1, :]).astype(BF16)
                y_diag.append(jnp.dot(m, xs16, preferred_element_type=F32))
                exp_a.append(jnp.exp(a_col))
                y_off.append(y_offs[g][:, pair])
                lhs = (bm_ts[g] * per_head_rows[2 * DT_REP + col:2 * DT_REP + col + 1, :]).astype(BF16)
                contrib.append(jnp.dot(lhs, xs16, preferred_element_type=F32))
                dec.append(jnp.broadcast_to(chunk_decay[:, col:col + 1], (SSD_STATE, LANES)))
            pick = lambda ab: jnp.where(first_half, ab[0], ab[1])
            y_pairs.append(pick(y_diag) + pick(exp_a) * pick(y_off))
            state_pairs.append(pick(dec) * state[:, pair] + pick(contrib))
        st_ref[direction] = jnp.concatenate(state_pairs, axis=1)
        y = jnp.concatenate(y_pairs, axis=1)
        if direction == 0:
            yf_ref[rows, :] = skip_ref[...] * xc_ref[rows, 0:SSD_WIDTH] + y
        else:
            yb_ref[rows, :] = y

    def scan_body(step, carry):
        chunk_step(step, 0)
        chunk_step(jnp.where(step < n_ctx_blk, n_ctx_blk - 1 - step, n_blk - 1 - (step - n_ctx_blk)), 1)
        return carry

    st_ref[...] = jnp.zeros(st_ref.shape, F32)
    lax.fori_loop(0, n_blk, scan_body, 0)

    def out_body(c, carry):
        rows = pl.ds(pl.multiple_of(c * CHUNK, CHUNK), CHUNK)
        gated = (yf_ref[rows, :] + yb_ref[rows, :]) * _silu(z_ref[0, rows, :])
        o_ref[0, rows, :] = (_rms(gated) * g_ref[...]).astype(BF16)
        return carry

    lax.fori_loop(0, n_blk, out_body, 0)


def _ssd(p_z, p_xbc, p_dt, conv_w, conv_b, dt_bias, a_log, skip, g, ctx_len):
    bsz, t_all, _ = p_z.shape
    kern = functools.partial(_ssd_kernel, n_blk=t_all // CHUNK, n_ctx_blk=ctx_len // CHUNK)
    const = lambda b: (0, 0)
    per_b = lambda w: pl.BlockSpec((1, t_all, w), lambda b: (b, 0, 0))
    return pl.pallas_call(
        kern,
        out_shape=jax.ShapeDtypeStruct((bsz, t_all, SSD_WIDTH), BF16),
        grid=(bsz,),
        in_specs=[per_b(SSD_WIDTH), per_b(SSD_CONV_CH), per_b(LANES),
                  pl.BlockSpec(conv_w.shape, const), pl.BlockSpec(conv_b.shape, const),
                  pl.BlockSpec(dt_bias.shape, const), pl.BlockSpec(a_log.shape, const),
                  pl.BlockSpec(skip.shape, const), pl.BlockSpec(g.shape, const)],
        out_specs=per_b(SSD_WIDTH),
        scratch_shapes=[pltpu.VMEM((t_all, SSD_CONV_CH), F32), pltpu.VMEM((t_all, SSD_WIDTH), F32),
                        pltpu.VMEM((t_all, SSD_WIDTH), F32),
                        pltpu.VMEM((2, SSD_STATE, SSD_WIDTH), F32)],
        compiler_params=pltpu.CompilerParams(dimension_semantics=("arbitrary",), vmem_limit_bytes=VMEM_LIMIT),
        name="ssd",
    )(p_z, p_xbc, p_dt, conv_w, conv_b, dt_bias, a_log, skip, g)


def _mla_proj_kernel(pq_ref, pkv_ref, pkr_ref, gq_ref, wq_ref, gkv_ref, wk_ref, wv_ref, cos_ref, sa_ref, sb_ref,
                     q_ref, k_ref, v_ref):
    cos = cos_ref[...]
    sin_a = sa_ref[...]
    sin_b = sb_ref[...]

    def rope(t):
        return t * cos + pltpu.roll(t, 8, 1) * sin_a + pltpu.roll(t, LANES - 8, 1) * sin_b

    qn = (_rms(pq_ref[0]) * gq_ref[...]).astype(BF16)
    q = jnp.dot(qn, wq_ref[...], preferred_element_type=F32)
    kvn = (_rms(pkv_ref[0]) * gkv_ref[...]).astype(BF16)
    k = jnp.dot(kvn, wk_ref[...], preferred_element_type=F32)
    v_t = lax.dot_general(wv_ref[...], kvn, (((1,), (1,)), ((), ())),
                          preferred_element_type=F32).astype(BF16)
    kr = rope(pkr_ref[0])
    for h in range(MLA_HEADS):
        slot = slice(h * HEAD_SLOT, (h + 1) * HEAD_SLOT)
        q_ref[0, h] = (rope(q[:, slot]) * (MLA_SCALE * LOG2_E)).astype(BF16)
        k_ref[0, h] = (k[:, slot] + kr).astype(BF16)
        v_ref[0, h] = v_t[h * MLA_V:(h + 1) * MLA_V, :]


def _mla_proj(p_q, p_kv, p_kr, g_q, w_q, g_kv, w_k, w_v, cos, sin_a, sin_b, rows):
    bsz, t_all, _ = p_q.shape
    const = lambda b, t: (0, 0)
    tok = lambda w: pl.BlockSpec((1, rows, w), lambda b, t: (b, t, 0))
    tab = pl.BlockSpec((rows, LANES), lambda b, t: (t, 0))
    out = jax.ShapeDtypeStruct((bsz, MLA_HEADS, t_all, HEAD_SLOT), BF16)
    out_vt = jax.ShapeDtypeStruct((bsz, MLA_HEADS, MLA_V, t_all), BF16)
    head_tok = pl.BlockSpec((1, MLA_HEADS, rows, HEAD_SLOT), lambda b, t: (b, 0, t, 0))
    return pl.pallas_call(
        _mla_proj_kernel,
        out_shape=[out, out, out_vt],
        grid=(bsz, t_all // rows),
        in_specs=[tok(Q_LORA), tok(KV_LORA), tok(LANES),
                  pl.BlockSpec(g_q.shape, const), pl.BlockSpec(w_q.shape, const),
                  pl.BlockSpec(g_kv.shape, const), pl.BlockSpec(w_k.shape, const), pl.BlockSpec(w_v.shape, const),
                  tab, tab, tab],
        out_specs=[head_tok, head_tok, pl.BlockSpec((1, MLA_HEADS, MLA_V, rows), lambda b, t: (b, 0, 0, t))],
        compiler_params=pltpu.CompilerParams(dimension_semantics=("arbitrary", "arbitrary"),
                                             vmem_limit_bytes=VMEM_LIMIT),
        name="mla_proj",
    )(p_q, p_kv, p_kr, g_q, w_q, g_kv, w_k, w_v, cos, sin_a, sin_b)


ATTN_Q = 512
ATTN_BUFS = 2


def _attn_kernel(*refs, n_q_blocks):
    q_refs = refs[:n_q_blocks]
    k_ref, vt_ref, o_ref, *bufs, ot_ref = refs[n_q_blocks:]
    s_bufs, m_bufs = bufs[:ATTN_BUFS], bufs[ATTN_BUFS:]
    def scores(h, s_ref, m_ref):
        qh = jnp.concatenate([r[0, h] for r in q_refs], axis=0)
        s_t = lax.dot_general(k_ref[0, h], qh, (((1,), (1,)), ((), ())), preferred_element_type=F32)
        s_ref[...] = s_t
        m_ref[...] = jnp.broadcast_to(jnp.max(s_t, axis=0, keepdims=True), m_ref.shape)

    def weigh(h, s_ref, m_ref):
        p_t = jnp.exp2(s_ref[...] - m_ref[0:1, :])
        denom = jnp.sum(p_t, axis=0, keepdims=True)
        o_t = jnp.dot(vt_ref[0, h], p_t.astype(BF16), preferred_element_type=F32)
        ot_ref[pl.ds(pl.multiple_of(h * MLA_V, MLA_V), MLA_V), :] = o_t / denom

    scores(0, s_bufs[0], m_bufs[0])

    def head_group(j, carry):
        h0 = ATTN_BUFS * j
        for i in range(ATTN_BUFS):
            nxt = (i + 1) % ATTN_BUFS
            scores(jnp.minimum(h0 + i + 1, MLA_HEADS - 1), s_bufs[nxt], m_bufs[nxt])
            weigh(h0 + i, s_bufs[i], m_bufs[i])
        return carry

    lax.fori_loop(0, MLA_HEADS // ATTN_BUFS, head_group, 0)
    o_ref[0] = ot_ref[...].T.astype(BF16)


def _attention(q, k, vt, ctx_len, latent):
    bsz, n_heads, t_all, width = q.shape
    if latent:
        n_q_blocks = ATTN_Q // ctx_len
        n_rows, n_keys, tq = t_all - ctx_len, t_all, ATTN_Q
        q_specs = [pl.BlockSpec((1, n_heads, ctx_len, width),
                                functools.partial(lambda b, i, j: (b, 0, n_q_blocks * i + 1 + j, 0), j=j))
                   for j in range(n_q_blocks)]
    else:
        n_q_blocks = 1
        n_rows, n_keys, tq = ctx_len, ctx_len, ctx_len
        q_specs = [pl.BlockSpec((1, n_heads, ctx_len, width), lambda b, i: (b, 0, 0, 0))]
    assert n_rows % tq == 0
    return pl.pallas_call(
        functools.partial(_attn_kernel, n_q_blocks=n_q_blocks),
        out_shape=jax.ShapeDtypeStruct((bsz, n_rows, MLA_WIDTH), BF16),
        grid=(bsz, n_rows // tq),
        in_specs=q_specs + [pl.BlockSpec((1, n_heads, n_keys, width), lambda b, i: (b, 0, 0, 0)),
                            pl.BlockSpec((1, n_heads, MLA_V, n_keys), lambda b, i: (b, 0, 0, 0))],
        out_specs=pl.BlockSpec((1, tq, MLA_WIDTH), lambda b, i: (b, i, 0)),
        scratch_shapes=([pltpu.VMEM((n_keys, tq), F32)] * ATTN_BUFS + [pltpu.VMEM((8, tq), F32)] * ATTN_BUFS
                        + [pltpu.VMEM((MLA_WIDTH, tq), F32)]),
        compiler_params=pltpu.CompilerParams(dimension_semantics=("arbitrary", "arbitrary"),
                                             vmem_limit_bytes=VMEM_LIMIT),
        name="attention",
    )(*([q] * n_q_blocks), k, vt)


def _mix_residual(x_in, o1_ref, o2_ref, o3, mod_ref, wo_ref, g_ref, b, t, rows, ctx_len):
    r1, r2 = SGU_WIDTH, SGU_WIDTH + SSD_WIDTH
    mix = (jnp.dot(o1_ref[0], wo_ref[0:r1, :], preferred_element_type=F32)
           + jnp.dot(o2_ref[0], wo_ref[r1:r2, :], preferred_element_type=F32)
           + jnp.dot(o3, wo_ref[r2:, :], preferred_element_type=F32))
    x = x_in + _row_select(mod_ref, b, t, rows, ctx_len, 2 * D_MODEL) * mix
    shift = _row_select(mod_ref, b, t, rows, ctx_len, 3 * D_MODEL)
    scale = _row_select(mod_ref, b, t, rows, ctx_len, 4 * D_MODEL)
    return x, _rms(x) * g_ref[...] * (1.0 + scale) + shift


def _out_kernel(x_ref, o1_ref, o2_ref, o3_ref, mod_ref, wo_ref, g_ref, wr_ref,
                xo_ref, h_ref, gate_ref, *, rows, ctx_len):
    b = pl.program_id(0)
    t = pl.program_id(1) + 1
    x, h = _mix_residual(x_ref[0], o1_ref, o2_ref, o3_ref[0], mod_ref, wo_ref, g_ref, b, t, rows, ctx_len)
    xo_ref[0] = x
    h_ref[0] = h.astype(BF16)
    h_hi = h.astype(BF16)
    h_lo = (h - h_hi.astype(F32)).astype(BF16)
    by_hi = jnp.dot(h_hi, wr_ref[...], preferred_element_type=F32)
    logits = by_hi[:, :LANES] + by_hi[:, LANES:] + jnp.dot(h_lo, wr_ref[:, :LANES], preferred_element_type=F32)
    lane = lax.broadcasted_iota(jnp.int32, logits.shape, 1)
    lane_f = lane.astype(F32)
    lg = jnp.where(lane < N_EXPERTS, logits, -jnp.inf)
    m1 = jnp.max(lg, axis=-1, keepdims=True)
    i1 = jnp.min(jnp.where(lg == m1, lane_f, float(LANES)), axis=-1, keepdims=True)
    lg2 = jnp.where(lane_f == i1, -jnp.inf, lg)
    m2 = jnp.max(lg2, axis=-1, keepdims=True)
    i2 = jnp.min(jnp.where(lg2 == m2, lane_f, float(LANES)), axis=-1, keepdims=True)
    e2 = jnp.exp(m2 - m1)
    w_top = 1.0 / (1.0 + e2)
    gate_ref[0] = jnp.where(lane_f == i1, w_top, 0.0) + jnp.where(lane_f == i2, e2 * w_top, 0.0)


def _out_proj_route(xa, o_sgu, o_ssd, o_mla, mod, wo, g, w_router, ctx_len):
    bsz, t_all, d = xa.shape
    rows = ctx_len
    n_out = t_all - ctx_len
    kern = functools.partial(_out_kernel, rows=rows, ctx_len=ctx_len)
    const = lambda b, t: (0, 0)
    tok = lambda w: pl.BlockSpec((1, rows, w), lambda b, t: (b, t + 1, 0))
    out_tok = lambda w: pl.BlockSpec((1, rows, w), lambda b, t: (b, t, 0))
    return pl.pallas_call(
        kern,
        out_shape=[jax.ShapeDtypeStruct((bsz, n_out, d), F32), jax.ShapeDtypeStruct((bsz, n_out, d), BF16),
                   jax.ShapeDtypeStruct((bsz, n_out, LANES), F32)],
        grid=(bsz, n_out // rows),
        in_specs=[tok(d), tok(SGU_WIDTH), tok(SSD_WIDTH), out_tok(MLA_WIDTH), pl.BlockSpec(mod.shape, const),
                  pl.BlockSpec(wo.shape, const), pl.BlockSpec(g.shape, const), pl.BlockSpec(w_router.shape, const)],
        out_specs=[out_tok(d), out_tok(d), out_tok(LANES)],
        compiler_params=pltpu.CompilerParams(dimension_semantics=("arbitrary", "arbitrary"),
                                             vmem_limit_bytes=VMEM_LIMIT),
        name="out_proj",
    )(xa, o_sgu, o_ssd, o_mla, mod, wo, g, w_router)


FF_SPLITS = ((0, 1536), (1536, D_FF))


def _mix_ffn_kernel(*refs, rows, ctx_len, n_tok_refs):
    tok_refs = refs[:n_tok_refs]
    (o1_ref, o2_ref, o3_ref, o3c_ref, mod_ref, wo_ref, g_ref, wg_ref, wu_ref, wd_ref,
     o_ref) = refs[n_tok_refs:]
    b = pl.program_id(0)
    t = pl.program_id(1)
    o3 = jnp.where(t == 0, o3c_ref[0], o3_ref[0])
    x, h = _mix_residual(_tokens_tile(tok_refs, t), o1_ref, o2_ref, o3, mod_ref, wo_ref, g_ref, b, t, rows, ctx_len)
    h = h.astype(BF16)
    acc = None
    for c0, c1 in FF_SPLITS:
        gate = jnp.dot(h, wg_ref[:, c0:c1], preferred_element_type=F32)
        up = jnp.dot(h, wu_ref[:, c0:c1], preferred_element_type=F32)
        part = jnp.dot((_silu(gate) * up).astype(BF16), wd_ref[c0:c1, :], preferred_element_type=F32)
        acc = part if acc is None else acc + part
    o_ref[0] = x + _row_select(mod_ref, b, t, rows, ctx_len, 5 * D_MODEL) * acc


def _mix_ffn(tokens, o_sgu, o_ssd, o_mla, o_mla_ctx, mod, wo, g, wg, wu, wd, ctx_len):
    bsz, t_all, _ = o_sgu.shape
    d = D_MODEL
    rows = ctx_len
    tok_specs, tok_args = _token_specs(tokens, ctx_len, rows, d)
    kern = functools.partial(_mix_ffn_kernel, rows=rows, ctx_len=ctx_len, n_tok_refs=len(tok_args))
    const = lambda b, t: (0, 0)
    tok = lambda w: pl.BlockSpec((1, rows, w), lambda b, t: (b, t, 0))
    whole = lambda a: pl.BlockSpec(a.shape, const)
    return pl.pallas_call(
        kern,
        out_shape=jax.ShapeDtypeStruct((bsz, t_all, d), F32),
        grid=(bsz, t_all // rows),
        in_specs=tok_specs + [tok(SGU_WIDTH), tok(SSD_WIDTH),
                              pl.BlockSpec((1, rows, MLA_WIDTH), lambda b, t: (b, jnp.maximum(t - 1, 0), 0)),
                              pl.BlockSpec((1, rows, MLA_WIDTH), lambda b, t: (b, 0, 0)),
                              whole(mod), whole(wo), whole(g), whole(wg), whole(wu), whole(wd)],
        out_specs=tok(d),
        compiler_params=pltpu.CompilerParams(dimension_semantics=("arbitrary", "arbitrary"),
                                             vmem_limit_bytes=VMEM_LIMIT),
        name="mix_ffn",
    )(*tok_args, o_sgu, o_ssd, o_mla, o_mla_ctx, mod, wo, g, wg, wu, wd)


MOE_TILE = 1024
MOE_BLOCK = 144
MOE_SLOTS = 256


def _plan_kernel(gate_ref, rank_ref, rank_rows_ref, count_ref):
    rows = gate_ref.shape[0]
    routed = gate_ref[...] > 0.0
    ti = lax.broadcasted_iota(jnp.int32, (rows, rows), 0)
    tj = lax.broadcasted_iota(jnp.int32, (rows, rows), 1)
    earlier = jnp.where(tj < ti, 1.0, 0.0).astype(BF16)
    ones = jnp.where(routed, 1.0, 0.0)
    before = jnp.dot(earlier, ones.astype(BF16), preferred_element_type=F32)
    rank = jnp.where(routed, before, -1.0)
    rank_ref[...] = rank
    rank_rows_ref[0] = rank.T[0:N_EXPERTS, :]
    count_ref[0] = jnp.broadcast_to(jnp.sum(ones, axis=0, keepdims=True), (8, LANES)).astype(jnp.int32)


def _route_plan(gates):
    n_tok = gates.shape[0]
    n_tiles = n_tok // MOE_TILE
    return pl.pallas_call(
        _plan_kernel,
        out_shape=[jax.ShapeDtypeStruct((n_tok, LANES), F32),
                   jax.ShapeDtypeStruct((n_tiles, N_EXPERTS, MOE_TILE), F32),
                   jax.ShapeDtypeStruct((n_tiles, 8, LANES), jnp.int32)],
        grid=(n_tiles,),
        in_specs=[pl.BlockSpec((MOE_TILE, LANES), lambda i: (i, 0))],
        out_specs=[pl.BlockSpec((MOE_TILE, LANES), lambda i: (i, 0)),
                   pl.BlockSpec((1, N_EXPERTS, MOE_TILE), lambda i: (i, 0, 0)),
                   pl.BlockSpec((1, 8, LANES), lambda i: (i, 0, 0))],
        compiler_params=pltpu.CompilerParams(dimension_semantics=("arbitrary",), vmem_limit_bytes=VMEM_LIMIT),
        name="route_plan",
    )(gates)


def _moe_kernel(count_ref, x_ref, h_ref, gate_ref, rank_ref, rank_rows_ref, mod_ref, wg_ref, wu_ref, wd_ref, gf_ref,
                o_ref, *, tiles_per_batch):
    i = pl.program_id(0)
    e = pl.program_id(1)
    rows = h_ref.shape[0]
    d = h_ref.shape[1]
    lane = lax.broadcasted_iota(jnp.int32, (rows, LANES), 1)
    ge = jnp.sum(jnp.where(lane == e, gate_ref[...], 0.0), axis=-1, keepdims=True)
    rank_lanes = jnp.broadcast_to(jnp.sum(jnp.where(lane == e, rank_ref[...], 0.0), axis=-1, keepdims=True),
                                  (rows, MOE_SLOTS))
    rank_row = rank_rows_ref[0, pl.ds(e, 1), :]

    @pl.when(e == 0)
    def _():
        o_ref[...] = jnp.zeros(o_ref.shape, F32)

    slot_sub = lax.broadcasted_iota(jnp.int32, (MOE_BLOCK, rows), 0)
    slot_lane = lax.broadcasted_iota(jnp.int32, (1, MOE_SLOTS), 1)

    def block(j, carry):
        lo = j * MOE_BLOCK
        pick = jnp.where(rank_row == (slot_sub + lo).astype(F32), 1.0, 0.0).astype(BF16)
        hg = jnp.dot(pick, h_ref[...], preferred_element_type=F32).astype(BF16)
        gate = jnp.dot(hg, wg_ref[0], preferred_element_type=F32)
        up = jnp.dot(hg, wu_ref[0], preferred_element_type=F32)
        y = jnp.dot((_silu(gate) * up).astype(BF16), wd_ref[0], preferred_element_type=F32).astype(BF16)
        y = jnp.concatenate([y, jnp.zeros((MOE_SLOTS - MOE_BLOCK, d), BF16)], axis=0)
        target = jnp.where(slot_lane < MOE_BLOCK, slot_lane + lo, -2).astype(F32)
        place = jnp.where(rank_lanes == target, 1.0, 0.0).astype(BF16)
        o_ref[...] += ge * jnp.dot(place, y, preferred_element_type=F32)
        return carry

    n_blocks = (count_ref[i * N_EXPERTS + e] + (MOE_BLOCK - 1)) // MOE_BLOCK
    lax.fori_loop(0, n_blocks, block, 0)

    @pl.when(e == N_EXPERTS - 1)
    def _():
        gate5 = mod_ref[pl.ds(i // tiles_per_batch, 1), 5 * D_MODEL:6 * D_MODEL]
        o_ref[...] = _rms(x_ref[...] + gate5 * o_ref[...]) * gf_ref[...]


def _moe_final(x, h, gates, mod, wg, wu, wd, g_final):
    bsz, n, d = x.shape
    assert n % MOE_TILE == 0
    n_tok = bsz * n
    gates = gates.reshape(n_tok, LANES)
    rank, rank_rows, counts = _route_plan(gates)
    counts = counts[:, 0, :N_EXPERTS].reshape(-1)
    kern = functools.partial(_moe_kernel, tiles_per_batch=n // MOE_TILE)
    const = lambda i, e, bnd: (0, 0)
    tok = lambda w: pl.BlockSpec((MOE_TILE, w), lambda i, e, bnd: (i, 0))
    per_expert = lambda a, b: pl.BlockSpec((1, a, b), lambda i, e, bnd: (e, 0, 0))
    out = pl.pallas_call(
        kern,
        out_shape=jax.ShapeDtypeStruct((n_tok, d), F32),
        grid_spec=pltpu.PrefetchScalarGridSpec(
            num_scalar_prefetch=1,
            grid=(n_tok // MOE_TILE, N_EXPERTS),
            in_specs=[tok(d), tok(d), tok(LANES), tok(LANES),
                      pl.BlockSpec((1, N_EXPERTS, MOE_TILE), lambda i, e, bnd: (i, 0, 0)),
                      pl.BlockSpec(mod.shape, const),
                      per_expert(d, D_FF_EXPERT), per_expert(d, D_FF_EXPERT), per_expert(D_FF_EXPERT, d),
                      pl.BlockSpec(g_final.shape, const)],
            out_specs=tok(d)),
        compiler_params=pltpu.CompilerParams(dimension_semantics=("arbitrary", "arbitrary"),
                                             vmem_limit_bytes=VMEM_LIMIT),
        name="moe_final",
    )(counts, x.reshape(n_tok, d), h.reshape(n_tok, d), gates, rank, rank_rows, mod, wg, wu, wd, g_final)
    return out.reshape(bsz, n, d)


def _dt_lanes(v):
    gap = jnp.zeros(v.shape[:-1] + (DT_REP - v.shape[-1],), v.dtype)
    tail = jnp.zeros(v.shape[:-1] + (LANES - 3 * DT_REP,), v.dtype)
    return jnp.concatenate([v, gap, v, gap, v, gap, tail], axis=-1)


def _head_slots(w, per_head, take0, take1, slot=HEAD_SLOT):
    k = w.shape[0]
    w = w.reshape(k, MLA_HEADS, per_head)[:, :, take0:take1]
    w = jnp.pad(w, ((0, 0), (0, 0), (0, slot - (take1 - take0))))
    return w.reshape(k, MLA_HEADS * slot).astype(BF16)


def _rope_tables(n_latent, ctx_len):
    t = jnp.arange(n_latent)
    pos = jnp.stack([(t // GRID_W).astype(F32), (t % GRID_W).astype(F32)], axis=1)
    n_freq = MLA_ROPE // 4
    inv_freq = ROPE_THETA ** (-jnp.arange(n_freq, dtype=F32) / n_freq)
    ang = pos[:, :, None] * inv_freq
    cos, sin = jnp.cos(ang), jnp.sin(ang)
    zero = jnp.zeros_like(sin)
    cos_r = jnp.stack([cos, cos], axis=2).reshape(n_latent, MLA_ROPE)
    sa_r = jnp.stack([zero, sin], axis=2).reshape(n_latent, MLA_ROPE)
    sb_r = jnp.stack([-sin, zero], axis=2).reshape(n_latent, MLA_ROPE)

    def slot(r, fill):
        lat = jnp.concatenate([jnp.full((n_latent, ROPE_LANE0), fill, F32), r,
                               jnp.full((n_latent, LANES - ROPE_LANE0 - MLA_ROPE), fill, F32)], axis=1)
        return jnp.concatenate([jnp.full((ctx_len, LANES), fill, F32), lat], axis=0)

    return slot(cos_r, 1.0), slot(sa_r, 0.0), slot(sb_r, 0.0)


def kernel(x, c, ctx, c_ctx, w_mod, b_mod, g_mix, w_in, w_sgu, b_sgu, g_sgu, beta_sgu, conv_w, conv_b, dt_bias,
           a_log, d_skip, g_ssd, g_q, w_uq, g_kv, w_ukv, w_out, g_ffn, w_gate, w_up, w_down, w_router, w_gate_e,
           w_up_e, w_down_e, g_final):
    bsz, n, d = x.shape
    ctx_len = ctx.shape[1]
    depth = w_in.shape[0]
    assert bsz <= CTX_MOD_ROW and ctx_len % CHUNK == 0 and n % CHUNK == 0

    cond = jnp.concatenate([c, jnp.zeros((CTX_MOD_ROW - bsz, d), F32), c_ctx[None],
                            jnp.zeros((MOD_ROWS - CTX_MOD_ROW - 1, d), F32)], axis=0)
    mod_all = _mod_table(cond, w_mod, b_mod)
    cos, sin_a, sin_b = _rope_tables(n, ctx_len)
    xa = (ctx, x)
    t_all = ctx_len + n
    rows_in = 768 if t_all % 768 == 0 else ctx_len

    out = None
    for layer in range(depth):
        last = layer == depth - 1
        mod = mod_all[layer]
        row = lambda v: v.reshape(1, -1)
        p_sgu, p_z, p_xbc, p_dt, p_q, p_kv, p_kr = _in_proj(
            xa, t_all, mod, row(g_mix[layer]), w_in, layer, ctx_len, rows_in)
        w_cat = jnp.transpose(w_sgu[layer], (1, 0, 2)).reshape(CHUNK, SGU_HEADS * CHUNK).astype(BF16)
        bias = jnp.repeat(b_sgu[layer].T, SGU_HEAD_DIM, axis=1)
        o_sgu = _sgu(p_sgu, w_cat, bias, row(g_sgu[layer]), row(beta_sgu[layer]), rows_in)
        o_ssd = _ssd(p_z, p_xbc, p_dt, conv_w[layer], row(conv_b[layer]), _dt_lanes(dt_bias[layer].reshape(1, -1)),
                     _dt_lanes(a_log[layer].reshape(1, -1)), row(jnp.repeat(d_skip[layer], SSD_HEAD_DIM)), row(g_ssd[layer]),
                     ctx_len)
        per_q = MLA_NOPE + MLA_ROPE
        per_kv = MLA_NOPE + MLA_V
        q, k, v = _mla_proj(p_q, p_kv, p_kr, row(g_q[layer]), _head_slots(w_uq[layer], per_q, 0, per_q),
                            row(g_kv[layer]), _head_slots(w_ukv[layer], per_kv, 0, MLA_NOPE),
                            _head_slots(w_ukv[layer], per_kv, MLA_NOPE, per_kv, MLA_V).T, cos, sin_a, sin_b,
                            rows_in)
        o_mla = _attention(q, k, v, ctx_len, latent=True)
        o_mla_ctx = None if last else _attention(q, k, v, ctx_len, latent=False)
        wo = w_out[layer].astype(BF16)
        i = layer // 2
        if layer % 2 == 0:
            if last:
                raise NotImplementedError("final dense channel mixer")
            xa = _mix_ffn(xa, o_sgu, o_ssd, o_mla, o_mla_ctx, mod, wo, row(g_ffn[layer]),
                          w_gate[i].astype(BF16), w_up[i].astype(BF16), w_down[i].astype(BF16), ctx_len)
        else:
            if not last:
                raise NotImplementedError("expert channel mixer on a non-final layer")
            wr_hi = w_router[i].astype(BF16)
            wr_lo = (w_router[i] - wr_hi.astype(F32)).astype(BF16)
            pad = lambda w: jnp.pad(w, ((0, 0), (0, LANES - N_EXPERTS)))
            wr = jnp.concatenate([pad(wr_hi), pad(wr_lo)], axis=1)
            x_mid, h_ffn, gates = _out_proj_route(xa, o_sgu, o_ssd, o_mla, mod, wo, row(g_ffn[layer]), wr,
                                                  ctx_len)
            out = _moe_final(x_mid, h_ffn, gates, mod, w_gate_e[i].astype(BF16), w_up_e[i].astype(BF16),
                             w_down_e[i].astype(BF16), row(g_final))
    return out
```

```python
import functools

import jax
import jax.numpy as jnp
from jax import lax
from jax.experimental import pallas as pl
from jax.experimental.pallas import tpu as pltpu

F32 = jnp.float32
BF16 = jnp.bfloat16
HIGHEST = lax.Precision.HIGHEST

D_MODEL = 1024
EPS = 1e-6
N_MOD = 6
GRID_W = 64
CHUNK = 128

SGU_HEADS = 4
SGU_HEAD_DIM = 64
SGU_WIDTH = 256

SSD_HEADS = 6
SSD_HEAD_DIM = 64
SSD_WIDTH = 384
SSD_GROUPS = 2
SSD_HPG = 3
SSD_STATE = 128
SSD_CONV = 5
SSD_CONV_CH = 896
SSD_GW = SSD_HPG * SSD_HEAD_DIM
DT_REP = 16

MLA_HEADS = 6
MLA_NOPE = 64
MLA_ROPE = 32
MLA_V = 64
MLA_WIDTH = 384
Q_LORA = 384
KV_LORA = 256
ROPE_THETA = 10000.0
MLA_SCALE = (MLA_NOPE + MLA_ROPE) ** -0.5
LOG2_E = 1.4426950408889634
HEAD_SLOT = 128
ROPE_LANE0 = MLA_NOPE
ROPE_HALF = MLA_ROPE // 4
ROPE_PARTNER_STARTS = (ROPE_HALF, 0, 3 * ROPE_HALF, 2 * ROPE_HALF)
assert HEAD_SLOT == ROPE_LANE0 + 2 * MLA_ROPE

D_FF = 2816
N_EXPERTS = 8
D_FF_EXPERT = 1408

LANES = 128
MOD_ROWS = 16
CTX_MOD_ROW = 8

IN_COLS = (("sgu", 0, 512), ("z", 512, 896), ("xbc", 896, 1792), ("dt", 1792, 1920), ("cq", 1920, 2304),
           ("ckv", 2304, 2560), ("kr", 2560, 2688))
IN_PAD_WIDTH = 2688

VMEM_LIMIT = 56 * 1024 * 1024


def _sigmoid(x):
    return 1.0 / (1.0 + jnp.exp(-x))


def _silu(x):
    return x * _sigmoid(x)


def _rms(x):
    return x * lax.rsqrt(jnp.mean(x * x, axis=-1, keepdims=True) + EPS)


def _aligned(v, m):
    return v if isinstance(v, int) else pl.multiple_of(v, m)


def _row_select(mod_ref, b, tile, rows, ctx_len, col0):
    row = tile * rows + lax.broadcasted_iota(jnp.int32, (rows, 1), 0)
    is_ctx = row < ctx_len
    mb = mod_ref[pl.ds(b, 1), col0:col0 + D_MODEL]
    mc = mod_ref[CTX_MOD_ROW:CTX_MOD_ROW + 1, col0:col0 + D_MODEL]
    return jnp.where(is_ctx, mc, mb)


def _mod_kernel(cond_ref, w_ref, b_ref, o_ref):
    s = _silu(cond_ref[...])
    o_ref[0] = jnp.dot(s, w_ref[0], precision=HIGHEST, preferred_element_type=F32) + b_ref[0]


def _mod_table(cond, w_mod, b_mod):
    n_layers, d, width = w_mod.shape
    cb = 1536
    return pl.pallas_call(
        _mod_kernel,
        out_shape=jax.ShapeDtypeStruct((n_layers, MOD_ROWS, width), F32),
        grid=(n_layers, width // cb),
        in_specs=[pl.BlockSpec((MOD_ROWS, d), lambda l, j: (0, 0)),
                  pl.BlockSpec((1, d, cb), lambda l, j: (l, 0, j)),
                  pl.BlockSpec((1, 1, cb), lambda l, j: (l, 0, j))],
        out_specs=pl.BlockSpec((1, MOD_ROWS, cb), lambda l, j: (l, 0, j)),
        compiler_params=pltpu.CompilerParams(dimension_semantics=("arbitrary", "arbitrary"),
                                             vmem_limit_bytes=VMEM_LIMIT),
        name="mod_table",
    )(cond, w_mod, b_mod.reshape(n_layers, 1, width))


def _token_specs(tokens, ctx_len, rows, d):
    if not isinstance(tokens, tuple):
        return [pl.BlockSpec((1, rows, d), lambda b, t: (b, t, 0))], [tokens]
    ctx, lat = tokens
    n_sub = rows // ctx_len
    assert rows == n_sub * ctx_len
    piece = lambda j: pl.BlockSpec((1, ctx_len, d), lambda b, t: (b, jnp.maximum(n_sub * t + j - 1, 0), 0))
    return ([pl.BlockSpec((1, ctx_len, d), lambda b, t: (b, 0, 0))] + [piece(j) for j in range(n_sub)],
            [ctx] + [lat] * n_sub)


def _tokens_tile(refs, t):
    if len(refs) == 1:
        return refs[0][0]
    first = jnp.where(t == 0, refs[0][0], refs[1][0])
    return first if len(refs) == 2 else jnp.concatenate([first] + [r[0] for r in refs[2:]], axis=0)


IN_RAW_DT = (1792, 1804)
IN_RAW_KR = IN_RAW_DT[1] + Q_LORA + KV_LORA
IN_RAW_WIDTH = IN_RAW_KR + MLA_ROPE


def _pad_in_weight(wraw_ref, w_ref):
    dt0, dt1 = IN_RAW_DT
    for r0 in range(0, D_MODEL, LANES):
        rows = slice(r0, r0 + LANES)
        z = lambda n: jnp.zeros((LANES, n), F32)
        dt = wraw_ref[0, rows, dt0:dt1]
        gap = z(DT_REP - (dt1 - dt0))
        w_ref[rows, :] = jnp.concatenate(
            [wraw_ref[0, rows, 0:dt0], dt, gap, dt, gap, dt, gap, z(LANES - 3 * DT_REP),
             wraw_ref[0, rows, dt1:IN_RAW_KR], z(ROPE_LANE0), wraw_ref[0, rows, IN_RAW_KR:IN_RAW_WIDTH]]
            + [wraw_ref[0, rows, IN_RAW_KR + c0:IN_RAW_KR + c0 + ROPE_HALF] for c0 in ROPE_PARTNER_STARTS],
            axis=1).astype(BF16)


def _in_kernel(*refs, rows, ctx_len, n_tok_refs):
    tok_refs = refs[:n_tok_refs]
    mod_ref, g_ref, wraw_ref, *out_refs, w_ref = refs[n_tok_refs:]
    b = pl.program_id(0)
    t = pl.program_id(1)

    @pl.when(jnp.logical_and(b == 0, t == 0))
    def _():
        _pad_in_weight(wraw_ref, w_ref)

    xn = _rms(_tokens_tile(tok_refs, t)) * g_ref[...]
    shift = _row_select(mod_ref, b, t, rows, ctx_len, 0)
    scale = _row_select(mod_ref, b, t, rows, ctx_len, D_MODEL)
    h = (xn * (1.0 + scale) + shift).astype(BF16)
    p = jnp.dot(h, w_ref[...], preferred_element_type=F32)
    for o_ref, (_, c0, c1) in zip(out_refs, IN_COLS):
        o_ref[0] = p[:, c0:c1]


def _in_proj(tokens, t_all, mod, g, w_in, layer, ctx_len, rows):
    tok_specs, tok_args = _token_specs(tokens, ctx_len, rows, D_MODEL)
    bsz = tok_args[0].shape[0]
    assert w_in.shape[1:] == (D_MODEL, IN_RAW_WIDTH)
    kern = functools.partial(_in_kernel, rows=rows, ctx_len=ctx_len, n_tok_refs=len(tok_args))
    return pl.pallas_call(
        kern,
        out_shape=[jax.ShapeDtypeStruct((bsz, t_all, c1 - c0), F32) for _, c0, c1 in IN_COLS],
        grid=(bsz, t_all // rows),
        in_specs=tok_specs + [pl.BlockSpec(mod.shape, lambda b, t: (0, 0)),
                              pl.BlockSpec((1, D_MODEL), lambda b, t: (0, 0)),
                              pl.BlockSpec((1,) + w_in.shape[1:], lambda b, t: (layer, 0, 0))],
        out_specs=[pl.BlockSpec((1, rows, c1 - c0), lambda b, t: (b, t, 0)) for _, c0, c1 in IN_COLS],
        scratch_shapes=[pltpu.VMEM((D_MODEL, IN_PAD_WIDTH), BF16)],
        compiler_params=pltpu.CompilerParams(dimension_semantics=("arbitrary", "arbitrary"),
                                             vmem_limit_bytes=VMEM_LIMIT),
        name="in_proj",
    )(*tok_args, mod, g, w_in)


def _sgu_kernel(p_ref, w_ref, bias_ref, g_ref, beta_ref, o_ref, *, n_chunks):
    lane = lax.broadcasted_iota(jnp.int32, (1, SGU_WIDTH), 1)
    head_of_lane = lane // SGU_HEAD_DIM
    w = w_ref[...]
    bias = bias_ref[...]
    c0 = 0.7978845608028654
    for c in range(n_chunks):
        p = p_ref[0, c * CHUNK:(c + 1) * CHUNK, :]
        ge = 0.5 * p * (1.0 + jnp.tanh(c0 * (p + 0.044715 * (p * p * p))))
        u = ge[:, :SGU_WIDTH]
        v = ge[:, SGU_WIDTH:]
        mu = jnp.mean(v, axis=-1, keepdims=True)
        vc = v - mu
        vn = vc * lax.rsqrt(jnp.mean(vc * vc, axis=-1, keepdims=True) + EPS) * g_ref[...] + beta_ref[...]
        stacked = jnp.concatenate(
            [jnp.where(head_of_lane == h, vn, 0.0).astype(BF16) for h in range(SGU_HEADS)], axis=0)
        mixed = jnp.dot(w, stacked, preferred_element_type=F32) + bias
        o_ref[0, c * CHUNK:(c + 1) * CHUNK, :] = (u * mixed).astype(BF16)


def _sgu(p_sgu, w_cat, bias, g, beta, rows):
    bsz, t_all, width = p_sgu.shape
    kern = functools.partial(_sgu_kernel, n_chunks=rows // CHUNK)
    const = lambda b, t: (0, 0)
    return pl.pallas_call(
        kern,
        out_shape=jax.ShapeDtypeStruct((bsz, t_all, SGU_WIDTH), BF16),
        grid=(bsz, t_all // rows),
        in_specs=[pl.BlockSpec((1, rows, width), lambda b, t: (b, t, 0)),
                  pl.BlockSpec(w_cat.shape, const), pl.BlockSpec(bias.shape, const),
                  pl.BlockSpec(g.shape, const), pl.BlockSpec(beta.shape, const)],
        out_specs=pl.BlockSpec((1, rows, SGU_WIDTH), lambda b, t: (b, t, 0)),
        compiler_params=pltpu.CompilerParams(dimension_semantics=("arbitrary", "arbitrary"),
                                             vmem_limit_bytes=VMEM_LIMIT),
        name="sgu",
    )(p_sgu, w_cat, bias, g, beta)


def _ssd_kernel(z_ref, xbc_ref, dt_ref, cw_ref, cb_ref, dtb_ref, alog_ref, skip_ref, g_ref, o_ref,
                xc_ref, yf_ref, yb_ref, st_ref, *, n_blk, n_ctx_blk):
    def conv_block(blk, seg_start, seg_end):
        r0 = _aligned(blk * CHUNK, CHUNK)
        for cg in range(SSD_CONV_CH // LANES):
            cols = slice(cg * LANES, (cg + 1) * LANES)
            zeros = jnp.zeros((8, LANES), F32)
            top = zeros if seg_start else xbc_ref[0, pl.ds(_aligned(r0 - 8, 8), 8), cols]
            bot = zeros if seg_end else xbc_ref[0, pl.ds(_aligned(r0 + CHUNK, 8), 8), cols]
            xw = jnp.concatenate([top, xbc_ref[0, pl.ds(r0, CHUNK), cols], bot], axis=0)
            acc = cb_ref[:, cols] + cw_ref[0:1, cols] * xw[6:6 + CHUNK]
            for k in range(1, SSD_CONV):
                acc = acc + cw_ref[k:k + 1, cols] * xw[6 + k:6 + k + CHUNK]
            xc_ref[pl.ds(r0, CHUNK), cols] = _silu(acc)

    static_blocks = sorted(set(list(range(n_ctx_blk)) + [n_ctx_blk, n_blk - 1]))
    for blk in static_blocks:
        conv_block(blk, blk == 0 or blk == n_ctx_blk, blk == n_ctx_blk - 1 or blk == n_blk - 1)
    if n_blk - 1 > n_ctx_blk + 1:
        def conv_body(blk, carry):
            conv_block(blk, False, False)
            return carry
        lax.fori_loop(n_ctx_blk + 1, n_blk - 1, conv_body, 0)

    li = lax.broadcasted_iota(jnp.int32, (CHUNK, CHUNK), 0)
    si = lax.broadcasted_iota(jnp.int32, (CHUNK, CHUNK), 1)
    a_neg = -jnp.exp(alog_ref[...])

    def chunk_step(c, direction):
        r0 = pl.multiple_of(c * CHUNK, CHUNK)
        rows = pl.ds(r0, CHUNK)
        mask = (si <= li) if direction == 0 else (si >= li)
        tri = jnp.where(mask, 1.0, 0.0).astype(BF16)
        dtr = dt_ref[0, rows, :] + dtb_ref[...]
        dt = jnp.maximum(dtr, 0.0) + jnp.log1p(jnp.exp(-jnp.abs(dtr)))
        adt = dt * a_neg
        p1 = adt.astype(BF16)
        r1 = adt - p1.astype(F32)
        p2 = r1.astype(BF16)
        p3 = (r1 - p2.astype(F32)).astype(BF16)
        parts = jnp.dot(tri, jnp.concatenate([p1, p2, p3], axis=1), preferred_element_type=F32)
        acs = parts[:, :LANES] + parts[:, LANES:2 * LANES] + parts[:, 2 * LANES:]
        end = CHUNK - 1 if direction == 0 else 0
        tot = acs[end:end + 1, :]
        to_end_dt = jnp.exp(tot - acs) * dt
        chunk_decay = jnp.exp(tot)
        per_head_rows = jnp.where(si < DT_REP, acs, jnp.where(si < 2 * DT_REP, dt, to_end_dt)).T
        first_half = si < SSD_HEAD_DIM
        cbs, bm_ts, y_offs = [], [], []
        state = st_ref[direction]
        state16 = state.astype(BF16)
        for g in range(SSD_GROUPS):
            bm = xc_ref[rows, SSD_WIDTH + g * SSD_STATE:SSD_WIDTH + (g + 1) * SSD_STATE]
            cm = xc_ref[rows, SSD_WIDTH + SSD_GROUPS * SSD_STATE + g * SSD_STATE:
                        SSD_WIDTH + SSD_GROUPS * SSD_STATE + (g + 1) * SSD_STATE]
            cm16 = cm.astype(BF16)
            cbs.append(lax.dot_general(cm16, bm.astype(BF16), (((1,), (1,)), ((), ())),
                                       preferred_element_type=F32))
            bm_ts.append(bm.T)
            y_offs.append(jnp.dot(cm16, state16, preferred_element_type=F32))
        y_pairs, state_pairs = [], []
        for k in range(SSD_HEADS // 2):
            pair = slice(k * LANES, (k + 1) * LANES)
            xs16 = xc_ref[rows, pair].astype(BF16)
            y_diag, contrib, exp_a, dec, y_off = [], [], [], [], []
            for hh in (2 * k, 2 * k + 1):
                g = hh // SSD_HPG
                col = direction * SSD_HEADS + hh
                a_col = jnp.broadcast_to(acs[:, col:col + 1], (CHUNK, CHUNK))
                seg = a_col - per_head_rows[col:col + 1, :]
                decay = jnp.exp(jnp.where(mask, seg, -1e30))
                m = (cbs[g] * decay * per_head_rows[DT_REP + col:DT_REP + col + 1, :]).astype(BF16)
                y_diag.append(jnp.dot(m, xs16, preferred_element_type=F32))
                exp_a.append(jnp.exp(a_col))
                y_off.append(y_offs[g][:, pair])
                lhs = (bm_ts[g] * per_head_rows[2 * DT_REP + col:2 * DT_REP + col + 1, :]).astype(BF16)
                contrib.append(jnp.dot(lhs, xs16, preferred_element_type=F32))
                dec.append(jnp.broadcast_to(chunk_decay[:, col:col + 1], (SSD_STATE, LANES)))
            pick = lambda ab: jnp.where(first_half, ab[0], ab[1])
            y_pairs.append(pick(y_diag) + pick(exp_a) * pick(y_off))
            state_pairs.append(pick(dec) * state[:, pair] + pick(contrib))
        st_ref[direction] = jnp.concatenate(state_pairs, axis=1)
        y = jnp.concatenate(y_pairs, axis=1)
        if direction == 0:
            yf_ref[rows, :] = skip_ref[...] * xc_ref[rows, 0:SSD_WIDTH] + y
        else:
            yb_ref[rows, :] = y

    def scan_body(step, carry):
        chunk_step(step, 0)
        chunk_step(jnp.where(step < n_ctx_blk, n_ctx_blk - 1 - step, n_blk - 1 - (step - n_ctx_blk)), 1)
        return carry

    st_ref[...] = jnp.zeros(st_ref.shape, F32)
    lax.fori_loop(0, n_blk, scan_body, 0)

    def out_body(c, carry):
        rows = pl.ds(pl.multiple_of(c * CHUNK, CHUNK), CHUNK)
        gated = (yf_ref[rows, :] + yb_ref[rows, :]) * _silu(z_ref[0, rows, :])
        o_ref[0, rows, :] = (_rms(gated) * g_ref[...]).astype(BF16)
        return carry

    lax.fori_loop(0, n_blk, out_body, 0)


def _ssd(p_z, p_xbc, p_dt, conv_w, conv_b, dt_bias, a_log, skip, g, ctx_len):
    bsz, t_all, _ = p_z.shape
    kern = functools.partial(_ssd_kernel, n_blk=t_all // CHUNK, n_ctx_blk=ctx_len // CHUNK)
    const = lambda b: (0, 0)
    per_b = lambda w: pl.BlockSpec((1, t_all, w), lambda b: (b, 0, 0))
    return pl.pallas_call(
        kern,
        out_shape=jax.ShapeDtypeStruct((bsz, t_all, SSD_WIDTH), BF16),
        grid=(bsz,),
        in_specs=[per_b(SSD_WIDTH), per_b(SSD_CONV_CH), per_b(LANES),
                  pl.BlockSpec(conv_w.shape, const), pl.BlockSpec(conv_b.shape, const),
                  pl.BlockSpec(dt_bias.shape, const), pl.BlockSpec(a_log.shape, const),
                  pl.BlockSpec(skip.shape, const), pl.BlockSpec(g.shape, const)],
        out_specs=per_b(SSD_WIDTH),
        scratch_shapes=[pltpu.VMEM((t_all, SSD_CONV_CH), F32), pltpu.VMEM((t_all, SSD_WIDTH), F32),
                        pltpu.VMEM((t_all, SSD_WIDTH), F32),
                        pltpu.VMEM((2, SSD_STATE, SSD_WIDTH), F32)],
        compiler_params=pltpu.CompilerParams(dimension_semantics=("arbitrary",), vmem_limit_bytes=VMEM_LIMIT),
        name="ssd",
    )(p_z, p_xbc, p_dt, conv_w, conv_b, dt_bias, a_log, skip, g)


def _mla_proj_kernel(pq_ref, pkv_ref, pkr_ref, gq_ref, wq_ref, gkv_ref, wk_ref, wv_ref, cos_ref, sin_ref,
                     q_ref, k_ref, v_ref):
    cos = cos_ref[...]
    sin = sin_ref[...]

    def rope(t):
        return t * cos + pltpu.roll(t, LANES - MLA_ROPE, 1) * sin

    qn = (_rms(pq_ref[0]) * gq_ref[...]).astype(BF16)
    q = jnp.dot(qn, wq_ref[...], preferred_element_type=F32)
    kvn = (_rms(pkv_ref[0]) * gkv_ref[...]).astype(BF16)
    k = jnp.dot(kvn, wk_ref[...], preferred_element_type=F32)
    v_t = lax.dot_general(wv_ref[...], kvn, (((1,), (1,)), ((), ())),
                          preferred_element_type=F32).astype(BF16)
    kr = rope(pkr_ref[0])
    for h in range(MLA_HEADS):
        slot = slice(h * HEAD_SLOT, (h + 1) * HEAD_SLOT)
        q_ref[0, h] = (rope(q[:, slot]) * (MLA_SCALE * LOG2_E)).astype(BF16)
        k_ref[0, h] = (k[:, slot] + kr).astype(BF16)
        v_ref[0, h] = v_t[h * MLA_V:(h + 1) * MLA_V, :]


def _mla_proj(p_q, p_kv, p_kr, g_q, w_q, g_kv, w_k, w_v, cos, sin, rows):
    bsz, t_all, _ = p_q.shape
    const = lambda b, t: (0, 0)
    tok = lambda w: pl.BlockSpec((1, rows, w), lambda b, t: (b, t, 0))
    tab = pl.BlockSpec((rows, LANES), lambda b, t: (t, 0))
    out = jax.ShapeDtypeStruct((bsz, MLA_HEADS, t_all, HEAD_SLOT), BF16)
    out_vt = jax.ShapeDtypeStruct((bsz, MLA_HEADS, MLA_V, t_all), BF16)
    head_tok = pl.BlockSpec((1, MLA_HEADS, rows, HEAD_SLOT), lambda b, t: (b, 0, t, 0))
    return pl.pallas_call(
        _mla_proj_kernel,
        out_shape=[out, out, out_vt],
        grid=(bsz, t_all // rows),
        in_specs=[tok(Q_LORA), tok(KV_LORA), tok(LANES),
                  pl.BlockSpec(g_q.shape, const), pl.BlockSpec(w_q.shape, const),
                  pl.BlockSpec(g_kv.shape, const), pl.BlockSpec(w_k.shape, const), pl.BlockSpec(w_v.shape, const),
                  tab, tab],
        out_specs=[head_tok, head_tok, pl.BlockSpec((1, MLA_HEADS, MLA_V, rows), lambda b, t: (b, 0, 0, t))],
        compiler_params=pltpu.CompilerParams(dimension_semantics=("arbitrary", "arbitrary"),
                                             vmem_limit_bytes=VMEM_LIMIT),
        name="mla_proj",
    )(p_q, p_kv, p_kr, g_q, w_q, g_kv, w_k, w_v, cos, sin)


ATTN_Q = 512
ATTN_BUFS = 2


def _attn_kernel(*refs, n_q_blocks):
    q_refs = refs[:n_q_blocks]
    k_ref, vt_ref, o_ref, *bufs, ot_ref = refs[n_q_blocks:]
    s_bufs, m_bufs = bufs[:ATTN_BUFS], bufs[ATTN_BUFS:]
    def scores(h, s_ref, m_ref):
        qh = jnp.concatenate([r[0, h] for r in q_refs], axis=0)
        s_t = lax.dot_general(k_ref[0, h], qh, (((1,), (1,)), ((), ())), preferred_element_type=F32)
        s_ref[...] = s_t
        m_ref[...] = jnp.broadcast_to(jnp.max(s_t, axis=0, keepdims=True), m_ref.shape)

    def weigh(h, s_ref, m_ref):
        p_t = jnp.exp2(s_ref[...] - m_ref[0:1, :])
        denom = jnp.sum(p_t, axis=0, keepdims=True)
        o_t = jnp.dot(vt_ref[0, h], p_t.astype(BF16), preferred_element_type=F32)
        ot_ref[pl.ds(pl.multiple_of(h * MLA_V, MLA_V), MLA_V), :] = o_t / denom

    scores(0, s_bufs[0], m_bufs[0])

    def head_group(j, carry):
        h0 = ATTN_BUFS * j
        for i in range(ATTN_BUFS):
            nxt = (i + 1) % ATTN_BUFS
            scores(jnp.minimum(h0 + i + 1, MLA_HEADS - 1), s_bufs[nxt], m_bufs[nxt])
            weigh(h0 + i, s_bufs[i], m_bufs[i])
        return carry

    lax.fori_loop(0, MLA_HEADS // ATTN_BUFS, head_group, 0)
    o_ref[0] = ot_ref[...].T.astype(BF16)


def _attention(q, k, vt, ctx_len, latent):
    bsz, n_heads, t_all, width = q.shape
    if latent:
        n_q_blocks = ATTN_Q // ctx_len
        n_rows, n_keys, tq = t_all - ctx_len, t_all, ATTN_Q
        q_specs = [pl.BlockSpec((1, n_heads, ctx_len, width),
                                functools.partial(lambda b, i, j: (b, 0, n_q_blocks * i + 1 + j, 0), j=j))
                   for j in range(n_q_blocks)]
    else:
        n_q_blocks = 1
        n_rows, n_keys, tq = ctx_len, ctx_len, ctx_len
        q_specs = [pl.BlockSpec((1, n_heads, ctx_len, width), lambda b, i: (b, 0, 0, 0))]
    assert n_rows % tq == 0
    return pl.pallas_call(
        functools.partial(_attn_kernel, n_q_blocks=n_q_blocks),
        out_shape=jax.ShapeDtypeStruct((bsz, n_rows, MLA_WIDTH), BF16),
        grid=(bsz, n_rows // tq),
        in_specs=q_specs + [pl.BlockSpec((1, n_heads, n_keys, width), lambda b, i: (b, 0, 0, 0)),
                            pl.BlockSpec((1, n_heads, MLA_V, n_keys), lambda b, i: (b, 0, 0, 0))],
        out_specs=pl.BlockSpec((1, tq, MLA_WIDTH), lambda b, i: (b, i, 0)),
        scratch_shapes=([pltpu.VMEM((n_keys, tq), F32)] * ATTN_BUFS + [pltpu.VMEM((8, tq), F32)] * ATTN_BUFS
                        + [pltpu.VMEM((MLA_WIDTH, tq), F32)]),
        compiler_params=pltpu.CompilerParams(dimension_semantics=("arbitrary", "arbitrary"),
                                             vmem_limit_bytes=VMEM_LIMIT),
        name="attention",
    )(*([q] * n_q_blocks), k, vt)


def _mix_residual(x_in, o1_ref, o2_ref, o3, mod_ref, wo_ref, g_ref, b, t, rows, ctx_len):
    r1, r2 = SGU_WIDTH, SGU_WIDTH + SSD_WIDTH
    mix = (jnp.dot(o1_ref[0], wo_ref[0:r1, :], preferred_element_type=F32)
           + jnp.dot(o2_ref[0], wo_ref[r1:r2, :], preferred_element_type=F32)
           + jnp.dot(o3, wo_ref[r2:, :], preferred_element_type=F32))
    x = x_in + _row_select(mod_ref, b, t, rows, ctx_len, 2 * D_MODEL) * mix
    shift = _row_select(mod_ref, b, t, rows, ctx_len, 3 * D_MODEL)
    scale = _row_select(mod_ref, b, t, rows, ctx_len, 4 * D_MODEL)
    return x, _rms(x) * g_ref[...] * (1.0 + scale) + shift


def _out_kernel(x_ref, o1_ref, o2_ref, o3_ref, mod_ref, wo_ref, g_ref, wr_ref,
                xo_ref, h_ref, gate_ref, *, rows, ctx_len):
    b = pl.program_id(0)
    t = pl.program_id(1) + 1
    x, h = _mix_residual(x_ref[0], o1_ref, o2_ref, o3_ref[0], mod_ref, wo_ref, g_ref, b, t, rows, ctx_len)
    xo_ref[0] = x
    h_ref[0] = h.astype(BF16)
    h_hi = h.astype(BF16)
    h_lo = (h - h_hi.astype(F32)).astype(BF16)
    by_hi = jnp.dot(h_hi, wr_ref[...], preferred_element_type=F32)
    logits = by_hi[:, :LANES] + by_hi[:, LANES:] + jnp.dot(h_lo, wr_ref[:, :LANES], preferred_element_type=F32)
    lane = lax.broadcasted_iota(jnp.int32, logits.shape, 1)
    lane_f = lane.astype(F32)
    lg = jnp.where(lane < N_EXPERTS, logits, -jnp.inf)
    m1 = jnp.max(lg, axis=-1, keepdims=True)
    i1 = jnp.min(jnp.where(lg == m1, lane_f, float(LANES)), axis=-1, keepdims=True)
    lg2 = jnp.where(lane_f == i1, -jnp.inf, lg)
    m2 = jnp.max(lg2, axis=-1, keepdims=True)
    i2 = jnp.min(jnp.where(lg2 == m2, lane_f, float(LANES)), axis=-1, keepdims=True)
    e2 = jnp.exp(m2 - m1)
    w_top = 1.0 / (1.0 + e2)
    gate_ref[0] = jnp.where(lane_f == i1, w_top, 0.0) + jnp.where(lane_f == i2, e2 * w_top, 0.0)


def _out_proj_route(xa, o_sgu, o_ssd, o_mla, mod, wo, g, w_router, ctx_len):
    bsz, t_all, d = xa.shape
    rows = ctx_len
    n_out = t_all - ctx_len
    kern = functools.partial(_out_kernel, rows=rows, ctx_len=ctx_len)
    const = lambda b, t: (0, 0)
    tok = lambda w: pl.BlockSpec((1, rows, w), lambda b, t: (b, t + 1, 0))
    out_tok = lambda w: pl.BlockSpec((1, rows, w), lambda b, t: (b, t, 0))
    return pl.pallas_call(
        kern,
        out_shape=[jax.ShapeDtypeStruct((bsz, n_out, d), F32), jax.ShapeDtypeStruct((bsz, n_out, d), BF16),
                   jax.ShapeDtypeStruct((bsz, n_out, LANES), F32)],
        grid=(bsz, n_out // rows),
        in_specs=[tok(d), tok(SGU_WIDTH), tok(SSD_WIDTH), out_tok(MLA_WIDTH), pl.BlockSpec(mod.shape, const),
                  pl.BlockSpec(wo.shape, const), pl.BlockSpec(g.shape, const), pl.BlockSpec(w_router.shape, const)],
        out_specs=[out_tok(d), out_tok(d), out_tok(LANES)],
        compiler_params=pltpu.CompilerParams(dimension_semantics=("arbitrary", "arbitrary"),
                                             vmem_limit_bytes=VMEM_LIMIT),
        name="out_proj",
    )(xa, o_sgu, o_ssd, o_mla, mod, wo, g, w_router)


FF_SPLITS = ((0, 1536), (1536, D_FF))


def _mix_ffn_kernel(*refs, rows, ctx_len, n_tok_refs):
    tok_refs = refs[:n_tok_refs]
    (o1_ref, o2_ref, o3_ref, o3c_ref, mod_ref, wo_ref, g_ref, wg_ref, wu_ref, wd_ref,
     o_ref) = refs[n_tok_refs:]
    b = pl.program_id(0)
    t = pl.program_id(1)
    o3 = jnp.where(t == 0, o3c_ref[0], o3_ref[0])
    x, h = _mix_residual(_tokens_tile(tok_refs, t), o1_ref, o2_ref, o3, mod_ref, wo_ref, g_ref, b, t, rows, ctx_len)
    h = h.astype(BF16)
    acc = None
    for c0, c1 in FF_SPLITS:
        gate = jnp.dot(h, wg_ref[:, c0:c1], preferred_element_type=F32)
        up = jnp.dot(h, wu_ref[:, c0:c1], preferred_element_type=F32)
        part = jnp.dot((_silu(gate) * up).astype(BF16), wd_ref[c0:c1, :], preferred_element_type=F32)
        acc = part if acc is None else acc + part
    o_ref[0] = x + _row_select(mod_ref, b, t, rows, ctx_len, 5 * D_MODEL) * acc


def _mix_ffn(tokens, o_sgu, o_ssd, o_mla, o_mla_ctx, mod, wo, g, wg, wu, wd, ctx_len):
    bsz, t_all, _ = o_sgu.shape
    d = D_MODEL
    rows = ctx_len
    tok_specs, tok_args = _token_specs(tokens, ctx_len, rows, d)
    kern = functools.partial(_mix_ffn_kernel, rows=rows, ctx_len=ctx_len, n_tok_refs=len(tok_args))
    const = lambda b, t: (0, 0)
    tok = lambda w: pl.BlockSpec((1, rows, w), lambda b, t: (b, t, 0))
    whole = lambda a: pl.BlockSpec(a.shape, const)
    return pl.pallas_call(
        kern,
        out_shape=jax.ShapeDtypeStruct((bsz, t_all, d), F32),
        grid=(bsz, t_all // rows),
        in_specs=tok_specs + [tok(SGU_WIDTH), tok(SSD_WIDTH),
                              pl.BlockSpec((1, rows, MLA_WIDTH), lambda b, t: (b, jnp.maximum(t - 1, 0), 0)),
                              pl.BlockSpec((1, rows, MLA_WIDTH), lambda b, t: (b, 0, 0)),
                              whole(mod), whole(wo), whole(g), whole(wg), whole(wu), whole(wd)],
        out_specs=tok(d),
        compiler_params=pltpu.CompilerParams(dimension_semantics=("arbitrary", "arbitrary"),
                                             vmem_limit_bytes=VMEM_LIMIT),
        name="mix_ffn",
    )(*tok_args, o_sgu, o_ssd, o_mla, o_mla_ctx, mod, wo, g, wg, wu, wd)


MOE_TILE = 512
MOE_BLOCK = 144
MOE_SLOTS = 256


def _plan_kernel(gate_ref, rank_ref, rank_rows_ref, count_ref):
    rows = gate_ref.shape[0]
    routed = gate_ref[...] > 0.0
    ti = lax.broadcasted_iota(jnp.int32, (rows, rows), 0)
    tj = lax.broadcasted_iota(jnp.int32, (rows, rows), 1)
    earlier = jnp.where(tj < ti, 1.0, 0.0).astype(BF16)
    ones = jnp.where(routed, 1.0, 0.0)
    before = jnp.dot(earlier, ones.astype(BF16), preferred_element_type=F32)
    rank = jnp.where(routed, before, -1.0)
    rank_ref[...] = rank
    rank_rows_ref[0] = rank.T[0:N_EXPERTS, :]
    count_ref[0] = jnp.broadcast_to(jnp.sum(ones, axis=0, keepdims=True), (8, LANES)).astype(jnp.int32)


def _route_plan(gates):
    n_tok = gates.shape[0]
    n_tiles = n_tok // MOE_TILE
    return pl.pallas_call(
        _plan_kernel,
        out_shape=[jax.ShapeDtypeStruct((n_tok, LANES), F32),
                   jax.ShapeDtypeStruct((n_tiles, N_EXPERTS, MOE_TILE), F32),
                   jax.ShapeDtypeStruct((n_tiles, 8, LANES), jnp.int32)],
        grid=(n_tiles,),
        in_specs=[pl.BlockSpec((MOE_TILE, LANES), lambda i: (i, 0))],
        out_specs=[pl.BlockSpec((MOE_TILE, LANES), lambda i: (i, 0)),
                   pl.BlockSpec((1, N_EXPERTS, MOE_TILE), lambda i: (i, 0, 0)),
                   pl.BlockSpec((1, 8, LANES), lambda i: (i, 0, 0))],
        compiler_params=pltpu.CompilerParams(dimension_semantics=("arbitrary",), vmem_limit_bytes=VMEM_LIMIT),
        name="route_plan",
    )(gates)


def _moe_kernel(count_ref, x_ref, h_ref, gate_ref, rank_ref, rank_rows_ref, mod_ref, wg_ref, wu_ref, wd_ref, gf_ref,
                o_ref, *, tiles_per_batch):
    i = pl.program_id(0)
    e = pl.program_id(1)
    rows = h_ref.shape[0]
    d = h_ref.shape[1]
    lane = lax.broadcasted_iota(jnp.int32, (rows, LANES), 1)
    ge = jnp.sum(jnp.where(lane == e, gate_ref[...], 0.0), axis=-1, keepdims=True)
    rank_lanes = jnp.broadcast_to(jnp.sum(jnp.where(lane == e, rank_ref[...], 0.0), axis=-1, keepdims=True),
                                  (rows, MOE_SLOTS))
    rank_row = rank_rows_ref[0, pl.ds(e, 1), :]

    @pl.when(e == 0)
    def _():
        o_ref[...] = jnp.zeros(o_ref.shape, F32)

    slot_sub = lax.broadcasted_iota(jnp.int32, (MOE_BLOCK, rows), 0)
    slot_lane = lax.broadcasted_iota(jnp.int32, (1, MOE_SLOTS), 1)

    def block(j, carry):
        lo = j * MOE_BLOCK
        pick = jnp.where(rank_row == (slot_sub + lo).astype(F32), 1.0, 0.0).astype(BF16)
        hg = jnp.dot(pick, h_ref[...], preferred_element_type=F32).astype(BF16)
        gate = jnp.dot(hg, wg_ref[0], preferred_element_type=F32)
        up = jnp.dot(hg, wu_ref[0], preferred_element_type=F32)
        y = jnp.dot((_silu(gate) * up).astype(BF16), wd_ref[0], preferred_element_type=F32).astype(BF16)
        y = jnp.concatenate([y, jnp.zeros((MOE_SLOTS - MOE_BLOCK, d), BF16)], axis=0)
        target = jnp.where(slot_lane < MOE_BLOCK, slot_lane + lo, -2).astype(F32)
        place = jnp.where(rank_lanes == target, 1.0, 0.0).astype(BF16)
        o_ref[...] += ge * jnp.dot(place, y, preferred_element_type=F32)
        return carry

    n_blocks = (count_ref[i * N_EXPERTS + e] + (MOE_BLOCK - 1)) // MOE_BLOCK
    lax.fori_loop(0, n_blocks, block, 0)

    @pl.when(e == N_EXPERTS - 1)
    def _():
        gate5 = mod_ref[pl.ds(i // tiles_per_batch, 1), 5 * D_MODEL:6 * D_MODEL]
        o_ref[...] = _rms(x_ref[...] + gate5 * o_ref[...]) * gf_ref[...]


def _moe_final(x, h, gates, mod, wg, wu, wd, g_final):
    bsz, n, d = x.shape
    assert n % MOE_TILE == 0
    n_tok = bsz * n
    gates = gates.reshape(n_tok, LANES)
    rank, rank_rows, counts = _route_plan(gates)
    counts = counts[:, 0, :N_EXPERTS].reshape(-1)
    kern = functools.partial(_moe_kernel, tiles_per_batch=n // MOE_TILE)
    const = lambda i, e, bnd: (0, 0)
    tok = lambda w: pl.BlockSpec((MOE_TILE, w), lambda i, e, bnd: (i, 0))
    per_expert = lambda a, b: pl.BlockSpec((1, a, b), lambda i, e, bnd: (e, 0, 0))
    out = pl.pallas_call(
        kern,
        out_shape=jax.ShapeDtypeStruct((n_tok, d), F32),
        grid_spec=pltpu.PrefetchScalarGridSpec(
            num_scalar_prefetch=1,
            grid=(n_tok // MOE_TILE, N_EXPERTS),
            in_specs=[tok(d), tok(d), tok(LANES), tok(LANES),
                      pl.BlockSpec((1, N_EXPERTS, MOE_TILE), lambda i, e, bnd: (i, 0, 0)),
                      pl.BlockSpec(mod.shape, const),
                      per_expert(d, D_FF_EXPERT), per_expert(d, D_FF_EXPERT), per_expert(D_FF_EXPERT, d),
                      pl.BlockSpec(g_final.shape, const)],
            out_specs=tok(d)),
        compiler_params=pltpu.CompilerParams(dimension_semantics=("arbitrary", "arbitrary"),
                                             vmem_limit_bytes=VMEM_LIMIT),
        name="moe_final",
    )(counts, x.reshape(n_tok, d), h.reshape(n_tok, d), gates, rank, rank_rows, mod, wg, wu, wd, g_final)
    return out.reshape(bsz, n, d)


def _dt_lanes(v):
    gap = jnp.zeros(v.shape[:-1] + (DT_REP - v.shape[-1],), v.dtype)
    tail = jnp.zeros(v.shape[:-1] + (LANES - 3 * DT_REP,), v.dtype)
    return jnp.concatenate([v, gap, v, gap, v, gap, tail], axis=-1)


def _head_slots(w, per_head, take0, take1, slot=HEAD_SLOT):
    k = w.shape[0]
    w = w.reshape(k, MLA_HEADS, per_head)[:, :, take0:take1]
    w = jnp.pad(w, ((0, 0), (0, 0), (0, slot - (take1 - take0))))
    return w.reshape(k, MLA_HEADS * slot).astype(BF16)


def _q_slots(w):
    k = w.shape[0]
    w = w.reshape(k, MLA_HEADS, MLA_NOPE + MLA_ROPE)
    r = w[:, :, MLA_NOPE:]
    partner = jnp.concatenate([r[:, :, c0:c0 + ROPE_HALF] for c0 in ROPE_PARTNER_STARTS], axis=2)
    return jnp.concatenate([w, partner], axis=2).reshape(k, MLA_HEADS * HEAD_SLOT).astype(BF16)


def _rope_tables(n_latent, ctx_len):
    t = jnp.arange(n_latent)
    pos = jnp.stack([(t // GRID_W).astype(F32), (t % GRID_W).astype(F32)], axis=1)
    n_freq = MLA_ROPE // 4
    inv_freq = ROPE_THETA ** (-jnp.arange(n_freq, dtype=F32) / n_freq)
    ang = pos[:, :, None] * inv_freq
    cos, sin = jnp.cos(ang), jnp.sin(ang)
    cos_r = jnp.stack([cos, cos], axis=2).reshape(n_latent, MLA_ROPE)
    sin_r = jnp.stack([-sin, sin], axis=2).reshape(n_latent, MLA_ROPE)

    def slot(r, fill):
        rope = jnp.concatenate([jnp.full((ctx_len, MLA_ROPE), fill, F32), r], axis=0)
        rows = ctx_len + n_latent
        return jnp.concatenate([jnp.full((rows, ROPE_LANE0), fill, F32), rope, jnp.zeros((rows, MLA_ROPE), F32)],
                               axis=1)

    return slot(cos_r, 1.0), slot(sin_r, 0.0)


def kernel(x, c, ctx, c_ctx, w_mod, b_mod, g_mix, w_in, w_sgu, b_sgu, g_sgu, beta_sgu, conv_w, conv_b, dt_bias,
           a_log, d_skip, g_ssd, g_q, w_uq, g_kv, w_ukv, w_out, g_ffn, w_gate, w_up, w_down, w_router, w_gate_e,
           w_up_e, w_down_e, g_final):
    bsz, n, d = x.shape
    ctx_len = ctx.shape[1]
    depth = w_in.shape[0]
    assert bsz <= CTX_MOD_ROW and ctx_len % CHUNK == 0 and n % CHUNK == 0

    cond = jnp.concatenate([c, jnp.zeros((CTX_MOD_ROW - bsz, d), F32), c_ctx[None],
                            jnp.zeros((MOD_ROWS - CTX_MOD_ROW - 1, d), F32)], axis=0)
    mod_all = _mod_table(cond, w_mod, b_mod)
    cos, sin = _rope_tables(n, ctx_len)
    xa = (ctx, x)
    t_all = ctx_len + n
    rows_in = 768 if t_all % 768 == 0 else ctx_len

    out = None
    for layer in range(depth):
        last = layer == depth - 1
        mod = mod_all[layer]
        row = lambda v: v.reshape(1, -1)
        p_sgu, p_z, p_xbc, p_dt, p_q, p_kv, p_kr = _in_proj(
            xa, t_all, mod, row(g_mix[layer]), w_in, layer, ctx_len, rows_in)
        w_cat = jnp.transpose(w_sgu[layer], (1, 0, 2)).reshape(CHUNK, SGU_HEADS * CHUNK).astype(BF16)
        bias = jnp.repeat(b_sgu[layer].T, SGU_HEAD_DIM, axis=1)
        o_sgu = _sgu(p_sgu, w_cat, bias, row(g_sgu[layer]), row(beta_sgu[layer]), rows_in)
        o_ssd = _ssd(p_z, p_xbc, p_dt, conv_w[layer], row(conv_b[layer]), _dt_lanes(dt_bias[layer].reshape(1, -1)),
                     _dt_lanes(a_log[layer].reshape(1, -1)), row(jnp.repeat(d_skip[layer], SSD_HEAD_DIM)), row(g_ssd[layer]),
                     ctx_len)
        per_kv = MLA_NOPE + MLA_V
        q, k, v = _mla_proj(p_q, p_kv, p_kr, row(g_q[layer]), _q_slots(w_uq[layer]),
                            row(g_kv[layer]), _head_slots(w_ukv[layer], per_kv, 0, MLA_NOPE),
                            _head_slots(w_ukv[layer], per_kv, MLA_NOPE, per_kv, MLA_V).T, cos, sin, rows_in)
        o_mla = _attention(q, k, v, ctx_len, latent=True)
        o_mla_ctx = None if last else _attention(q, k, v, ctx_len, latent=False)
        wo = w_out[layer].astype(BF16)
        i = layer // 2
        if layer % 2 == 0:
            if last:
                raise NotImplementedError("final dense channel mixer")
            xa = _mix_ffn(xa, o_sgu, o_ssd, o_mla, o_mla_ctx, mod, wo, row(g_ffn[layer]),
                          w_gate[i].astype(BF16), w_up[i].astype(BF16), w_down[i].astype(BF16), ctx_len)
        else:
            if not last:
                raise NotImplementedError("expert channel mixer on a non-final layer")
            wr_hi = w_router[i].astype(BF16)
            wr_lo = (w_router[i] - wr_hi.astype(F32)).astype(BF16)
            pad = lambda w: jnp.pad(w, ((0, 0), (0, LANES - N_EXPERTS)))
            wr = jnp.concatenate([pad(wr_hi), pad(wr_lo)], axis=1)
            x_mid, h_ffn, gates = _out_proj_route(xa, o_sgu, o_ssd, o_mla, mod, wo, row(g_ffn[layer]), wr,
                                                  ctx_len)
            out = _moe_final(x_mid, h_ffn, gates, mod, w_gate_e[i].astype(BF16), w_up_e[i].astype(BF16),
                             w_down_e[i].astype(BF16), row(g_final))
    return out
```

```python
import functools

import jax
import jax.numpy as jnp
from jax import lax
from jax.experimental import pallas as pl
from jax.experimental.pallas import tpu as pltpu

F32 = jnp.float32
BF16 = jnp.bfloat16

D_MODEL = 1024
EPS = 1e-6
N_MOD = 6
GRID_W = 64
CHUNK = 128

SGU_HEADS = 4
SGU_HEAD_DIM = 64
SGU_WIDTH = 256

SSD_HEADS = 6
SSD_HEAD_DIM = 64
SSD_WIDTH = 384
SSD_GROUPS = 2
SSD_HPG = 3
SSD_STATE = 128
SSD_CONV = 5
SSD_CONV_CH = 896
SSD_GW = SSD_HPG * SSD_HEAD_DIM
DT_REP = 16

MLA_HEADS = 6
MLA_NOPE = 64
MLA_ROPE = 32
MLA_V = 64
MLA_WIDTH = 384
Q_LORA = 384
KV_LORA = 256
ROPE_THETA = 10000.0
MLA_SCALE = (MLA_NOPE + MLA_ROPE) ** -0.5
LOG2_E = 1.4426950408889634
HEAD_SLOT = 128
ROPE_LANE0 = MLA_NOPE
ROPE_HALF = MLA_ROPE // 4
ROPE_PARTNER_STARTS = (ROPE_HALF, 0, 3 * ROPE_HALF, 2 * ROPE_HALF)
assert HEAD_SLOT == ROPE_LANE0 + 2 * MLA_ROPE

D_FF = 2816
N_EXPERTS = 8
D_FF_EXPERT = 1408

LANES = 128
MOD_ROWS = 16
CTX_MOD_ROW = 8

IN_COLS = (("sgu", 0, 512), ("z", 512, 896), ("xbc", 896, 1792), ("dt", 1792, 1920), ("cq", 1920, 2304),
           ("ckv", 2304, 2560), ("kr", 2560, 2688))
IN_PAD_WIDTH = 2688

VMEM_LIMIT = 56 * 1024 * 1024


def _sigmoid(x):
    return 1.0 / (1.0 + jnp.exp(-x))


def _silu(x):
    return x * _sigmoid(x)


def _rms(x):
    return x * lax.rsqrt(jnp.mean(x * x, axis=-1, keepdims=True) + EPS)


def _aligned(v, m):
    return v if isinstance(v, int) else pl.multiple_of(v, m)


def _row_select(mod_ref, b, tile, rows, ctx_len, col0):
    row = tile * rows + lax.broadcasted_iota(jnp.int32, (rows, 1), 0)
    is_ctx = row < ctx_len
    mb = mod_ref[pl.ds(b, 1), col0:col0 + D_MODEL]
    mc = mod_ref[CTX_MOD_ROW:CTX_MOD_ROW + 1, col0:col0 + D_MODEL]
    return jnp.where(is_ctx, mc, mb)


def _mod_kernel(cond_ref, w_ref, b_ref, o_ref):
    s = _silu(cond_ref[...])
    s_hi = s.astype(BF16)
    s_lo = (s - s_hi.astype(F32)).astype(BF16)
    w = w_ref[0]
    w_hi = w.astype(BF16)
    w_lo = (w - w_hi.astype(F32)).astype(BF16)
    by_hi = jnp.dot(jnp.concatenate([s_hi, s_lo], axis=0), w_hi, preferred_element_type=F32)
    o_ref[0] = (by_hi[:MOD_ROWS] + by_hi[MOD_ROWS:] + jnp.dot(s_hi, w_lo, preferred_element_type=F32)
                + b_ref[0])


def _mod_table(cond, w_mod, b_mod):
    n_layers, d, width = w_mod.shape
    cb = 1536
    return pl.pallas_call(
        _mod_kernel,
        out_shape=jax.ShapeDtypeStruct((n_layers, MOD_ROWS, width), F32),
        grid=(n_layers, width // cb),
        in_specs=[pl.BlockSpec((MOD_ROWS, d), lambda l, j: (0, 0)),
                  pl.BlockSpec((1, d, cb), lambda l, j: (l, 0, j)),
                  pl.BlockSpec((1, 1, cb), lambda l, j: (l, 0, j))],
        out_specs=pl.BlockSpec((1, MOD_ROWS, cb), lambda l, j: (l, 0, j)),
        compiler_params=pltpu.CompilerParams(dimension_semantics=("arbitrary", "arbitrary"),
                                             vmem_limit_bytes=VMEM_LIMIT),
        name="mod_table",
    )(cond, w_mod, b_mod.reshape(n_layers, 1, width))


def _token_specs(tokens, ctx_len, rows, d):
    if not isinstance(tokens, tuple):
        return [pl.BlockSpec((1, rows, d), lambda b, t: (b, t, 0))], [tokens]
    ctx, lat = tokens
    n_sub = rows // ctx_len
    assert rows == n_sub * ctx_len
    piece = lambda j: pl.BlockSpec((1, ctx_len, d), lambda b, t: (b, jnp.maximum(n_sub * t + j - 1, 0), 0))
    return ([pl.BlockSpec((1, ctx_len, d), lambda b, t: (b, 0, 0))] + [piece(j) for j in range(n_sub)],
            [ctx] + [lat] * n_sub)


def _tokens_tile(refs, t):
    if len(refs) == 1:
        return refs[0][0]
    first = jnp.where(t == 0, refs[0][0], refs[1][0])
    return first if len(refs) == 2 else jnp.concatenate([first] + [r[0] for r in refs[2:]], axis=0)


IN_RAW_DT = (1792, 1804)
IN_RAW_KR = IN_RAW_DT[1] + Q_LORA + KV_LORA
IN_RAW_WIDTH = IN_RAW_KR + MLA_ROPE


def _pad_in_weight(wraw_ref, w_ref):
    dt0, dt1 = IN_RAW_DT
    for r0 in range(0, D_MODEL, LANES):
        rows = slice(r0, r0 + LANES)
        z = lambda n: jnp.zeros((LANES, n), F32)
        dt = wraw_ref[0, rows, dt0:dt1]
        gap = z(DT_REP - (dt1 - dt0))
        w_ref[rows, :] = jnp.concatenate(
            [wraw_ref[0, rows, 0:dt0], dt, gap, dt, gap, dt, gap, z(LANES - 3 * DT_REP),
             wraw_ref[0, rows, dt1:IN_RAW_KR], z(ROPE_LANE0), wraw_ref[0, rows, IN_RAW_KR:IN_RAW_WIDTH]]
            + [wraw_ref[0, rows, IN_RAW_KR + c0:IN_RAW_KR + c0 + ROPE_HALF] for c0 in ROPE_PARTNER_STARTS],
            axis=1).astype(BF16)


def _sgu_chunks(p_sgu, w_ref, bias_ref, g_ref, beta_ref, o_ref):
    lane = lax.broadcasted_iota(jnp.int32, (1, SGU_WIDTH), 1)
    head_of_lane = lane // SGU_HEAD_DIM
    w = w_ref[...]
    bias = bias_ref[...]
    c0 = 0.7978845608028654
    for c in range(p_sgu.shape[0] // CHUNK):
        p = p_sgu[c * CHUNK:(c + 1) * CHUNK, :]
        ge = 0.5 * p * (1.0 + jnp.tanh(c0 * (p + 0.044715 * (p * p * p))))
        u = ge[:, :SGU_WIDTH]
        v = ge[:, SGU_WIDTH:]
        mu = jnp.mean(v, axis=-1, keepdims=True)
        vc = v - mu
        vn = vc * lax.rsqrt(jnp.mean(vc * vc, axis=-1, keepdims=True) + EPS) * g_ref[...] + beta_ref[...]
        stacked = jnp.concatenate(
            [jnp.where(head_of_lane == h, vn, 0.0).astype(BF16) for h in range(SGU_HEADS)], axis=0)
        mixed = jnp.dot(w, stacked, preferred_element_type=F32) + bias
        o_ref[0, c * CHUNK:(c + 1) * CHUNK, :] = (u * mixed).astype(BF16)


def _in_kernel(*refs, rows, ctx_len, n_tok_refs):
    tok_refs = refs[:n_tok_refs]
    mod_ref, g_ref, wraw_ref, sgu_w_ref, sgu_bias_ref, sgu_g_ref, sgu_beta_ref, *out_refs, w_ref = refs[n_tok_refs:]
    b = pl.program_id(0)
    t = pl.program_id(1)

    @pl.when(jnp.logical_and(b == 0, t == 0))
    def _():
        _pad_in_weight(wraw_ref, w_ref)

    xn = _rms(_tokens_tile(tok_refs, t)) * g_ref[...]
    shift = _row_select(mod_ref, b, t, rows, ctx_len, 0)
    scale = _row_select(mod_ref, b, t, rows, ctx_len, D_MODEL)
    h = (xn * (1.0 + scale) + shift).astype(BF16)
    p = jnp.dot(h, w_ref[...], preferred_element_type=F32)
    (_, s0, s1), other_cols = IN_COLS[0], IN_COLS[1:]
    _sgu_chunks(p[:, s0:s1], sgu_w_ref, sgu_bias_ref, sgu_g_ref, sgu_beta_ref, out_refs[0])
    for o_ref, (_, c0, c1) in zip(out_refs[1:], other_cols):
        o_ref[0] = p[:, c0:c1]


def _in_proj(tokens, t_all, mod, g, w_in, layer, sgu_w, sgu_bias, sgu_g, sgu_beta, ctx_len, rows):
    tok_specs, tok_args = _token_specs(tokens, ctx_len, rows, D_MODEL)
    bsz = tok_args[0].shape[0]
    assert w_in.shape[1:] == (D_MODEL, IN_RAW_WIDTH) and rows % CHUNK == 0
    kern = functools.partial(_in_kernel, rows=rows, ctx_len=ctx_len, n_tok_refs=len(tok_args))
    const = lambda b, t: (0, 0)
    widths = [(SGU_WIDTH, BF16)] + [(c1 - c0, F32) for _, c0, c1 in IN_COLS[1:]]
    return pl.pallas_call(
        kern,
        out_shape=[jax.ShapeDtypeStruct((bsz, t_all, w), dt) for w, dt in widths],
        grid=(bsz, t_all // rows),
        in_specs=tok_specs + [pl.BlockSpec(mod.shape, const), pl.BlockSpec((1, D_MODEL), const),
                              pl.BlockSpec((1,) + w_in.shape[1:], lambda b, t: (layer, 0, 0)),
                              pl.BlockSpec(sgu_w.shape, const), pl.BlockSpec(sgu_bias.shape, const),
                              pl.BlockSpec(sgu_g.shape, const), pl.BlockSpec(sgu_beta.shape, const)],
        out_specs=[pl.BlockSpec((1, rows, w), lambda b, t: (b, t, 0)) for w, _ in widths],
        scratch_shapes=[pltpu.VMEM((D_MODEL, IN_PAD_WIDTH), BF16)],
        compiler_params=pltpu.CompilerParams(dimension_semantics=("arbitrary", "arbitrary"),
                                             vmem_limit_bytes=VMEM_LIMIT),
        name="in_proj",
    )(*tok_args, mod, g, w_in, sgu_w, sgu_bias, sgu_g, sgu_beta)


def _ssd_kernel(z_ref, xbc_ref, dt_ref, cw_ref, cb_ref, dtb_ref, alog_ref, skip_ref, g_ref, o_ref,
                xc_ref, yf_ref, yb_ref, st_ref, *, n_blk, n_ctx_blk):
    def conv_block(blk, seg_start, seg_end):
        r0 = _aligned(blk * CHUNK, CHUNK)
        for cg in range(SSD_CONV_CH // LANES):
            cols = slice(cg * LANES, (cg + 1) * LANES)
            zeros = jnp.zeros((8, LANES), F32)
            top = zeros if seg_start else xbc_ref[0, pl.ds(_aligned(r0 - 8, 8), 8), cols]
            bot = zeros if seg_end else xbc_ref[0, pl.ds(_aligned(r0 + CHUNK, 8), 8), cols]
            xw = jnp.concatenate([top, xbc_ref[0, pl.ds(r0, CHUNK), cols], bot], axis=0)
            acc = cb_ref[:, cols] + cw_ref[0:1, cols] * xw[6:6 + CHUNK]
            for k in range(1, SSD_CONV):
                acc = acc + cw_ref[k:k + 1, cols] * xw[6 + k:6 + k + CHUNK]
            xc_ref[pl.ds(r0, CHUNK), cols] = _silu(acc)

    static_blocks = sorted(set(list(range(n_ctx_blk)) + [n_ctx_blk, n_blk - 1]))
    for blk in static_blocks:
        conv_block(blk, blk == 0 or blk == n_ctx_blk, blk == n_ctx_blk - 1 or blk == n_blk - 1)
    if n_blk - 1 > n_ctx_blk + 1:
        def conv_body(blk, carry):
            conv_block(blk, False, False)
            return carry
        lax.fori_loop(n_ctx_blk + 1, n_blk - 1, conv_body, 0)

    li = lax.broadcasted_iota(jnp.int32, (CHUNK, CHUNK), 0)
    si = lax.broadcasted_iota(jnp.int32, (CHUNK, CHUNK), 1)
    a_neg = -jnp.exp(alog_ref[...])

    def chunk_step(c, direction):
        r0 = pl.multiple_of(c * CHUNK, CHUNK)
        rows = pl.ds(r0, CHUNK)
        mask = (si <= li) if direction == 0 else (si >= li)
        tri = jnp.where(mask, 1.0, 0.0).astype(BF16)
        dtr = dt_ref[0, rows, :] + dtb_ref[...]
        dt = jnp.maximum(dtr, 0.0) + jnp.log1p(jnp.exp(-jnp.abs(dtr)))
        adt = dt * a_neg
        p1 = adt.astype(BF16)
        r1 = adt - p1.astype(F32)
        p2 = r1.astype(BF16)
        p3 = (r1 - p2.astype(F32)).astype(BF16)
        parts = jnp.dot(tri, jnp.concatenate([p1, p2, p3], axis=1), preferred_element_type=F32)
        acs = parts[:, :LANES] + parts[:, LANES:2 * LANES] + parts[:, 2 * LANES:]
        end = CHUNK - 1 if direction == 0 else 0
        tot = acs[end:end + 1, :]
        to_end_dt = jnp.exp(tot - acs) * dt
        chunk_decay = jnp.exp(tot)
        per_head_rows = jnp.where(si < DT_REP, acs, jnp.where(si < 2 * DT_REP, dt, to_end_dt)).T
        first_half = si < SSD_HEAD_DIM
        cbs, bm_ts, y_offs = [], [], []
        state = st_ref[direction]
        state16 = state.astype(BF16)
        for g in range(SSD_GROUPS):
            bm = xc_ref[rows, SSD_WIDTH + g * SSD_STATE:SSD_WIDTH + (g + 1) * SSD_STATE]
            cm = xc_ref[rows, SSD_WIDTH + SSD_GROUPS * SSD_STATE + g * SSD_STATE:
                        SSD_WIDTH + SSD_GROUPS * SSD_STATE + (g + 1) * SSD_STATE]
            cm16 = cm.astype(BF16)
            cbs.append(lax.dot_general(cm16, bm.astype(BF16), (((1,), (1,)), ((), ())),
                                       preferred_element_type=F32))
            bm_ts.append(bm.T)
            y_offs.append(jnp.dot(cm16, state16, preferred_element_type=F32))
        y_pairs, state_pairs = [], []
        for k in range(SSD_HEADS // 2):
            pair = slice(k * LANES, (k + 1) * LANES)
            xs16 = xc_ref[rows, pair].astype(BF16)
            y_diag, contrib, exp_a, dec, y_off = [], [], [], [], []
            for hh in (2 * k, 2 * k + 1):
                g = hh // SSD_HPG
                col = direction * SSD_HEADS + hh
                a_col = jnp.broadcast_to(acs[:, col:col + 1], (CHUNK, CHUNK))
                seg = a_col - per_head_rows[col:col + 1, :]
                decay = jnp.exp(jnp.where(mask, seg, -1e30))
                m = (cbs[g] * decay * per_head_rows[DT_REP + col:DT_REP + col + 1, :]).astype(BF16)
                y_diag.append(jnp.dot(m, xs16, preferred_element_type=F32))
                exp_a.append(jnp.exp(a_col))
                y_off.append(y_offs[g][:, pair])
                lhs = (bm_ts[g] * per_head_rows[2 * DT_REP + col:2 * DT_REP + col + 1, :]).astype(BF16)
                contrib.append(jnp.dot(lhs, xs16, preferred_element_type=F32))
                dec.append(jnp.broadcast_to(chunk_decay[:, col:col + 1], (SSD_STATE, LANES)))
            pick = lambda ab: jnp.where(first_half, ab[0], ab[1])
            y_pairs.append(pick(y_diag) + pick(exp_a) * pick(y_off))
            state_pairs.append(pick(dec) * state[:, pair] + pick(contrib))
        st_ref[direction] = jnp.concatenate(state_pairs, axis=1)
        y = jnp.concatenate(y_pairs, axis=1)
        if direction == 0:
            yf_ref[rows, :] = skip_ref[...] * xc_ref[rows, 0:SSD_WIDTH] + y
        else:
            yb_ref[rows, :] = y

    def scan_body(step, carry):
        chunk_step(step, 0)
        chunk_step(jnp.where(step < n_ctx_blk, n_ctx_blk - 1 - step, n_blk - 1 - (step - n_ctx_blk)), 1)
        return carry

    st_ref[...] = jnp.zeros(st_ref.shape, F32)
    lax.fori_loop(0, n_blk, scan_body, 0)

    def out_body(c, carry):
        rows = pl.ds(pl.multiple_of(c * CHUNK, CHUNK), CHUNK)
        gated = (yf_ref[rows, :] + yb_ref[rows, :]) * _silu(z_ref[0, rows, :])
        o_ref[0, rows, :] = (_rms(gated) * g_ref[...]).astype(BF16)
        return carry

    lax.fori_loop(0, n_blk, out_body, 0)


def _ssd(p_z, p_xbc, p_dt, conv_w, conv_b, dt_bias, a_log, skip, g, ctx_len):
    bsz, t_all, _ = p_z.shape
    kern = functools.partial(_ssd_kernel, n_blk=t_all // CHUNK, n_ctx_blk=ctx_len // CHUNK)
    const = lambda b: (0, 0)
    per_b = lambda w: pl.BlockSpec((1, t_all, w), lambda b: (b, 0, 0))
    return pl.pallas_call(
        kern,
        out_shape=jax.ShapeDtypeStruct((bsz, t_all, SSD_WIDTH), BF16),
        grid=(bsz,),
        in_specs=[per_b(SSD_WIDTH), per_b(SSD_CONV_CH), per_b(LANES),
                  pl.BlockSpec(conv_w.shape, const), pl.BlockSpec(conv_b.shape, const),
                  pl.BlockSpec(dt_bias.shape, const), pl.BlockSpec(a_log.shape, const),
                  pl.BlockSpec(skip.shape, const), pl.BlockSpec(g.shape, const)],
        out_specs=per_b(SSD_WIDTH),
        scratch_shapes=[pltpu.VMEM((t_all, SSD_CONV_CH), F32), pltpu.VMEM((t_all, SSD_WIDTH), F32),
                        pltpu.VMEM((t_all, SSD_WIDTH), F32),
                        pltpu.VMEM((2, SSD_STATE, SSD_WIDTH), F32)],
        compiler_params=pltpu.CompilerParams(dimension_semantics=("arbitrary",), vmem_limit_bytes=VMEM_LIMIT),
        name="ssd",
    )(p_z, p_xbc, p_dt, conv_w, conv_b, dt_bias, a_log, skip, g)


def _mla_proj_kernel(pq_ref, pkv_ref, pkr_ref, gq_ref, wq_ref, gkv_ref, wk_ref, wv_ref, cos_ref, sin_ref,
                     q_ref, k_ref, v_ref):
    cos = cos_ref[...]
    sin = sin_ref[...]

    def rope(t):
        return t * cos + pltpu.roll(t, LANES - MLA_ROPE, 1) * sin

    qn = (_rms(pq_ref[0]) * gq_ref[...]).astype(BF16)
    q = jnp.dot(qn, wq_ref[...], preferred_element_type=F32)
    kvn = (_rms(pkv_ref[0]) * gkv_ref[...]).astype(BF16)
    k = jnp.dot(kvn, wk_ref[...], preferred_element_type=F32)
    v_t = lax.dot_general(wv_ref[...], kvn, (((1,), (1,)), ((), ())),
                          preferred_element_type=F32).astype(BF16)
    kr = rope(pkr_ref[0])
    for h in range(MLA_HEADS):
        slot = slice(h * HEAD_SLOT, (h + 1) * HEAD_SLOT)
        q_ref[0, h] = (rope(q[:, slot]) * (MLA_SCALE * LOG2_E)).astype(BF16)
        k_ref[0, h] = (k[:, slot] + kr).astype(BF16)
        v_ref[0, h] = v_t[h * MLA_V:(h + 1) * MLA_V, :]


def _mla_proj(p_q, p_kv, p_kr, g_q, w_q, g_kv, w_k, w_v, cos, sin, rows):
    bsz, t_all, _ = p_q.shape
    const = lambda b, t: (0, 0)
    tok = lambda w: pl.BlockSpec((1, rows, w), lambda b, t: (b, t, 0))
    tab = pl.BlockSpec((rows, LANES), lambda b, t: (t, 0))
    out = jax.ShapeDtypeStruct((bsz, MLA_HEADS, t_all, HEAD_SLOT), BF16)
    out_vt = jax.ShapeDtypeStruct((bsz, MLA_HEADS, MLA_V, t_all), BF16)
    head_tok = pl.BlockSpec((1, MLA_HEADS, rows, HEAD_SLOT), lambda b, t: (b, 0, t, 0))
    return pl.pallas_call(
        _mla_proj_kernel,
        out_shape=[out, out, out_vt],
        grid=(bsz, t_all // rows),
        in_specs=[tok(Q_LORA), tok(KV_LORA), tok(LANES),
                  pl.BlockSpec(g_q.shape, const), pl.BlockSpec(w_q.shape, const),
                  pl.BlockSpec(g_kv.shape, const), pl.BlockSpec(w_k.shape, const), pl.BlockSpec(w_v.shape, const),
                  tab, tab],
        out_specs=[head_tok, head_tok, pl.BlockSpec((1, MLA_HEADS, MLA_V, rows), lambda b, t: (b, 0, 0, t))],
        compiler_params=pltpu.CompilerParams(dimension_semantics=("arbitrary", "arbitrary"),
                                             vmem_limit_bytes=VMEM_LIMIT),
        name="mla_proj",
    )(p_q, p_kv, p_kr, g_q, w_q, g_kv, w_k, w_v, cos, sin)


ATTN_Q = 512
ATTN_BUFS = 2


def _attn_kernel(*refs, n_q_blocks):
    q_refs = refs[:n_q_blocks]
    k_ref, vt_ref, o_ref, *bufs, ot_ref = refs[n_q_blocks:]
    s_bufs, m_bufs = bufs[:ATTN_BUFS], bufs[ATTN_BUFS:]
    def scores(h, s_ref, m_ref):
        qh = jnp.concatenate([r[0, h] for r in q_refs], axis=0)
        s_t = lax.dot_general(k_ref[0, h], qh, (((1,), (1,)), ((), ())), preferred_element_type=F32)
        s_ref[...] = s_t
        m_ref[...] = jnp.broadcast_to(jnp.max(s_t, axis=0, keepdims=True), m_ref.shape)

    def weigh(h, s_ref, m_ref):
        p_t = jnp.exp2(s_ref[...] - m_ref[0:1, :])
        denom = jnp.sum(p_t, axis=0, keepdims=True)
        o_t = jnp.dot(vt_ref[0, h], p_t.astype(BF16), preferred_element_type=F32)
        ot_ref[pl.ds(pl.multiple_of(h * MLA_V, MLA_V), MLA_V), :] = o_t / denom

    scores(0, s_bufs[0], m_bufs[0])

    def head_group(j, carry):
        h0 = ATTN_BUFS * j
        for i in range(ATTN_BUFS):
            nxt = (i + 1) % ATTN_BUFS
            scores(jnp.minimum(h0 + i + 1, MLA_HEADS - 1), s_bufs[nxt], m_bufs[nxt])
            weigh(h0 + i, s_bufs[i], m_bufs[i])
        return carry

    lax.fori_loop(0, MLA_HEADS // ATTN_BUFS, head_group, 0)
    o_ref[0] = ot_ref[...].T.astype(BF16)


def _attention(q, k, vt, ctx_len, latent):
    bsz, n_heads, t_all, width = q.shape
    if latent:
        n_q_blocks = ATTN_Q // ctx_len
        n_rows, n_keys, tq = t_all - ctx_len, t_all, ATTN_Q
        q_specs = [pl.BlockSpec((1, n_heads, ctx_len, width),
                                functools.partial(lambda b, i, j: (b, 0, n_q_blocks * i + 1 + j, 0), j=j))
                   for j in range(n_q_blocks)]
    else:
        n_q_blocks = 1
        n_rows, n_keys, tq = ctx_len, ctx_len, ctx_len
        q_specs = [pl.BlockSpec((1, n_heads, ctx_len, width), lambda b, i: (b, 0, 0, 0))]
    assert n_rows % tq == 0
    return pl.pallas_call(
        functools.partial(_attn_kernel, n_q_blocks=n_q_blocks),
        out_shape=jax.ShapeDtypeStruct((bsz, n_rows, MLA_WIDTH), BF16),
        grid=(bsz, n_rows // tq),
        in_specs=q_specs + [pl.BlockSpec((1, n_heads, n_keys, width), lambda b, i: (b, 0, 0, 0)),
                            pl.BlockSpec((1, n_heads, MLA_V, n_keys), lambda b, i: (b, 0, 0, 0))],
        out_specs=pl.BlockSpec((1, tq, MLA_WIDTH), lambda b, i: (b, i, 0)),
        scratch_shapes=([pltpu.VMEM((n_keys, tq), F32)] * ATTN_BUFS + [pltpu.VMEM((8, tq), F32)] * ATTN_BUFS
                        + [pltpu.VMEM((MLA_WIDTH, tq), F32)]),
        compiler_params=pltpu.CompilerParams(dimension_semantics=("arbitrary", "arbitrary"),
                                             vmem_limit_bytes=VMEM_LIMIT),
        name="attention",
    )(*([q] * n_q_blocks), k, vt)


def _mix_residual(x_in, o1_ref, o2_ref, o3, mod_ref, wo_ref, g_ref, b, t, rows, ctx_len):
    r1, r2 = SGU_WIDTH, SGU_WIDTH + SSD_WIDTH
    mix = (jnp.dot(o1_ref[0], wo_ref[0:r1, :], preferred_element_type=F32)
           + jnp.dot(o2_ref[0], wo_ref[r1:r2, :], preferred_element_type=F32)
           + jnp.dot(o3, wo_ref[r2:, :], preferred_element_type=F32))
    x = x_in + _row_select(mod_ref, b, t, rows, ctx_len, 2 * D_MODEL) * mix
    shift = _row_select(mod_ref, b, t, rows, ctx_len, 3 * D_MODEL)
    scale = _row_select(mod_ref, b, t, rows, ctx_len, 4 * D_MODEL)
    return x, _rms(x) * g_ref[...] * (1.0 + scale) + shift


def _out_kernel(x_ref, o1_ref, o2_ref, o3_ref, mod_ref, wo_ref, g_ref, wr_ref,
                xo_ref, h_ref, gate_ref, *, rows, ctx_len):
    b = pl.program_id(0)
    t = pl.program_id(1) + 1
    x, h = _mix_residual(x_ref[0], o1_ref, o2_ref, o3_ref[0], mod_ref, wo_ref, g_ref, b, t, rows, ctx_len)
    xo_ref[0] = x
    h_ref[0] = h.astype(BF16)
    h_hi = h.astype(BF16)
    h_lo = (h - h_hi.astype(F32)).astype(BF16)
    by_hi = jnp.dot(h_hi, wr_ref[...], preferred_element_type=F32)
    logits = by_hi[:, :LANES] + by_hi[:, LANES:] + jnp.dot(h_lo, wr_ref[:, :LANES], preferred_element_type=F32)
    lane = lax.broadcasted_iota(jnp.int32, logits.shape, 1)
    lane_f = lane.astype(F32)
    lg = jnp.where(lane < N_EXPERTS, logits, -jnp.inf)
    m1 = jnp.max(lg, axis=-1, keepdims=True)
    i1 = jnp.min(jnp.where(lg == m1, lane_f, float(LANES)), axis=-1, keepdims=True)
    lg2 = jnp.where(lane_f == i1, -jnp.inf, lg)
    m2 = jnp.max(lg2, axis=-1, keepdims=True)
    i2 = jnp.min(jnp.where(lg2 == m2, lane_f, float(LANES)), axis=-1, keepdims=True)
    e2 = jnp.exp(m2 - m1)
    w_top = 1.0 / (1.0 + e2)
    gate_ref[0] = jnp.where(lane_f == i1, w_top, 0.0) + jnp.where(lane_f == i2, e2 * w_top, 0.0)


def _out_proj_route(xa, o_sgu, o_ssd, o_mla, mod, wo, g, w_router, ctx_len):
    bsz, t_all, d = xa.shape
    rows = ctx_len
    n_out = t_all - ctx_len
    kern = functools.partial(_out_kernel, rows=rows, ctx_len=ctx_len)
    const = lambda b, t: (0, 0)
    tok = lambda w: pl.BlockSpec((1, rows, w), lambda b, t: (b, t + 1, 0))
    out_tok = lambda w: pl.BlockSpec((1, rows, w), lambda b, t: (b, t, 0))
    return pl.pallas_call(
        kern,
        out_shape=[jax.ShapeDtypeStruct((bsz, n_out, d), F32), jax.ShapeDtypeStruct((bsz, n_out, d), BF16),
                   jax.ShapeDtypeStruct((bsz, n_out, LANES), F32)],
        grid=(bsz, n_out // rows),
        in_specs=[tok(d), tok(SGU_WIDTH), tok(SSD_WIDTH), out_tok(MLA_WIDTH), pl.BlockSpec(mod.shape, const),
                  pl.BlockSpec(wo.shape, const), pl.BlockSpec(g.shape, const), pl.BlockSpec(w_router.shape, const)],
        out_specs=[out_tok(d), out_tok(d), out_tok(LANES)],
        compiler_params=pltpu.CompilerParams(dimension_semantics=("arbitrary", "arbitrary"),
                                             vmem_limit_bytes=VMEM_LIMIT),
        name="out_proj",
    )(xa, o_sgu, o_ssd, o_mla, mod, wo, g, w_router)


FF_SPLITS = ((0, 1536), (1536, D_FF))


def _mix_ffn_kernel(*refs, rows, ctx_len, n_tok_refs):
    tok_refs = refs[:n_tok_refs]
    (o1_ref, o2_ref, o3_ref, o3c_ref, mod_ref, wo_ref, g_ref, wg_ref, wu_ref, wd_ref,
     o_ref) = refs[n_tok_refs:]
    b = pl.program_id(0)
    t = pl.program_id(1)
    o3 = jnp.where(t == 0, o3c_ref[0], o3_ref[0])
    x, h = _mix_residual(_tokens_tile(tok_refs, t), o1_ref, o2_ref, o3, mod_ref, wo_ref, g_ref, b, t, rows, ctx_len)
    h = h.astype(BF16)
    acc = None
    for c0, c1 in FF_SPLITS:
        gate = jnp.dot(h, wg_ref[:, c0:c1], preferred_element_type=F32)
        up = jnp.dot(h, wu_ref[:, c0:c1], preferred_element_type=F32)
        part = jnp.dot((_silu(gate) * up).astype(BF16), wd_ref[c0:c1, :], preferred_element_type=F32)
        acc = part if acc is None else acc + part
    o_ref[0] = x + _row_select(mod_ref, b, t, rows, ctx_len, 5 * D_MODEL) * acc


def _mix_ffn(tokens, o_sgu, o_ssd, o_mla, o_mla_ctx, mod, wo, g, wg, wu, wd, ctx_len):
    bsz, t_all, _ = o_sgu.shape
    d = D_MODEL
    rows = ctx_len
    tok_specs, tok_args = _token_specs(tokens, ctx_len, rows, d)
    kern = functools.partial(_mix_ffn_kernel, rows=rows, ctx_len=ctx_len, n_tok_refs=len(tok_args))
    const = lambda b, t: (0, 0)
    tok = lambda w: pl.BlockSpec((1, rows, w), lambda b, t: (b, t, 0))
    whole = lambda a: pl.BlockSpec(a.shape, const)
    return pl.pallas_call(
        kern,
        out_shape=jax.ShapeDtypeStruct((bsz, t_all, d), F32),
        grid=(bsz, t_all // rows),
        in_specs=tok_specs + [tok(SGU_WIDTH), tok(SSD_WIDTH),
                              pl.BlockSpec((1, rows, MLA_WIDTH), lambda b, t: (b, jnp.maximum(t - 1, 0), 0)),
                              pl.BlockSpec((1, rows, MLA_WIDTH), lambda b, t: (b, 0, 0)),
                              whole(mod), whole(wo), whole(g), whole(wg), whole(wu), whole(wd)],
        out_specs=tok(d),
        compiler_params=pltpu.CompilerParams(dimension_semantics=("arbitrary", "arbitrary"),
                                             vmem_limit_bytes=VMEM_LIMIT),
        name="mix_ffn",
    )(*tok_args, o_sgu, o_ssd, o_mla, o_mla_ctx, mod, wo, g, wg, wu, wd)


MOE_TILE = 1024
MOE_BLOCK = 144
MOE_SLOTS = 256


def _plan_kernel(gate_ref, rank_ref, rank_rows_ref, count_ref):
    rows = gate_ref.shape[0]
    routed = gate_ref[...] > 0.0
    ti = lax.broadcasted_iota(jnp.int32, (rows, rows), 0)
    tj = lax.broadcasted_iota(jnp.int32, (rows, rows), 1)
    earlier = jnp.where(tj < ti, 1.0, 0.0).astype(BF16)
    ones = jnp.where(routed, 1.0, 0.0)
    before = jnp.dot(earlier, ones.astype(BF16), preferred_element_type=F32)
    rank = jnp.where(routed, before, -1.0)
    rank_ref[...] = rank
    rank_rows_ref[0] = rank.T[0:N_EXPERTS, :]
    count_ref[0] = jnp.broadcast_to(jnp.sum(ones, axis=0, keepdims=True), (8, LANES)).astype(jnp.int32)


def _route_plan(gates):
    n_tok = gates.shape[0]
    n_tiles = n_tok // MOE_TILE
    return pl.pallas_call(
        _plan_kernel,
        out_shape=[jax.ShapeDtypeStruct((n_tok, LANES), F32),
                   jax.ShapeDtypeStruct((n_tiles, N_EXPERTS, MOE_TILE), F32),
                   jax.ShapeDtypeStruct((n_tiles, 8, LANES), jnp.int32)],
        grid=(n_tiles,),
        in_specs=[pl.BlockSpec((MOE_TILE, LANES), lambda i: (i, 0))],
        out_specs=[pl.BlockSpec((MOE_TILE, LANES), lambda i: (i, 0)),
                   pl.BlockSpec((1, N_EXPERTS, MOE_TILE), lambda i: (i, 0, 0)),
                   pl.BlockSpec((1, 8, LANES), lambda i: (i, 0, 0))],
        compiler_params=pltpu.CompilerParams(dimension_semantics=("arbitrary",), vmem_limit_bytes=VMEM_LIMIT),
        name="route_plan",
    )(gates)


def _moe_kernel(count_ref, x_ref, h_ref, gate_ref, rank_ref, rank_rows_ref, mod_ref, wg_ref, wu_ref, wd_ref, gf_ref,
                o_ref, *, tiles_per_batch):
    i = pl.program_id(0)
    e = pl.program_id(1)
    rows = h_ref.shape[0]
    d = h_ref.shape[1]
    lane = lax.broadcasted_iota(jnp.int32, (rows, LANES), 1)
    ge = jnp.sum(jnp.where(lane == e, gate_ref[...], 0.0), axis=-1, keepdims=True)
    rank_lanes = jnp.broadcast_to(jnp.sum(jnp.where(lane == e, rank_ref[...], 0.0), axis=-1, keepdims=True),
                                  (rows, MOE_SLOTS))
    rank_row = rank_rows_ref[0, pl.ds(e, 1), :]

    @pl.when(e == 0)
    def _():
        o_ref[...] = jnp.zeros(o_ref.shape, F32)

    slot_sub = lax.broadcasted_iota(jnp.int32, (MOE_BLOCK, rows), 0)
    slot_lane = lax.broadcasted_iota(jnp.int32, (1, MOE_SLOTS), 1)

    def block(j, carry):
        lo = j * MOE_BLOCK
        pick = jnp.where(rank_row == (slot_sub + lo).astype(F32), 1.0, 0.0).astype(BF16)
        hg = jnp.dot(pick, h_ref[...], preferred_element_type=F32).astype(BF16)
        gate = jnp.dot(hg, wg_ref[0], preferred_element_type=F32)
        up = jnp.dot(hg, wu_ref[0], preferred_element_type=F32)
        y = jnp.dot((_silu(gate) * up).astype(BF16), wd_ref[0], preferred_element_type=F32).astype(BF16)
        y = jnp.concatenate([y, jnp.zeros((MOE_SLOTS - MOE_BLOCK, d), BF16)], axis=0)
        target = jnp.where(slot_lane < MOE_BLOCK, slot_lane + lo, -2).astype(F32)
        place = jnp.where(rank_lanes == target, 1.0, 0.0).astype(BF16)
        o_ref[...] += ge * jnp.dot(place, y, preferred_element_type=F32)
        return carry

    n_blocks = (count_ref[i * N_EXPERTS + e] + (MOE_BLOCK - 1)) // MOE_BLOCK
    lax.fori_loop(0, n_blocks, block, 0)

    @pl.when(e == N_EXPERTS - 1)
    def _():
        gate5 = mod_ref[pl.ds(i // tiles_per_batch, 1), 5 * D_MODEL:6 * D_MODEL]
        o_ref[...] = _rms(x_ref[...] + gate5 * o_ref[...]) * gf_ref[...]


def _moe_final(x, h, gates, mod, wg, wu, wd, g_final):
    bsz, n, d = x.shape
    assert n % MOE_TILE == 0
    n_tok = bsz * n
    gates = gates.reshape(n_tok, LANES)
    rank, rank_rows, counts = _route_plan(gates)
    counts = counts[:, 0, :N_EXPERTS].reshape(-1)
    kern = functools.partial(_moe_kernel, tiles_per_batch=n // MOE_TILE)
    const = lambda i, e, bnd: (0, 0)
    tok = lambda w: pl.BlockSpec((MOE_TILE, w), lambda i, e, bnd: (i, 0))
    per_expert = lambda a, b: pl.BlockSpec((1, a, b), lambda i, e, bnd: (e, 0, 0))
    out = pl.pallas_call(
        kern,
        out_shape=jax.ShapeDtypeStruct((n_tok, d), F32),
        grid_spec=pltpu.PrefetchScalarGridSpec(
            num_scalar_prefetch=1,
            grid=(n_tok // MOE_TILE, N_EXPERTS),
            in_specs=[tok(d), tok(d), tok(LANES), tok(LANES),
                      pl.BlockSpec((1, N_EXPERTS, MOE_TILE), lambda i, e, bnd: (i, 0, 0)),
                      pl.BlockSpec(mod.shape, const),
                      per_expert(d, D_FF_EXPERT), per_expert(d, D_FF_EXPERT), per_expert(D_FF_EXPERT, d),
                      pl.BlockSpec(g_final.shape, const)],
            out_specs=tok(d)),
        compiler_params=pltpu.CompilerParams(dimension_semantics=("arbitrary", "arbitrary"),
                                             vmem_limit_bytes=VMEM_LIMIT),
        name="moe_final",
    )(counts, x.reshape(n_tok, d), h.reshape(n_tok, d), gates, rank, rank_rows, mod, wg, wu, wd, g_final)
    return out.reshape(bsz, n, d)


def _dt_lanes(v):
    gap = jnp.zeros(v.shape[:-1] + (DT_REP - v.shape[-1],), v.dtype)
    tail = jnp.zeros(v.shape[:-1] + (LANES - 3 * DT_REP,), v.dtype)
    return jnp.concatenate([v, gap, v, gap, v, gap, tail], axis=-1)


def _head_slots(w, per_head, take0, take1, slot=HEAD_SLOT):
    k = w.shape[0]
    w = w.reshape(k, MLA_HEADS, per_head)[:, :, take0:take1]
    w = jnp.pad(w, ((0, 0), (0, 0), (0, slot - (take1 - take0))))
    return w.reshape(k, MLA_HEADS * slot).astype(BF16)


def _q_slots(w):
    k = w.shape[0]
    w = w.reshape(k, MLA_HEADS, MLA_NOPE + MLA_ROPE)
    r = w[:, :, MLA_NOPE:]
    partner = jnp.concatenate([r[:, :, c0:c0 + ROPE_HALF] for c0 in ROPE_PARTNER_STARTS], axis=2)
    return jnp.concatenate([w, partner], axis=2).reshape(k, MLA_HEADS * HEAD_SLOT).astype(BF16)


def _rope_tables(n_latent, ctx_len):
    t = jnp.arange(n_latent)
    pos = jnp.stack([(t // GRID_W).astype(F32), (t % GRID_W).astype(F32)], axis=1)
    n_freq = MLA_ROPE // 4
    inv_freq = ROPE_THETA ** (-jnp.arange(n_freq, dtype=F32) / n_freq)
    ang = pos[:, :, None] * inv_freq
    cos, sin = jnp.cos(ang), jnp.sin(ang)
    cos_r = jnp.stack([cos, cos], axis=2).reshape(n_latent, MLA_ROPE)
    sin_r = jnp.stack([-sin, sin], axis=2).reshape(n_latent, MLA_ROPE)

    def slot(r, fill):
        rope = jnp.concatenate([jnp.full((ctx_len, MLA_ROPE), fill, F32), r], axis=0)
        rows = ctx_len + n_latent
        return jnp.concatenate([jnp.full((rows, ROPE_LANE0), fill, F32), rope, jnp.zeros((rows, MLA_ROPE), F32)],
                               axis=1)

    return slot(cos_r, 1.0), slot(sin_r, 0.0)


def kernel(x, c, ctx, c_ctx, w_mod, b_mod, g_mix, w_in, w_sgu, b_sgu, g_sgu, beta_sgu, conv_w, conv_b, dt_bias,
           a_log, d_skip, g_ssd, g_q, w_uq, g_kv, w_ukv, w_out, g_ffn, w_gate, w_up, w_down, w_router, w_gate_e,
           w_up_e, w_down_e, g_final):
    bsz, n, d = x.shape
    ctx_len = ctx.shape[1]
    depth = w_in.shape[0]
    assert bsz <= CTX_MOD_ROW and ctx_len % CHUNK == 0 and n % CHUNK == 0

    cond = jnp.concatenate([c, jnp.zeros((CTX_MOD_ROW - bsz, d), F32), c_ctx[None],
                            jnp.zeros((MOD_ROWS - CTX_MOD_ROW - 1, d), F32)], axis=0)
    mod_all = _mod_table(cond, w_mod, b_mod)
    cos, sin = _rope_tables(n, ctx_len)
    xa = (ctx, x)
    t_all = ctx_len + n
    rows_in = 768 if t_all % 768 == 0 else ctx_len

    out = None
    for layer in range(depth):
        last = layer == depth - 1
        mod = mod_all[layer]
        row = lambda v: v.reshape(1, -1)
        w_cat = jnp.transpose(w_sgu[layer], (1, 0, 2)).reshape(CHUNK, SGU_HEADS * CHUNK).astype(BF16)
        bias = jnp.repeat(b_sgu[layer].T, SGU_HEAD_DIM, axis=1)
        o_sgu, p_z, p_xbc, p_dt, p_q, p_kv, p_kr = _in_proj(
            xa, t_all, mod, row(g_mix[layer]), w_in, layer, w_cat, bias, row(g_sgu[layer]), row(beta_sgu[layer]),
            ctx_len, rows_in)
        o_ssd = _ssd(p_z, p_xbc, p_dt, conv_w[layer], row(conv_b[layer]), _dt_lanes(dt_bias[layer].reshape(1, -1)),
                     _dt_lanes(a_log[layer].reshape(1, -1)), row(jnp.repeat(d_skip[layer], SSD_HEAD_DIM)), row(g_ssd[layer]),
                     ctx_len)
        per_kv = MLA_NOPE + MLA_V
        q, k, v = _mla_proj(p_q, p_kv, p_kr, row(g_q[layer]), _q_slots(w_uq[layer]),
                            row(g_kv[layer]), _head_slots(w_ukv[layer], per_kv, 0, MLA_NOPE),
                            _head_slots(w_ukv[layer], per_kv, MLA_NOPE, per_kv, MLA_V).T, cos, sin, rows_in)
        o_mla = _attention(q, k, v, ctx_len, latent=True)
        o_mla_ctx = None if last else _attention(q, k, v, ctx_len, latent=False)
        wo = w_out[layer].astype(BF16)
        i = layer // 2
        if layer % 2 == 0:
            if last:
                raise NotImplementedError("final dense channel mixer")
            xa = _mix_ffn(xa, o_sgu, o_ssd, o_mla, o_mla_ctx, mod, wo, row(g_ffn[layer]),
                          w_gate[i].astype(BF16), w_up[i].astype(BF16), w_down[i].astype(BF16), ctx_len)
        else:
            if not last:
                raise NotImplementedError("expert channel mixer on a non-final layer")
            wr_hi = w_router[i].astype(BF16)
            wr_lo = (w_router[i] - wr_hi.astype(F32)).astype(BF16)
            pad = lambda w: jnp.pad(w, ((0, 0), (0, LANES - N_EXPERTS)))
            wr = jnp.concatenate([pad(wr_hi), pad(wr_lo)], axis=1)
            x_mid, h_ffn, gates = _out_proj_route(xa, o_sgu, o_ssd, o_mla, mod, wo, row(g_ffn[layer]), wr,
                                                  ctx_len)
            out = _moe_final(x_mid, h_ffn, gates, mod, w_gate_e[i].astype(BF16), w_up_e[i].astype(BF16),
                             w_down_e[i].astype(BF16), row(g_final))
    return out
```

```python
import functools

import jax
import jax.numpy as jnp
from jax import lax
from jax.experimental import pallas as pl
from jax.experimental.pallas import tpu as pltpu

F32 = jnp.float32
BF16 = jnp.bfloat16

D_MODEL = 1024
EPS = 1e-6
N_MOD = 6
GRID_W = 64
CHUNK = 128

SGU_HEADS = 4
SGU_HEAD_DIM = 64
SGU_WIDTH = 256

SSD_HEADS = 6
SSD_HEAD_DIM = 64
SSD_WIDTH = 384
SSD_GROUPS = 2
SSD_HPG = 3
SSD_STATE = 128
SSD_CONV = 5
SSD_CONV_CH = 896
SSD_GW = SSD_HPG * SSD_HEAD_DIM
DT_REP = 16

MLA_HEADS = 6
MLA_NOPE = 64
MLA_ROPE = 32
MLA_V = 64
MLA_WIDTH = 384
Q_LORA = 384
KV_LORA = 256
ROPE_THETA = 10000.0
MLA_SCALE = (MLA_NOPE + MLA_ROPE) ** -0.5
LOG2_E = 1.4426950408889634
HEAD_SLOT = 128
ROPE_LANE0 = MLA_NOPE
ROPE_HALF = MLA_ROPE // 4
ROPE_PARTNER_STARTS = (ROPE_HALF, 0, 3 * ROPE_HALF, 2 * ROPE_HALF)
assert HEAD_SLOT == ROPE_LANE0 + 2 * MLA_ROPE

D_FF = 2816
N_EXPERTS = 8
D_FF_EXPERT = 1408

LANES = 128
MOD_ROWS = 16
CTX_MOD_ROW = 8

IN_COLS = (("sgu", 0, 512), ("z", 512, 896), ("xbc", 896, 1792), ("dt", 1792, 1920), ("cq", 1920, 2304),
           ("ckv", 2304, 2560), ("kr", 2560, 2688))
IN_PAD_WIDTH = 2688

VMEM_LIMIT = 56 * 1024 * 1024


def _sigmoid(x):
    return 1.0 / (1.0 + jnp.exp(-x))


def _silu(x):
    return x * _sigmoid(x)


def _rms(x):
    return x * lax.rsqrt(jnp.mean(x * x, axis=-1, keepdims=True) + EPS)


def _aligned(v, m):
    return v if isinstance(v, int) else pl.multiple_of(v, m)


def _row_select(mod_ref, b, tile, rows, ctx_len, col0):
    row = tile * rows + lax.broadcasted_iota(jnp.int32, (rows, 1), 0)
    is_ctx = row < ctx_len
    mb = mod_ref[pl.ds(b, 1), col0:col0 + D_MODEL]
    mc = mod_ref[CTX_MOD_ROW:CTX_MOD_ROW + 1, col0:col0 + D_MODEL]
    return jnp.where(is_ctx, mc, mb)


def _mod_kernel(cond_ref, w_ref, b_ref, o_ref):
    s = _silu(cond_ref[...])
    s_hi = s.astype(BF16)
    s_lo = (s - s_hi.astype(F32)).astype(BF16)
    w = w_ref[0]
    w_hi = w.astype(BF16)
    w_lo = (w - w_hi.astype(F32)).astype(BF16)
    by_hi = jnp.dot(jnp.concatenate([s_hi, s_lo], axis=0), w_hi, preferred_element_type=F32)
    o_ref[0] = (by_hi[:MOD_ROWS] + by_hi[MOD_ROWS:] + jnp.dot(s_hi, w_lo, preferred_element_type=F32)
                + b_ref[0])


def _mod_table(cond, w_mod, b_mod):
    n_layers, d, width = w_mod.shape
    cb = 1536
    return pl.pallas_call(
        _mod_kernel,
        out_shape=jax.ShapeDtypeStruct((n_layers, MOD_ROWS, width), F32),
        grid=(n_layers, width // cb),
        in_specs=[pl.BlockSpec((MOD_ROWS, d), lambda l, j: (0, 0)),
                  pl.BlockSpec((1, d, cb), lambda l, j: (l, 0, j)),
                  pl.BlockSpec((1, 1, cb), lambda l, j: (l, 0, j))],
        out_specs=pl.BlockSpec((1, MOD_ROWS, cb), lambda l, j: (l, 0, j)),
        compiler_params=pltpu.CompilerParams(dimension_semantics=("arbitrary", "arbitrary"),
                                             vmem_limit_bytes=VMEM_LIMIT),
        name="mod_table",
    )(cond, w_mod, b_mod.reshape(n_layers, 1, width))


def _token_specs(tokens, ctx_len, rows, d):
    if not isinstance(tokens, tuple):
        return [pl.BlockSpec((1, rows, d), lambda b, t: (b, t, 0))], [tokens]
    ctx, lat = tokens
    n_sub = rows // ctx_len
    assert rows == n_sub * ctx_len
    piece = lambda j: pl.BlockSpec((1, ctx_len, d), lambda b, t: (b, jnp.maximum(n_sub * t + j - 1, 0), 0))
    return ([pl.BlockSpec((1, ctx_len, d), lambda b, t: (b, 0, 0))] + [piece(j) for j in range(n_sub)],
            [ctx] + [lat] * n_sub)


def _tokens_tile(refs, t):
    if len(refs) == 1:
        return refs[0][0]
    first = jnp.where(t == 0, refs[0][0], refs[1][0])
    return first if len(refs) == 2 else jnp.concatenate([first] + [r[0] for r in refs[2:]], axis=0)


IN_RAW_DT = (1792, 1804)
IN_RAW_KR = IN_RAW_DT[1] + Q_LORA + KV_LORA
IN_RAW_WIDTH = IN_RAW_KR + MLA_ROPE


def _pad_in_weight(wraw_ref, w_ref):
    dt0, dt1 = IN_RAW_DT
    for r0 in range(0, D_MODEL, LANES):
        rows = slice(r0, r0 + LANES)
        z = lambda n: jnp.zeros((LANES, n), F32)
        dt = wraw_ref[0, rows, dt0:dt1]
        gap = z(DT_REP - (dt1 - dt0))
        w_ref[rows, :] = jnp.concatenate(
            [wraw_ref[0, rows, 0:dt0], dt, gap, dt, gap, dt, gap, z(LANES - 3 * DT_REP),
             wraw_ref[0, rows, dt1:IN_RAW_KR], z(ROPE_LANE0), wraw_ref[0, rows, IN_RAW_KR:IN_RAW_WIDTH]]
            + [wraw_ref[0, rows, IN_RAW_KR + c0:IN_RAW_KR + c0 + ROPE_HALF] for c0 in ROPE_PARTNER_STARTS],
            axis=1).astype(BF16)


def _sgu_chunks(p_sgu, w_ref, bias_ref, g_ref, beta_ref, o_ref):
    lane = lax.broadcasted_iota(jnp.int32, (1, SGU_WIDTH), 1)
    head_of_lane = lane // SGU_HEAD_DIM
    w = w_ref[...]
    bias = bias_ref[...]
    c0 = 0.7978845608028654
    for c in range(p_sgu.shape[0] // CHUNK):
        p = p_sgu[c * CHUNK:(c + 1) * CHUNK, :]
        ge = 0.5 * p * (1.0 + jnp.tanh(c0 * (p + 0.044715 * (p * p * p))))
        u = ge[:, :SGU_WIDTH]
        v = ge[:, SGU_WIDTH:]
        mu = jnp.mean(v, axis=-1, keepdims=True)
        vc = v - mu
        vn = vc * lax.rsqrt(jnp.mean(vc * vc, axis=-1, keepdims=True) + EPS) * g_ref[...] + beta_ref[...]
        stacked = jnp.concatenate(
            [jnp.where(head_of_lane == h, vn, 0.0).astype(BF16) for h in range(SGU_HEADS)], axis=0)
        mixed = jnp.dot(w, stacked, preferred_element_type=F32) + bias
        o_ref[0, c * CHUNK:(c + 1) * CHUNK, :] = (u * mixed).astype(BF16)


def _in_kernel(*refs, rows, ctx_len, n_tok_refs):
    tok_refs = refs[:n_tok_refs]
    mod_ref, g_ref, wraw_ref, sgu_w_ref, sgu_bias_ref, sgu_g_ref, sgu_beta_ref, *out_refs, w_ref = refs[n_tok_refs:]
    b = pl.program_id(0)
    t = pl.program_id(1)

    @pl.when(jnp.logical_and(b == 0, t == 0))
    def _():
        _pad_in_weight(wraw_ref, w_ref)

    xn = _rms(_tokens_tile(tok_refs, t)) * g_ref[...]
    shift = _row_select(mod_ref, b, t, rows, ctx_len, 0)
    scale = _row_select(mod_ref, b, t, rows, ctx_len, D_MODEL)
    h = (xn * (1.0 + scale) + shift).astype(BF16)
    p = jnp.dot(h, w_ref[...], preferred_element_type=F32)
    (_, s0, s1), other_cols = IN_COLS[0], IN_COLS[1:]
    _sgu_chunks(p[:, s0:s1], sgu_w_ref, sgu_bias_ref, sgu_g_ref, sgu_beta_ref, out_refs[0])
    for o_ref, (_, c0, c1) in zip(out_refs[1:], other_cols):
        o_ref[0] = p[:, c0:c1]


def _in_proj(tokens, t_all, mod, g, w_in, layer, sgu_w, sgu_bias, sgu_g, sgu_beta, ctx_len, rows):
    tok_specs, tok_args = _token_specs(tokens, ctx_len, rows, D_MODEL)
    bsz = tok_args[0].shape[0]
    assert w_in.shape[1:] == (D_MODEL, IN_RAW_WIDTH) and rows % CHUNK == 0
    kern = functools.partial(_in_kernel, rows=rows, ctx_len=ctx_len, n_tok_refs=len(tok_args))
    const = lambda b, t: (0, 0)
    widths = [(SGU_WIDTH, BF16)] + [(c1 - c0, F32) for _, c0, c1 in IN_COLS[1:]]
    return pl.pallas_call(
        kern,
        out_shape=[jax.ShapeDtypeStruct((bsz, t_all, w), dt) for w, dt in widths],
        grid=(bsz, t_all // rows),
        in_specs=tok_specs + [pl.BlockSpec(mod.shape, const), pl.BlockSpec((1, D_MODEL), const),
                              pl.BlockSpec((1,) + w_in.shape[1:], lambda b, t: (layer, 0, 0)),
                              pl.BlockSpec(sgu_w.shape, const), pl.BlockSpec(sgu_bias.shape, const),
                              pl.BlockSpec(sgu_g.shape, const), pl.BlockSpec(sgu_beta.shape, const)],
        out_specs=[pl.BlockSpec((1, rows, w), lambda b, t: (b, t, 0)) for w, _ in widths],
        scratch_shapes=[pltpu.VMEM((D_MODEL, IN_PAD_WIDTH), BF16)],
        compiler_params=pltpu.CompilerParams(dimension_semantics=("arbitrary", "arbitrary"),
                                             vmem_limit_bytes=VMEM_LIMIT),
        name="in_proj",
    )(*tok_args, mod, g, w_in, sgu_w, sgu_bias, sgu_g, sgu_beta)


def _ssd_kernel(z_ref, xbc_ref, dt_ref, cw_ref, cb_ref, dtb_ref, alog_ref, skip_ref, g_ref, o_ref,
                xc_ref, yf_ref, yb_ref, st_ref, *, n_blk, n_ctx_blk):
    def conv_block(blk, seg_start, seg_end):
        r0 = _aligned(blk * CHUNK, CHUNK)
        for cg in range(SSD_CONV_CH // LANES):
            cols = slice(cg * LANES, (cg + 1) * LANES)
            zeros = jnp.zeros((8, LANES), F32)
            top = zeros if seg_start else xbc_ref[0, pl.ds(_aligned(r0 - 8, 8), 8), cols]
            bot = zeros if seg_end else xbc_ref[0, pl.ds(_aligned(r0 + CHUNK, 8), 8), cols]
            xw = jnp.concatenate([top, xbc_ref[0, pl.ds(r0, CHUNK), cols], bot], axis=0)
            acc = cb_ref[:, cols] + cw_ref[0:1, cols] * xw[6:6 + CHUNK]
            for k in range(1, SSD_CONV):
                acc = acc + cw_ref[k:k + 1, cols] * xw[6 + k:6 + k + CHUNK]
            xc_ref[pl.ds(r0, CHUNK), cols] = _silu(acc)

    static_blocks = sorted(set(list(range(n_ctx_blk)) + [n_ctx_blk, n_blk - 1]))
    for blk in static_blocks:
        conv_block(blk, blk == 0 or blk == n_ctx_blk, blk == n_ctx_blk - 1 or blk == n_blk - 1)
    if n_blk - 1 > n_ctx_blk + 1:
        def conv_body(blk, carry):
            conv_block(blk, False, False)
            return carry
        lax.fori_loop(n_ctx_blk + 1, n_blk - 1, conv_body, 0)

    li = lax.broadcasted_iota(jnp.int32, (CHUNK, CHUNK), 0)
    si = lax.broadcasted_iota(jnp.int32, (CHUNK, CHUNK), 1)
    a_neg = -jnp.exp(alog_ref[...])

    def chunk_step(c, direction):
        r0 = pl.multiple_of(c * CHUNK, CHUNK)
        rows = pl.ds(r0, CHUNK)
        mask = (si <= li) if direction == 0 else (si >= li)
        tri = jnp.where(mask, 1.0, 0.0).astype(BF16)
        dtr = dt_ref[0, rows, :] + dtb_ref[...]
        dt = jnp.maximum(dtr, 0.0) + jnp.log1p(jnp.exp(-jnp.abs(dtr)))
        adt = dt * a_neg
        p1 = adt.astype(BF16)
        r1 = adt - p1.astype(F32)
        p2 = r1.astype(BF16)
        p3 = (r1 - p2.astype(F32)).astype(BF16)
        parts = jnp.dot(tri, jnp.concatenate([p1, p2, p3], axis=1), preferred_element_type=F32)
        acs = parts[:, :LANES] + parts[:, LANES:2 * LANES] + parts[:, 2 * LANES:]
        end = CHUNK - 1 if direction == 0 else 0
        tot = acs[end:end + 1, :]
        to_end_dt = jnp.exp(tot - acs) * dt
        chunk_decay = jnp.exp(tot)
        per_head_rows = jnp.where(si < DT_REP, acs, jnp.where(si < 2 * DT_REP, dt, to_end_dt)).T
        first_half = si < SSD_HEAD_DIM
        cbs, bm_ts, y_offs = [], [], []
        state = st_ref[direction]
        state16 = state.astype(BF16)
        for g in range(SSD_GROUPS):
            bm = xc_ref[rows, SSD_WIDTH + g * SSD_STATE:SSD_WIDTH + (g + 1) * SSD_STATE]
            cm = xc_ref[rows, SSD_WIDTH + SSD_GROUPS * SSD_STATE + g * SSD_STATE:
                        SSD_WIDTH + SSD_GROUPS * SSD_STATE + (g + 1) * SSD_STATE]
            cm16 = cm.astype(BF16)
            cbs.append(lax.dot_general(cm16, bm.astype(BF16), (((1,), (1,)), ((), ())),
                                       preferred_element_type=F32))
            bm_ts.append(bm.T)
            y_offs.append(jnp.dot(cm16, state16, preferred_element_type=F32))
        y_pairs, state_pairs = [], []
        for k in range(SSD_HEADS // 2):
            pair = slice(k * LANES, (k + 1) * LANES)
            xs16 = xc_ref[rows, pair].astype(BF16)
            y_diag, contrib, exp_a, dec, y_off = [], [], [], [], []
            for hh in (2 * k, 2 * k + 1):
                g = hh // SSD_HPG
                col = direction * SSD_HEADS + hh
                a_col = jnp.broadcast_to(acs[:, col:col + 1], (CHUNK, CHUNK))
                seg = a_col - per_head_rows[col:col + 1, :]
                decay = jnp.exp(jnp.where(mask, seg, -1e30))
                m = (cbs[g] * decay * per_head_rows[DT_REP + col:DT_REP + col + 1, :]).astype(BF16)
                y_diag.append(jnp.dot(m, xs16, preferred_element_type=F32))
                exp_a.append(jnp.exp(a_col))
                y_off.append(y_offs[g][:, pair])
                lhs = (bm_ts[g] * per_head_rows[2 * DT_REP + col:2 * DT_REP + col + 1, :]).astype(BF16)
                contrib.append(jnp.dot(lhs, xs16, preferred_element_type=F32))
                dec.append(jnp.broadcast_to(chunk_decay[:, col:col + 1], (SSD_STATE, LANES)))
            pick = lambda ab: jnp.where(first_half, ab[0], ab[1])
            y_pairs.append(pick(y_diag) + pick(exp_a) * pick(y_off))
            state_pairs.append(pick(dec) * state[:, pair] + pick(contrib))
        st_ref[direction] = jnp.concatenate(state_pairs, axis=1)
        y = jnp.concatenate(y_pairs, axis=1)
        if direction == 0:
            yf_ref[rows, :] = skip_ref[...] * xc_ref[rows, 0:SSD_WIDTH] + y
        else:
            yb_ref[rows, :] = y

    def scan_body(step, carry):
        chunk_step(step, 0)
        chunk_step(jnp.where(step < n_ctx_blk, n_ctx_blk - 1 - step, n_blk - 1 - (step - n_ctx_blk)), 1)
        return carry

    st_ref[...] = jnp.zeros(st_ref.shape, F32)
    lax.fori_loop(0, n_blk, scan_body, 0)

    def out_body(c, carry):
        rows = pl.ds(pl.multiple_of(c * CHUNK, CHUNK), CHUNK)
        gated = (yf_ref[rows, :] + yb_ref[rows, :]) * _silu(z_ref[0, rows, :])
        o_ref[0, rows, :] = (_rms(gated) * g_ref[...]).astype(BF16)
        return carry

    lax.fori_loop(0, n_blk, out_body, 0)


def _ssd(p_z, p_xbc, p_dt, conv_w, conv_b, dt_bias, a_log, skip, g, ctx_len):
    bsz, t_all, _ = p_z.shape
    kern = functools.partial(_ssd_kernel, n_blk=t_all // CHUNK, n_ctx_blk=ctx_len // CHUNK)
    const = lambda b: (0, 0)
    per_b = lambda w: pl.BlockSpec((1, t_all, w), lambda b: (b, 0, 0))
    return pl.pallas_call(
        kern,
        out_shape=jax.ShapeDtypeStruct((bsz, t_all, SSD_WIDTH), BF16),
        grid=(bsz,),
        in_specs=[per_b(SSD_WIDTH), per_b(SSD_CONV_CH), per_b(LANES),
                  pl.BlockSpec(conv_w.shape, const), pl.BlockSpec(conv_b.shape, const),
                  pl.BlockSpec(dt_bias.shape, const), pl.BlockSpec(a_log.shape, const),
                  pl.BlockSpec(skip.shape, const), pl.BlockSpec(g.shape, const)],
        out_specs=per_b(SSD_WIDTH),
        scratch_shapes=[pltpu.VMEM((t_all, SSD_CONV_CH), F32), pltpu.VMEM((t_all, SSD_WIDTH), F32),
                        pltpu.VMEM((t_all, SSD_WIDTH), F32),
                        pltpu.VMEM((2, SSD_STATE, SSD_WIDTH), F32)],
        compiler_params=pltpu.CompilerParams(dimension_semantics=("arbitrary",), vmem_limit_bytes=VMEM_LIMIT),
        name="ssd",
    )(p_z, p_xbc, p_dt, conv_w, conv_b, dt_bias, a_log, skip, g)


def _mla_proj_kernel(pq_ref, pkv_ref, pkr_ref, gq_ref, wq_ref, gkv_ref, wk_ref, wv_ref, cos_ref, sin_ref,
                     q_ref, k_ref, v_ref):
    cos = cos_ref[...]
    sin = sin_ref[...]

    def rope(t):
        return t * cos + pltpu.roll(t, LANES - MLA_ROPE, 1) * sin

    qn = (_rms(pq_ref[0]) * gq_ref[...]).astype(BF16)
    q = jnp.dot(qn, wq_ref[...], preferred_element_type=F32)
    kvn = (_rms(pkv_ref[0]) * gkv_ref[...]).astype(BF16)
    k = jnp.dot(kvn, wk_ref[...], preferred_element_type=F32)
    v_t = lax.dot_general(wv_ref[...], kvn, (((1,), (1,)), ((), ())),
                          preferred_element_type=F32).astype(BF16)
    kr = rope(pkr_ref[0])
    for h in range(MLA_HEADS):
        slot = slice(h * HEAD_SLOT, (h + 1) * HEAD_SLOT)
        q_ref[0, h] = (rope(q[:, slot]) * (MLA_SCALE * LOG2_E)).astype(BF16)
        k_ref[0, h] = (k[:, slot] + kr).astype(BF16)
        v_ref[0, h] = v_t[h * MLA_V:(h + 1) * MLA_V, :]


def _mla_proj(p_q, p_kv, p_kr, g_q, w_q, g_kv, w_k, w_v, cos, sin, rows):
    bsz, t_all, _ = p_q.shape
    const = lambda b, t: (0, 0)
    tok = lambda w: pl.BlockSpec((1, rows, w), lambda b, t: (b, t, 0))
    tab = pl.BlockSpec((rows, LANES), lambda b, t: (t, 0))
    out = jax.ShapeDtypeStruct((bsz, MLA_HEADS, t_all, HEAD_SLOT), BF16)
    out_vt = jax.ShapeDtypeStruct((bsz, MLA_HEADS, MLA_V, t_all), BF16)
    head_tok = pl.BlockSpec((1, MLA_HEADS, rows, HEAD_SLOT), lambda b, t: (b, 0, t, 0))
    return pl.pallas_call(
        _mla_proj_kernel,
        out_shape=[out, out, out_vt],
        grid=(bsz, t_all // rows),
        in_specs=[tok(Q_LORA), tok(KV_LORA), tok(LANES),
                  pl.BlockSpec(g_q.shape, const), pl.BlockSpec(w_q.shape, const),
                  pl.BlockSpec(g_kv.shape, const), pl.BlockSpec(w_k.shape, const), pl.BlockSpec(w_v.shape, const),
                  tab, tab],
        out_specs=[head_tok, head_tok, pl.BlockSpec((1, MLA_HEADS, MLA_V, rows), lambda b, t: (b, 0, 0, t))],
        compiler_params=pltpu.CompilerParams(dimension_semantics=("arbitrary", "arbitrary"),
                                             vmem_limit_bytes=VMEM_LIMIT),
        name="mla_proj",
    )(p_q, p_kv, p_kr, g_q, w_q, g_kv, w_k, w_v, cos, sin)


ATTN_Q = 512
ATTN_BUFS = 2


def _attn_kernel(*refs, n_q_blocks):
    q_refs = refs[:n_q_blocks]
    k_ref, vt_ref, o_ref, *bufs, ot_ref = refs[n_q_blocks:]
    s_bufs, m_bufs = bufs[:ATTN_BUFS], bufs[ATTN_BUFS:]
    def scores(h, s_ref, m_ref):
        qh = jnp.concatenate([r[0, h] for r in q_refs], axis=0)
        s_t = lax.dot_general(k_ref[0, h], qh, (((1,), (1,)), ((), ())), preferred_element_type=F32)
        s_ref[...] = s_t
        m_ref[...] = jnp.broadcast_to(jnp.max(s_t, axis=0, keepdims=True), m_ref.shape)

    def weigh(h, s_ref, m_ref):
        p_t = jnp.exp2(s_ref[...] - m_ref[0:1, :])
        denom = jnp.sum(p_t, axis=0, keepdims=True)
        o_t = jnp.dot(vt_ref[0, h], p_t.astype(BF16), preferred_element_type=F32)
        ot_ref[pl.ds(pl.multiple_of(h * MLA_V, MLA_V), MLA_V), :] = o_t / denom

    scores(0, s_bufs[0], m_bufs[0])

    def head_group(j, carry):
        h0 = ATTN_BUFS * j
        for i in range(ATTN_BUFS):
            nxt = (i + 1) % ATTN_BUFS
            scores(jnp.minimum(h0 + i + 1, MLA_HEADS - 1), s_bufs[nxt], m_bufs[nxt])
            weigh(h0 + i, s_bufs[i], m_bufs[i])
        return carry

    lax.fori_loop(0, MLA_HEADS // ATTN_BUFS, head_group, 0)
    o_ref[0] = ot_ref[...].T.astype(BF16)


def _attention(q, k, vt, ctx_len, latent):
    bsz, n_heads, t_all, width = q.shape
    if latent:
        n_q_blocks = ATTN_Q // ctx_len
        n_rows, n_keys, tq = t_all - ctx_len, t_all, ATTN_Q
        q_specs = [pl.BlockSpec((1, n_heads, ctx_len, width),
                                functools.partial(lambda b, i, j: (b, 0, n_q_blocks * i + 1 + j, 0), j=j))
                   for j in range(n_q_blocks)]
    else:
        n_q_blocks = 1
        n_rows, n_keys, tq = ctx_len, ctx_len, ctx_len
        q_specs = [pl.BlockSpec((1, n_heads, ctx_len, width), lambda b, i: (b, 0, 0, 0))]
    assert n_rows % tq == 0
    return pl.pallas_call(
        functools.partial(_attn_kernel, n_q_blocks=n_q_blocks),
        out_shape=jax.ShapeDtypeStruct((bsz, n_rows, MLA_WIDTH), BF16),
        grid=(bsz, n_rows // tq),
        in_specs=q_specs + [pl.BlockSpec((1, n_heads, n_keys, width), lambda b, i: (b, 0, 0, 0)),
                            pl.BlockSpec((1, n_heads, MLA_V, n_keys), lambda b, i: (b, 0, 0, 0))],
        out_specs=pl.BlockSpec((1, tq, MLA_WIDTH), lambda b, i: (b, i, 0)),
        scratch_shapes=([pltpu.VMEM((n_keys, tq), F32)] * ATTN_BUFS + [pltpu.VMEM((8, tq), F32)] * ATTN_BUFS
                        + [pltpu.VMEM((MLA_WIDTH, tq), F32)]),
        compiler_params=pltpu.CompilerParams(dimension_semantics=("arbitrary", "arbitrary"),
                                             vmem_limit_bytes=VMEM_LIMIT),
        name="attention",
    )(*([q] * n_q_blocks), k, vt)


def _mix_residual(x_in, o1, o2, o3, mod_ref, wo_ref, g_ref, b, t, rows, ctx_len):
    r1, r2 = SGU_WIDTH, SGU_WIDTH + SSD_WIDTH
    mix = (jnp.dot(o1, wo_ref[0:r1, :], preferred_element_type=F32)
           + jnp.dot(o2, wo_ref[r1:r2, :], preferred_element_type=F32)
           + jnp.dot(o3, wo_ref[r2:, :], preferred_element_type=F32))
    x = x_in + _row_select(mod_ref, b, t, rows, ctx_len, 2 * D_MODEL) * mix
    shift = _row_select(mod_ref, b, t, rows, ctx_len, 3 * D_MODEL)
    scale = _row_select(mod_ref, b, t, rows, ctx_len, 4 * D_MODEL)
    return x, _rms(x) * g_ref[...] * (1.0 + scale) + shift


OUT_ROWS = 512


def _out_kernel(*refs, rows, ctx_len, n_sub):
    x_refs, o1_refs, o2_refs = refs[:n_sub], refs[n_sub:2 * n_sub], refs[2 * n_sub:3 * n_sub]
    o3_ref, mod_ref, wo_ref, g_ref, wr_ref, xo_ref, h_ref, gate_ref = refs[3 * n_sub:]
    pieces = lambda rs: rs[0][0] if n_sub == 1 else jnp.concatenate([r[0] for r in rs], axis=0)
    b = pl.program_id(0)
    t = pl.program_id(1) + 1
    x, h = _mix_residual(pieces(x_refs), pieces(o1_refs), pieces(o2_refs), o3_ref[0], mod_ref, wo_ref, g_ref, b, t,
                         rows, ctx_len)
    xo_ref[0] = x
    h_ref[0] = h.astype(BF16)
    h_hi = h.astype(BF16)
    h_lo = (h - h_hi.astype(F32)).astype(BF16)
    by_hi = jnp.dot(h_hi, wr_ref[...], preferred_element_type=F32)
    logits = by_hi[:, :LANES] + by_hi[:, LANES:] + jnp.dot(h_lo, wr_ref[:, :LANES], preferred_element_type=F32)
    lane = lax.broadcasted_iota(jnp.int32, logits.shape, 1)
    lane_f = lane.astype(F32)
    lg = jnp.where(lane < N_EXPERTS, logits, -jnp.inf)
    m1 = jnp.max(lg, axis=-1, keepdims=True)
    i1 = jnp.min(jnp.where(lg == m1, lane_f, float(LANES)), axis=-1, keepdims=True)
    lg2 = jnp.where(lane_f == i1, -jnp.inf, lg)
    m2 = jnp.max(lg2, axis=-1, keepdims=True)
    i2 = jnp.min(jnp.where(lg2 == m2, lane_f, float(LANES)), axis=-1, keepdims=True)
    e2 = jnp.exp(m2 - m1)
    w_top = 1.0 / (1.0 + e2)
    gate_ref[0] = jnp.where(lane_f == i1, w_top, 0.0) + jnp.where(lane_f == i2, e2 * w_top, 0.0)


def _out_proj_route(xa, o_sgu, o_ssd, o_mla, mod, wo, g, w_router, ctx_len):
    bsz, t_all, d = xa.shape
    n_out = t_all - ctx_len
    rows = OUT_ROWS if n_out % OUT_ROWS == 0 and OUT_ROWS % ctx_len == 0 else ctx_len
    n_sub = rows // ctx_len
    kern = functools.partial(_out_kernel, rows=rows, ctx_len=ctx_len, n_sub=n_sub)
    const = lambda b, t: (0, 0)
    piece = lambda w, j: pl.BlockSpec((1, ctx_len, w), lambda b, t: (b, n_sub * t + 1 + j, 0))
    tok = lambda w: [piece(w, j) for j in range(n_sub)]
    out_tok = lambda w: pl.BlockSpec((1, rows, w), lambda b, t: (b, t, 0))
    return pl.pallas_call(
        kern,
        out_shape=[jax.ShapeDtypeStruct((bsz, n_out, d), F32), jax.ShapeDtypeStruct((bsz, n_out, d), BF16),
                   jax.ShapeDtypeStruct((bsz, n_out, LANES), F32)],
        grid=(bsz, n_out // rows),
        in_specs=tok(d) + tok(SGU_WIDTH) + tok(SSD_WIDTH) + [
            out_tok(MLA_WIDTH), pl.BlockSpec(mod.shape, const), pl.BlockSpec(wo.shape, const),
            pl.BlockSpec(g.shape, const), pl.BlockSpec(w_router.shape, const)],
        out_specs=[out_tok(d), out_tok(d), out_tok(LANES)],
        compiler_params=pltpu.CompilerParams(dimension_semantics=("arbitrary", "arbitrary"),
                                             vmem_limit_bytes=VMEM_LIMIT),
        name="out_proj",
    )(*([xa] * n_sub + [o_sgu] * n_sub + [o_ssd] * n_sub), o_mla, mod, wo, g, w_router)


FF_SPLITS = ((0, 1536), (1536, D_FF))


def _mix_ffn_kernel(*refs, rows, ctx_len, n_tok_refs):
    tok_refs = refs[:n_tok_refs]
    (o1_ref, o2_ref, o3_ref, o3c_ref, mod_ref, wo_ref, g_ref, wg_ref, wu_ref, wd_ref,
     o_ref) = refs[n_tok_refs:]
    b = pl.program_id(0)
    t = pl.program_id(1)
    o3 = jnp.where(t == 0, o3c_ref[0], o3_ref[0])
    x, h = _mix_residual(_tokens_tile(tok_refs, t), o1_ref[0], o2_ref[0], o3, mod_ref, wo_ref, g_ref, b, t, rows,
                         ctx_len)
    h = h.astype(BF16)
    acc = None
    for c0, c1 in FF_SPLITS:
        gate = jnp.dot(h, wg_ref[:, c0:c1], preferred_element_type=F32)
        up = jnp.dot(h, wu_ref[:, c0:c1], preferred_element_type=F32)
        part = jnp.dot((_silu(gate) * up).astype(BF16), wd_ref[c0:c1, :], preferred_element_type=F32)
        acc = part if acc is None else acc + part
    o_ref[0] = x + _row_select(mod_ref, b, t, rows, ctx_len, 5 * D_MODEL) * acc


def _mix_ffn(tokens, o_sgu, o_ssd, o_mla, o_mla_ctx, mod, wo, g, wg, wu, wd, ctx_len):
    bsz, t_all, _ = o_sgu.shape
    d = D_MODEL
    rows = ctx_len
    tok_specs, tok_args = _token_specs(tokens, ctx_len, rows, d)
    kern = functools.partial(_mix_ffn_kernel, rows=rows, ctx_len=ctx_len, n_tok_refs=len(tok_args))
    const = lambda b, t: (0, 0)
    tok = lambda w: pl.BlockSpec((1, rows, w), lambda b, t: (b, t, 0))
    whole = lambda a: pl.BlockSpec(a.shape, const)
    return pl.pallas_call(
        kern,
        out_shape=jax.ShapeDtypeStruct((bsz, t_all, d), F32),
        grid=(bsz, t_all // rows),
        in_specs=tok_specs + [tok(SGU_WIDTH), tok(SSD_WIDTH),
                              pl.BlockSpec((1, rows, MLA_WIDTH), lambda b, t: (b, jnp.maximum(t - 1, 0), 0)),
                              pl.BlockSpec((1, rows, MLA_WIDTH), lambda b, t: (b, 0, 0)),
                              whole(mod), whole(wo), whole(g), whole(wg), whole(wu), whole(wd)],
        out_specs=tok(d),
        compiler_params=pltpu.CompilerParams(dimension_semantics=("arbitrary", "arbitrary"),
                                             vmem_limit_bytes=VMEM_LIMIT),
        name="mix_ffn",
    )(*tok_args, o_sgu, o_ssd, o_mla, o_mla_ctx, mod, wo, g, wg, wu, wd)


MOE_TILE = 1024
MOE_BLOCK = 144
MOE_SLOTS = 256


def _plan_kernel(gate_ref, rank_ref, rank_rows_ref, count_ref):
    rows = gate_ref.shape[0]
    routed = gate_ref[...] > 0.0
    ti = lax.broadcasted_iota(jnp.int32, (rows, rows), 0)
    tj = lax.broadcasted_iota(jnp.int32, (rows, rows), 1)
    earlier = jnp.where(tj < ti, 1.0, 0.0).astype(BF16)
    ones = jnp.where(routed, 1.0, 0.0)
    before = jnp.dot(earlier, ones.astype(BF16), preferred_element_type=F32)
    rank = jnp.where(routed, before, -1.0)
    rank_ref[...] = rank
    rank_rows_ref[0] = rank.T[0:N_EXPERTS, :]
    count_ref[0] = jnp.broadcast_to(jnp.sum(ones, axis=0, keepdims=True), (8, LANES)).astype(jnp.int32)


def _route_plan(gates):
    n_tok = gates.shape[0]
    n_tiles = n_tok // MOE_TILE
    return pl.pallas_call(
        _plan_kernel,
        out_shape=[jax.ShapeDtypeStruct((n_tok, LANES), F32),
                   jax.ShapeDtypeStruct((n_tiles, N_EXPERTS, MOE_TILE), F32),
                   jax.ShapeDtypeStruct((n_tiles, 8, LANES), jnp.int32)],
        grid=(n_tiles,),
        in_specs=[pl.BlockSpec((MOE_TILE, LANES), lambda i: (i, 0))],
        out_specs=[pl.BlockSpec((MOE_TILE, LANES), lambda i: (i, 0)),
                   pl.BlockSpec((1, N_EXPERTS, MOE_TILE), lambda i: (i, 0, 0)),
                   pl.BlockSpec((1, 8, LANES), lambda i: (i, 0, 0))],
        compiler_params=pltpu.CompilerParams(dimension_semantics=("arbitrary",), vmem_limit_bytes=VMEM_LIMIT),
        name="route_plan",
    )(gates)


def _moe_kernel(count_ref, x_ref, h_ref, gate_ref, rank_ref, rank_rows_ref, mod_ref, wg_ref, wu_ref, wd_ref, gf_ref,
                o_ref, *, tiles_per_batch):
    i = pl.program_id(0)
    e = pl.program_id(1)
    rows = h_ref.shape[0]
    d = h_ref.shape[1]
    lane = lax.broadcasted_iota(jnp.int32, (rows, LANES), 1)
    ge = jnp.sum(jnp.where(lane == e, gate_ref[...], 0.0), axis=-1, keepdims=True)
    rank_lanes = jnp.broadcast_to(jnp.sum(jnp.where(lane == e, rank_ref[...], 0.0), axis=-1, keepdims=True),
                                  (rows, MOE_SLOTS))
    rank_row = rank_rows_ref[0, pl.ds(e, 1), :]

    @pl.when(e == 0)
    def _():
        o_ref[...] = jnp.zeros(o_ref.shape, F32)

    slot_sub = lax.broadcasted_iota(jnp.int32, (MOE_BLOCK, rows), 0)
    slot_lane = lax.broadcasted_iota(jnp.int32, (1, MOE_SLOTS), 1)

    def block(j, carry):
        lo = j * MOE_BLOCK
        pick = jnp.where(rank_row == (slot_sub + lo).astype(F32), 1.0, 0.0).astype(BF16)
        hg = jnp.dot(pick, h_ref[...], preferred_element_type=F32).astype(BF16)
        gate = jnp.dot(hg, wg_ref[0], preferred_element_type=F32)
        up = jnp.dot(hg, wu_ref[0], preferred_element_type=F32)
        y = jnp.dot((_silu(gate) * up).astype(BF16), wd_ref[0], preferred_element_type=F32).astype(BF16)
        y = jnp.concatenate([y, jnp.zeros((MOE_SLOTS - MOE_BLOCK, d), BF16)], axis=0)
        target = jnp.where(slot_lane < MOE_BLOCK, slot_lane + lo, -2).astype(F32)
        place = jnp.where(rank_lanes == target, 1.0, 0.0).astype(BF16)
        o_ref[...] += ge * jnp.dot(place, y, preferred_element_type=F32)
        return carry

    n_blocks = (count_ref[i * N_EXPERTS + e] + (MOE_BLOCK - 1)) // MOE_BLOCK
    lax.fori_loop(0, n_blocks, block, 0)

    @pl.when(e == N_EXPERTS - 1)
    def _():
        gate5 = mod_ref[pl.ds(i // tiles_per_batch, 1), 5 * D_MODEL:6 * D_MODEL]
        o_ref[...] = _rms(x_ref[...] + gate5 * o_ref[...]) * gf_ref[...]


def _moe_final(x, h, gates, mod, wg, wu, wd, g_final):
    bsz, n, d = x.shape
    assert n % MOE_TILE == 0
    n_tok = bsz * n
    gates = gates.reshape(n_tok, LANES)
    rank, rank_rows, counts = _route_plan(gates)
    counts = counts[:, 0, :N_EXPERTS].reshape(-1)
    kern = functools.partial(_moe_kernel, tiles_per_batch=n // MOE_TILE)
    const = lambda i, e, bnd: (0, 0)
    tok = lambda w: pl.BlockSpec((MOE_TILE, w), lambda i, e, bnd: (i, 0))
    per_expert = lambda a, b: pl.BlockSpec((1, a, b), lambda i, e, bnd: (e, 0, 0))
    out = pl.pallas_call(
        kern,
        out_shape=jax.ShapeDtypeStruct((n_tok, d), F32),
        grid_spec=pltpu.PrefetchScalarGridSpec(
            num_scalar_prefetch=1,
            grid=(n_tok // MOE_TILE, N_EXPERTS),
            in_specs=[tok(d), tok(d), tok(LANES), tok(LANES),
                      pl.BlockSpec((1, N_EXPERTS, MOE_TILE), lambda i, e, bnd: (i, 0, 0)),
                      pl.BlockSpec(mod.shape, const),
                      per_expert(d, D_FF_EXPERT), per_expert(d, D_FF_EXPERT), per_expert(D_FF_EXPERT, d),
                      pl.BlockSpec(g_final.shape, const)],
            out_specs=tok(d)),
        compiler_params=pltpu.CompilerParams(dimension_semantics=("arbitrary", "arbitrary"),
                                             vmem_limit_bytes=VMEM_LIMIT),
        name="moe_final",
    )(counts, x.reshape(n_tok, d), h.reshape(n_tok, d), gates, rank, rank_rows, mod, wg, wu, wd, g_final)
    return out.reshape(bsz, n, d)


def _dt_lanes(v):
    gap = jnp.zeros(v.shape[:-1] + (DT_REP - v.shape[-1],), v.dtype)
    tail = jnp.zeros(v.shape[:-1] + (LANES - 3 * DT_REP,), v.dtype)
    return jnp.concatenate([v, gap, v, gap, v, gap, tail], axis=-1)


def _head_slots(w, per_head, take0, take1, slot=HEAD_SLOT):
    k = w.shape[0]
    w = w.reshape(k, MLA_HEADS, per_head)[:, :, take0:take1]
    w = jnp.pad(w, ((0, 0), (0, 0), (0, slot - (take1 - take0))))
    return w.reshape(k, MLA_HEADS * slot).astype(BF16)


def _q_slots(w):
    k = w.shape[0]
    w = w.reshape(k, MLA_HEADS, MLA_NOPE + MLA_ROPE)
    r = w[:, :, MLA_NOPE:]
    partner = jnp.concatenate([r[:, :, c0:c0 + ROPE_HALF] for c0 in ROPE_PARTNER_STARTS], axis=2)
    return jnp.concatenate([w, partner], axis=2).reshape(k, MLA_HEADS * HEAD_SLOT).astype(BF16)


def _rope_tables(n_latent, ctx_len):
    t = jnp.arange(n_latent)
    pos = jnp.stack([(t // GRID_W).astype(F32), (t % GRID_W).astype(F32)], axis=1)
    n_freq = MLA_ROPE // 4
    inv_freq = ROPE_THETA ** (-jnp.arange(n_freq, dtype=F32) / n_freq)
    ang = pos[:, :, None] * inv_freq
    cos, sin = jnp.cos(ang), jnp.sin(ang)
    cos_r = jnp.stack([cos, cos], axis=2).reshape(n_latent, MLA_ROPE)
    sin_r = jnp.stack([-sin, sin], axis=2).reshape(n_latent, MLA_ROPE)

    def slot(r, fill):
        rope = jnp.concatenate([jnp.full((ctx_len, MLA_ROPE), fill, F32), r], axis=0)
        rows = ctx_len + n_latent
        return jnp.concatenate([jnp.full((rows, ROPE_LANE0), fill, F32), rope, jnp.zeros((rows, MLA_ROPE), F32)],
                               axis=1)

    return slot(cos_r, 1.0), slot(sin_r, 0.0)


def kernel(x, c, ctx, c_ctx, w_mod, b_mod, g_mix, w_in, w_sgu, b_sgu, g_sgu, beta_sgu, conv_w, conv_b, dt_bias,
           a_log, d_skip, g_ssd, g_q, w_uq, g_kv, w_ukv, w_out, g_ffn, w_gate, w_up, w_down, w_router, w_gate_e,
           w_up_e, w_down_e, g_final):
    bsz, n, d = x.shape
    ctx_len = ctx.shape[1]
    depth = w_in.shape[0]
    assert bsz <= CTX_MOD_ROW and ctx_len % CHUNK == 0 and n % CHUNK == 0

    cond = jnp.concatenate([c, jnp.zeros((CTX_MOD_ROW - bsz, d), F32), c_ctx[None],
                            jnp.zeros((MOD_ROWS - CTX_MOD_ROW - 1, d), F32)], axis=0)
    mod_all = _mod_table(cond, w_mod, b_mod)
    cos, sin = _rope_tables(n, ctx_len)
    xa = (ctx, x)
    t_all = ctx_len + n
    rows_in = 768 if t_all % 768 == 0 else ctx_len

    out = None
    for layer in range(depth):
        last = layer == depth - 1
        mod = mod_all[layer]
        row = lambda v: v.reshape(1, -1)
        w_cat = jnp.transpose(w_sgu[layer], (1, 0, 2)).reshape(CHUNK, SGU_HEADS * CHUNK).astype(BF16)
        bias = jnp.repeat(b_sgu[layer].T, SGU_HEAD_DIM, axis=1)
        o_sgu, p_z, p_xbc, p_dt, p_q, p_kv, p_kr = _in_proj(
            xa, t_all, mod, row(g_mix[layer]), w_in, layer, w_cat, bias, row(g_sgu[layer]), row(beta_sgu[layer]),
            ctx_len, rows_in)
        o_ssd = _ssd(p_z, p_xbc, p_dt, conv_w[layer], row(conv_b[layer]), _dt_lanes(dt_bias[layer].reshape(1, -1)),
                     _dt_lanes(a_log[layer].reshape(1, -1)), row(jnp.repeat(d_skip[layer], SSD_HEAD_DIM)), row(g_ssd[layer]),
                     ctx_len)
        per_kv = MLA_NOPE + MLA_V
        q, k, v = _mla_proj(p_q, p_kv, p_kr, row(g_q[layer]), _q_slots(w_uq[layer]),
                            row(g_kv[layer]), _head_slots(w_ukv[layer], per_kv, 0, MLA_NOPE),
                            _head_slots(w_ukv[layer], per_kv, MLA_NOPE, per_kv, MLA_V).T, cos, sin, rows_in)
        o_mla = _attention(q, k, v, ctx_len, latent=True)
        o_mla_ctx = None if last else _attention(q, k, v, ctx_len, latent=False)
        wo = w_out[layer].astype(BF16)
        i = layer // 2
        if layer % 2 == 0:
            if last:
                raise NotImplementedError("final dense channel mixer")
            xa = _mix_ffn(xa, o_sgu, o_ssd, o_mla, o_mla_ctx, mod, wo, row(g_ffn[layer]),
                          w_gate[i].astype(BF16), w_up[i].astype(BF16), w_down[i].astype(BF16), ctx_len)
        else:
            if not last:
                raise NotImplementedError("expert channel mixer on a non-final layer")
            wr_hi = w_router[i].astype(BF16)
            wr_lo = (w_router[i] - wr_hi.astype(F32)).astype(BF16)
            pad = lambda w: jnp.pad(w, ((0, 0), (0, LANES - N_EXPERTS)))
            wr = jnp.concatenate([pad(wr_hi), pad(wr_lo)], axis=1)
            x_mid, h_ffn, gates = _out_proj_route(xa, o_sgu, o_ssd, o_mla, mod, wo, row(g_ffn[layer]), wr,
                                                  ctx_len)
            out = _moe_final(x_mid, h_ffn, gates, mod, w_gate_e[i].astype(BF16), w_up_e[i].astype(BF16),
                             w_down_e[i].astype(BF16), row(g_final))
    return out
```

```python
import functools

import jax
import jax.numpy as jnp
from jax import lax
from jax.experimental import pallas as pl
from jax.experimental.pallas import tpu as pltpu

F32 = jnp.float32
BF16 = jnp.bfloat16

D_MODEL = 1024
EPS = 1e-6
N_MOD = 6
GRID_W = 64
CHUNK = 128

SGU_HEADS = 4
SGU_HEAD_DIM = 64
SGU_WIDTH = 256

SSD_HEADS = 6
SSD_HEAD_DIM = 64
SSD_WIDTH = 384
SSD_GROUPS = 2
SSD_HPG = 3
SSD_STATE = 128
SSD_CONV = 5
SSD_CONV_CH = 896
SSD_GW = SSD_HPG * SSD_HEAD_DIM
DT_REP = 16

MLA_HEADS = 6
MLA_NOPE = 64
MLA_ROPE = 32
MLA_V = 64
MLA_WIDTH = 384
Q_LORA = 384
KV_LORA = 256
ROPE_THETA = 10000.0
MLA_SCALE = (MLA_NOPE + MLA_ROPE) ** -0.5
LOG2_E = 1.4426950408889634
HEAD_SLOT = 128
ROPE_LANE0 = MLA_NOPE
ROPE_HALF = MLA_ROPE // 4
ROPE_PARTNER_STARTS = (ROPE_HALF, 0, 3 * ROPE_HALF, 2 * ROPE_HALF)
assert HEAD_SLOT == ROPE_LANE0 + 2 * MLA_ROPE

D_FF = 2816
N_EXPERTS = 8
D_FF_EXPERT = 1408

LANES = 128
MOD_ROWS = 16
CTX_MOD_ROW = 8

IN_COLS = (("sgu", 0, 512), ("z", 512, 896), ("xbc", 896, 1792), ("dt", 1792, 1920), ("cq", 1920, 2304),
           ("ckv", 2304, 2560), ("kr", 2560, 2688))
IN_PAD_WIDTH = 2688

VMEM_LIMIT = 56 * 1024 * 1024


def _sigmoid(x):
    return 1.0 / (1.0 + jnp.exp(-x))


def _silu(x):
    return x * _sigmoid(x)


def _rms(x):
    return x * lax.rsqrt(jnp.mean(x * x, axis=-1, keepdims=True) + EPS)


def _aligned(v, m):
    return v if isinstance(v, int) else pl.multiple_of(v, m)


def _row_select(mod_ref, b, tile, rows, ctx_len, col0):
    row = tile * rows + lax.broadcasted_iota(jnp.int32, (rows, 1), 0)
    is_ctx = row < ctx_len
    mb = mod_ref[pl.ds(b, 1), col0:col0 + D_MODEL]
    mc = mod_ref[CTX_MOD_ROW:CTX_MOD_ROW + 1, col0:col0 + D_MODEL]
    return jnp.where(is_ctx, mc, mb)


def _mod_kernel(cond_ref, w_ref, b_ref, o_ref):
    s = _silu(cond_ref[...])
    s_hi = s.astype(BF16)
    s_lo = (s - s_hi.astype(F32)).astype(BF16)
    w = w_ref[0]
    w_hi = w.astype(BF16)
    w_lo = (w - w_hi.astype(F32)).astype(BF16)
    by_hi = jnp.dot(jnp.concatenate([s_hi, s_lo], axis=0), w_hi, preferred_element_type=F32)
    o_ref[0] = (by_hi[:MOD_ROWS] + by_hi[MOD_ROWS:] + jnp.dot(s_hi, w_lo, preferred_element_type=F32)
                + b_ref[0])


def _mod_table(cond, w_mod, b_mod):
    n_layers, d, width = w_mod.shape
    cb = 1536
    return pl.pallas_call(
        _mod_kernel,
        out_shape=jax.ShapeDtypeStruct((n_layers, MOD_ROWS, width), F32),
        grid=(n_layers, width // cb),
        in_specs=[pl.BlockSpec((MOD_ROWS, d), lambda l, j: (0, 0)),
                  pl.BlockSpec((1, d, cb), lambda l, j: (l, 0, j)),
                  pl.BlockSpec((1, 1, cb), lambda l, j: (l, 0, j))],
        out_specs=pl.BlockSpec((1, MOD_ROWS, cb), lambda l, j: (l, 0, j)),
        compiler_params=pltpu.CompilerParams(dimension_semantics=("arbitrary", "arbitrary"),
                                             vmem_limit_bytes=VMEM_LIMIT),
        name="mod_table",
    )(cond, w_mod, b_mod.reshape(n_layers, 1, width))


def _token_specs(tokens, ctx_len, rows, d):
    if not isinstance(tokens, tuple):
        return [pl.BlockSpec((1, rows, d), lambda b, t: (b, t, 0))], [tokens]
    ctx, lat = tokens
    n_sub = rows // ctx_len
    assert rows == n_sub * ctx_len
    piece = lambda j: pl.BlockSpec((1, ctx_len, d), lambda b, t: (b, jnp.maximum(n_sub * t + j - 1, 0), 0))
    return ([pl.BlockSpec((1, ctx_len, d), lambda b, t: (b, 0, 0))] + [piece(j) for j in range(n_sub)],
            [ctx] + [lat] * n_sub)


def _tokens_tile(refs, t):
    if len(refs) == 1:
        return refs[0][0]
    first = jnp.where(t == 0, refs[0][0], refs[1][0])
    return first if len(refs) == 2 else jnp.concatenate([first] + [r[0] for r in refs[2:]], axis=0)


IN_RAW_DT = (1792, 1804)
IN_RAW_KR = IN_RAW_DT[1] + Q_LORA + KV_LORA
IN_RAW_WIDTH = IN_RAW_KR + MLA_ROPE


def _pad_in_weight(wraw_ref, w_ref):
    dt0, dt1 = IN_RAW_DT
    for r0 in range(0, D_MODEL, LANES):
        rows = slice(r0, r0 + LANES)
        z = lambda n: jnp.zeros((LANES, n), F32)
        dt = wraw_ref[0, rows, dt0:dt1]
        gap = z(DT_REP - (dt1 - dt0))
        w_ref[rows, :] = jnp.concatenate(
            [wraw_ref[0, rows, 0:dt0], dt, gap, dt, gap, dt, gap, z(LANES - 3 * DT_REP),
             wraw_ref[0, rows, dt1:IN_RAW_KR], z(ROPE_LANE0), wraw_ref[0, rows, IN_RAW_KR:IN_RAW_WIDTH]]
            + [wraw_ref[0, rows, IN_RAW_KR + c0:IN_RAW_KR + c0 + ROPE_HALF] for c0 in ROPE_PARTNER_STARTS],
            axis=1).astype(BF16)


def _sgu_chunks(p_sgu, w_ref, bias_ref, g_ref, beta_ref, o_ref):
    lane = lax.broadcasted_iota(jnp.int32, (1, SGU_WIDTH), 1)
    head_of_lane = lane // SGU_HEAD_DIM
    w = w_ref[...]
    bias = bias_ref[...]
    c0 = 0.7978845608028654
    for c in range(p_sgu.shape[0] // CHUNK):
        p = p_sgu[c * CHUNK:(c + 1) * CHUNK, :]
        ge = 0.5 * p * (1.0 + jnp.tanh(c0 * (p + 0.044715 * (p * p * p))))
        u = ge[:, :SGU_WIDTH]
        v = ge[:, SGU_WIDTH:]
        mu = jnp.mean(v, axis=-1, keepdims=True)
        vc = v - mu
        vn = vc * lax.rsqrt(jnp.mean(vc * vc, axis=-1, keepdims=True) + EPS) * g_ref[...] + beta_ref[...]
        stacked = jnp.concatenate(
            [jnp.where(head_of_lane == h, vn, 0.0).astype(BF16) for h in range(SGU_HEADS)], axis=0)
        mixed = jnp.dot(w, stacked, preferred_element_type=F32) + bias
        o_ref[0, c * CHUNK:(c + 1) * CHUNK, :] = (u * mixed).astype(BF16)


def _in_kernel(*refs, rows, ctx_len, n_tok_refs):
    tok_refs = refs[:n_tok_refs]
    mod_ref, g_ref, wraw_ref, sgu_w_ref, sgu_bias_ref, sgu_g_ref, sgu_beta_ref, *out_refs, w_ref = refs[n_tok_refs:]
    b = pl.program_id(0)
    t = pl.program_id(1)

    @pl.when(jnp.logical_and(b == 0, t == 0))
    def _():
        _pad_in_weight(wraw_ref, w_ref)

    xn = _rms(_tokens_tile(tok_refs, t)) * g_ref[...]
    shift = _row_select(mod_ref, b, t, rows, ctx_len, 0)
    scale = _row_select(mod_ref, b, t, rows, ctx_len, D_MODEL)
    h = (xn * (1.0 + scale) + shift).astype(BF16)
    p = jnp.dot(h, w_ref[...], preferred_element_type=F32)
    (_, s0, s1), other_cols = IN_COLS[0], IN_COLS[1:]
    _sgu_chunks(p[:, s0:s1], sgu_w_ref, sgu_bias_ref, sgu_g_ref, sgu_beta_ref, out_refs[0])
    for o_ref, (_, c0, c1) in zip(out_refs[1:], other_cols):
        o_ref[0] = p[:, c0:c1]


def _in_proj(tokens, t_all, mod, g, w_in, layer, sgu_w, sgu_bias, sgu_g, sgu_beta, ctx_len, rows):
    tok_specs, tok_args = _token_specs(tokens, ctx_len, rows, D_MODEL)
    bsz = tok_args[0].shape[0]
    assert w_in.shape[1:] == (D_MODEL, IN_RAW_WIDTH) and rows % CHUNK == 0
    kern = functools.partial(_in_kernel, rows=rows, ctx_len=ctx_len, n_tok_refs=len(tok_args))
    const = lambda b, t: (0, 0)
    widths = [(SGU_WIDTH, BF16)] + [(c1 - c0, F32) for _, c0, c1 in IN_COLS[1:]]
    return pl.pallas_call(
        kern,
        out_shape=[jax.ShapeDtypeStruct((bsz, t_all, w), dt) for w, dt in widths],
        grid=(bsz, t_all // rows),
        in_specs=tok_specs + [pl.BlockSpec(mod.shape, const), pl.BlockSpec((1, D_MODEL), const),
                              pl.BlockSpec((1,) + w_in.shape[1:], lambda b, t: (layer, 0, 0)),
                              pl.BlockSpec(sgu_w.shape, const), pl.BlockSpec(sgu_bias.shape, const),
                              pl.BlockSpec(sgu_g.shape, const), pl.BlockSpec(sgu_beta.shape, const)],
        out_specs=[pl.BlockSpec((1, rows, w), lambda b, t: (b, t, 0)) for w, _ in widths],
        scratch_shapes=[pltpu.VMEM((D_MODEL, IN_PAD_WIDTH), BF16)],
        compiler_params=pltpu.CompilerParams(dimension_semantics=("arbitrary", "arbitrary"),
                                             vmem_limit_bytes=VMEM_LIMIT),
        name="in_proj",
    )(*tok_args, mod, g, w_in, sgu_w, sgu_bias, sgu_g, sgu_beta)


def _ssd_kernel(z_ref, xbc_ref, dt_ref, cw_ref, cb_ref, dtb_ref, alog_ref, skip_ref, g_ref, o_ref,
                xc_ref, yf_ref, yb_ref, st_ref, *, n_blk, n_ctx_blk):
    def conv_block(blk, seg_start, seg_end):
        r0 = _aligned(blk * CHUNK, CHUNK)
        for cg in range(SSD_CONV_CH // LANES):
            cols = slice(cg * LANES, (cg + 1) * LANES)
            zeros = jnp.zeros((8, LANES), F32)
            top = zeros if seg_start else xbc_ref[0, pl.ds(_aligned(r0 - 8, 8), 8), cols]
            bot = zeros if seg_end else xbc_ref[0, pl.ds(_aligned(r0 + CHUNK, 8), 8), cols]
            xw = jnp.concatenate([top, xbc_ref[0, pl.ds(r0, CHUNK), cols], bot], axis=0)
            acc = cb_ref[:, cols] + cw_ref[0:1, cols] * xw[6:6 + CHUNK]
            for k in range(1, SSD_CONV):
                acc = acc + cw_ref[k:k + 1, cols] * xw[6 + k:6 + k + CHUNK]
            xc_ref[pl.ds(r0, CHUNK), cols] = _silu(acc)

    n_interior = n_blk - n_ctx_blk - 2
    assert n_interior >= 0 and n_interior % 2 == 0
    for blk in sorted(set(list(range(n_ctx_blk)) + [n_ctx_blk, n_blk - 1])):
        conv_block(blk, blk == 0 or blk == n_ctx_blk, blk == n_ctx_blk - 1 or blk == n_blk - 1)

    li = lax.broadcasted_iota(jnp.int32, (CHUNK, CHUNK), 0)
    si = lax.broadcasted_iota(jnp.int32, (CHUNK, CHUNK), 1)
    a_neg = -jnp.exp(alog_ref[...])

    def chunk_step(c, direction):
        r0 = pl.multiple_of(c * CHUNK, CHUNK)
        rows = pl.ds(r0, CHUNK)
        mask = (si <= li) if direction == 0 else (si >= li)
        tri = jnp.where(mask, 1.0, 0.0).astype(BF16)
        dtr = dt_ref[0, rows, :] + dtb_ref[...]
        dt = jnp.maximum(dtr, 0.0) + jnp.log1p(jnp.exp(-jnp.abs(dtr)))
        adt = dt * a_neg
        p1 = adt.astype(BF16)
        r1 = adt - p1.astype(F32)
        p2 = r1.astype(BF16)
        p3 = (r1 - p2.astype(F32)).astype(BF16)
        parts = jnp.dot(tri, jnp.concatenate([p1, p2, p3], axis=1), preferred_element_type=F32)
        acs = parts[:, :LANES] + parts[:, LANES:2 * LANES] + parts[:, 2 * LANES:]
        end = CHUNK - 1 if direction == 0 else 0
        tot = acs[end:end + 1, :]
        to_end_dt = jnp.exp(tot - acs) * dt
        chunk_decay = jnp.exp(tot)
        per_head_rows = jnp.where(si < DT_REP, acs, jnp.where(si < 2 * DT_REP, dt, to_end_dt)).T
        first_half = si < SSD_HEAD_DIM
        cbs, bm_ts, y_offs = [], [], []
        state = st_ref[direction]
        state16 = state.astype(BF16)
        for g in range(SSD_GROUPS):
            bm = xc_ref[rows, SSD_WIDTH + g * SSD_STATE:SSD_WIDTH + (g + 1) * SSD_STATE]
            cm = xc_ref[rows, SSD_WIDTH + SSD_GROUPS * SSD_STATE + g * SSD_STATE:
                        SSD_WIDTH + SSD_GROUPS * SSD_STATE + (g + 1) * SSD_STATE]
            cm16 = cm.astype(BF16)
            cbs.append(lax.dot_general(cm16, bm.astype(BF16), (((1,), (1,)), ((), ())),
                                       preferred_element_type=F32))
            bm_ts.append(bm.T)
            y_offs.append(jnp.dot(cm16, state16, preferred_element_type=F32))
        y_pairs, state_pairs = [], []
        for k in range(SSD_HEADS // 2):
            pair = slice(k * LANES, (k + 1) * LANES)
            xs16 = xc_ref[rows, pair].astype(BF16)
            y_diag, contrib, exp_a, dec, y_off = [], [], [], [], []
            for hh in (2 * k, 2 * k + 1):
                g = hh // SSD_HPG
                col = direction * SSD_HEADS + hh
                a_col = jnp.broadcast_to(acs[:, col:col + 1], (CHUNK, CHUNK))
                seg = a_col - per_head_rows[col:col + 1, :]
                decay = jnp.exp(jnp.where(mask, seg, -1e30))
                m = (cbs[g] * decay * per_head_rows[DT_REP + col:DT_REP + col + 1, :]).astype(BF16)
                y_diag.append(jnp.dot(m, xs16, preferred_element_type=F32))
                exp_a.append(jnp.exp(a_col))
                y_off.append(y_offs[g][:, pair])
                lhs = (bm_ts[g] * per_head_rows[2 * DT_REP + col:2 * DT_REP + col + 1, :]).astype(BF16)
                contrib.append(jnp.dot(lhs, xs16, preferred_element_type=F32))
                dec.append(jnp.broadcast_to(chunk_decay[:, col:col + 1], (SSD_STATE, LANES)))
            pick = lambda ab: jnp.where(first_half, ab[0], ab[1])
            y_pairs.append(pick(y_diag) + pick(exp_a) * pick(y_off))
            state_pairs.append(pick(dec) * state[:, pair] + pick(contrib))
        st_ref[direction] = jnp.concatenate(state_pairs, axis=1)
        y = jnp.concatenate(y_pairs, axis=1)
        if direction == 0:
            yf_ref[rows, :] = skip_ref[...] * xc_ref[rows, 0:SSD_WIDTH] + y
        else:
            yb_ref[rows, :] = y

    def scan_body(step, carry):
        chunk_step(step, 0)
        chunk_step(jnp.where(step < n_ctx_blk, n_ctx_blk - 1 - step, n_blk - 1 - (step - n_ctx_blk)), 1)
        return carry

    def scan_conv_body(step, carry):
        scan_body(step, carry)
        conv_block(step + 1, False, False)
        conv_block(n_blk - 2 - (step - n_ctx_blk), False, False)
        return carry

    st_ref[...] = jnp.zeros(st_ref.shape, F32)
    conv_steps_end = n_ctx_blk + n_interior // 2
    lax.fori_loop(0, n_ctx_blk, scan_body, 0)
    lax.fori_loop(n_ctx_blk, conv_steps_end, scan_conv_body, 0)
    lax.fori_loop(conv_steps_end, n_blk, scan_body, 0)

    def out_body(c, carry):
        rows = pl.ds(pl.multiple_of(c * CHUNK, CHUNK), CHUNK)
        gated = (yf_ref[rows, :] + yb_ref[rows, :]) * _silu(z_ref[0, rows, :])
        o_ref[0, rows, :] = (_rms(gated) * g_ref[...]).astype(BF16)
        return carry

    lax.fori_loop(0, n_blk, out_body, 0)


def _ssd(p_z, p_xbc, p_dt, conv_w, conv_b, dt_bias, a_log, skip, g, ctx_len):
    bsz, t_all, _ = p_z.shape
    kern = functools.partial(_ssd_kernel, n_blk=t_all // CHUNK, n_ctx_blk=ctx_len // CHUNK)
    const = lambda b: (0, 0)
    per_b = lambda w: pl.BlockSpec((1, t_all, w), lambda b: (b, 0, 0))
    return pl.pallas_call(
        kern,
        out_shape=jax.ShapeDtypeStruct((bsz, t_all, SSD_WIDTH), BF16),
        grid=(bsz,),
        in_specs=[per_b(SSD_WIDTH), per_b(SSD_CONV_CH), per_b(LANES),
                  pl.BlockSpec(conv_w.shape, const), pl.BlockSpec(conv_b.shape, const),
                  pl.BlockSpec(dt_bias.shape, const), pl.BlockSpec(a_log.shape, const),
                  pl.BlockSpec(skip.shape, const), pl.BlockSpec(g.shape, const)],
        out_specs=per_b(SSD_WIDTH),
        scratch_shapes=[pltpu.VMEM((t_all, SSD_CONV_CH), F32), pltpu.VMEM((t_all, SSD_WIDTH), F32),
                        pltpu.VMEM((t_all, SSD_WIDTH), F32),
                        pltpu.VMEM((2, SSD_STATE, SSD_WIDTH), F32)],
        compiler_params=pltpu.CompilerParams(dimension_semantics=("arbitrary",), vmem_limit_bytes=VMEM_LIMIT),
        name="ssd",
    )(p_z, p_xbc, p_dt, conv_w, conv_b, dt_bias, a_log, skip, g)


def _mla_proj_kernel(pq_ref, pkv_ref, pkr_ref, gq_ref, wq_ref, gkv_ref, wk_ref, wv_ref, cos_ref, sin_ref,
                     q_ref, k_ref, v_ref):
    cos = cos_ref[...]
    sin = sin_ref[...]

    def rope(t):
        return t * cos + pltpu.roll(t, LANES - MLA_ROPE, 1) * sin

    qn = (_rms(pq_ref[0]) * gq_ref[...]).astype(BF16)
    q = jnp.dot(qn, wq_ref[...], preferred_element_type=F32)
    kvn = (_rms(pkv_ref[0]) * gkv_ref[...]).astype(BF16)
    k = jnp.dot(kvn, wk_ref[...], preferred_element_type=F32)
    v_t = lax.dot_general(wv_ref[...], kvn, (((1,), (1,)), ((), ())),
                          preferred_element_type=F32).astype(BF16)
    kr = rope(pkr_ref[0])
    for h in range(MLA_HEADS):
        slot = slice(h * HEAD_SLOT, (h + 1) * HEAD_SLOT)
        q_ref[0, h] = (rope(q[:, slot]) * (MLA_SCALE * LOG2_E)).astype(BF16)
        k_ref[0, h] = (k[:, slot] + kr).astype(BF16)
        v_ref[0, h] = v_t[h * MLA_V:(h + 1) * MLA_V, :]


def _mla_proj(p_q, p_kv, p_kr, g_q, w_q, g_kv, w_k, w_v, cos, sin, rows):
    bsz, t_all, _ = p_q.shape
    const = lambda b, t: (0, 0)
    tok = lambda w: pl.BlockSpec((1, rows, w), lambda b, t: (b, t, 0))
    tab = pl.BlockSpec((rows, LANES), lambda b, t: (t, 0))
    out = jax.ShapeDtypeStruct((bsz, MLA_HEADS, t_all, HEAD_SLOT), BF16)
    out_vt = jax.ShapeDtypeStruct((bsz, MLA_HEADS, MLA_V, t_all), BF16)
    head_tok = pl.BlockSpec((1, MLA_HEADS, rows, HEAD_SLOT), lambda b, t: (b, 0, t, 0))
    return pl.pallas_call(
        _mla_proj_kernel,
        out_shape=[out, out, out_vt],
        grid=(bsz, t_all // rows),
        in_specs=[tok(Q_LORA), tok(KV_LORA), tok(LANES),
                  pl.BlockSpec(g_q.shape, const), pl.BlockSpec(w_q.shape, const),
                  pl.BlockSpec(g_kv.shape, const), pl.BlockSpec(w_k.shape, const), pl.BlockSpec(w_v.shape, const),
                  tab, tab],
        out_specs=[head_tok, head_tok, pl.BlockSpec((1, MLA_HEADS, MLA_V, rows), lambda b, t: (b, 0, 0, t))],
        compiler_params=pltpu.CompilerParams(dimension_semantics=("arbitrary", "arbitrary"),
                                             vmem_limit_bytes=VMEM_LIMIT),
        name="mla_proj",
    )(p_q, p_kv, p_kr, g_q, w_q, g_kv, w_k, w_v, cos, sin)


ATTN_Q = 512
ATTN_BUFS = 2


def _attn_kernel(*refs, n_q_blocks):
    q_refs = refs[:n_q_blocks]
    k_ref, vt_ref, o_ref, *bufs, ot_ref = refs[n_q_blocks:]
    s_bufs, m_bufs = bufs[:ATTN_BUFS], bufs[ATTN_BUFS:]
    def scores(h, s_ref, m_ref):
        qh = jnp.concatenate([r[0, h] for r in q_refs], axis=0)
        s_t = lax.dot_general(k_ref[0, h], qh, (((1,), (1,)), ((), ())), preferred_element_type=F32)
        s_ref[...] = s_t
        m_ref[...] = jnp.broadcast_to(jnp.max(s_t, axis=0, keepdims=True), m_ref.shape)

    def weigh(h, s_ref, m_ref):
        p_t = jnp.exp2(s_ref[...] - m_ref[0:1, :])
        denom = jnp.sum(p_t, axis=0, keepdims=True)
        o_t = jnp.dot(vt_ref[0, h], p_t.astype(BF16), preferred_element_type=F32)
        ot_ref[pl.ds(pl.multiple_of(h * MLA_V, MLA_V), MLA_V), :] = o_t / denom

    scores(0, s_bufs[0], m_bufs[0])

    def head_group(j, carry):
        h0 = ATTN_BUFS * j
        for i in range(ATTN_BUFS):
            nxt = (i + 1) % ATTN_BUFS
            scores(jnp.minimum(h0 + i + 1, MLA_HEADS - 1), s_bufs[nxt], m_bufs[nxt])
            weigh(h0 + i, s_bufs[i], m_bufs[i])
        return carry

    lax.fori_loop(0, MLA_HEADS // ATTN_BUFS, head_group, 0)
    o_ref[0] = ot_ref[...].T.astype(BF16)


def _attention(q, k, vt, ctx_len, latent):
    bsz, n_heads, t_all, width = q.shape
    if latent:
        n_q_blocks = ATTN_Q // ctx_len
        n_rows, n_keys, tq = t_all - ctx_len, t_all, ATTN_Q
        q_specs = [pl.BlockSpec((1, n_heads, ctx_len, width),
                                functools.partial(lambda b, i, j: (b, 0, n_q_blocks * i + 1 + j, 0), j=j))
                   for j in range(n_q_blocks)]
    else:
        n_q_blocks = 1
        n_rows, n_keys, tq = ctx_len, ctx_len, ctx_len
        q_specs = [pl.BlockSpec((1, n_heads, ctx_len, width), lambda b, i: (b, 0, 0, 0))]
    assert n_rows % tq == 0
    return pl.pallas_call(
        functools.partial(_attn_kernel, n_q_blocks=n_q_blocks),
        out_shape=jax.ShapeDtypeStruct((bsz, n_rows, MLA_WIDTH), BF16),
        grid=(bsz, n_rows // tq),
        in_specs=q_specs + [pl.BlockSpec((1, n_heads, n_keys, width), lambda b, i: (b, 0, 0, 0)),
                            pl.BlockSpec((1, n_heads, MLA_V, n_keys), lambda b, i: (b, 0, 0, 0))],
        out_specs=pl.BlockSpec((1, tq, MLA_WIDTH), lambda b, i: (b, i, 0)),
        scratch_shapes=([pltpu.VMEM((n_keys, tq), F32)] * ATTN_BUFS + [pltpu.VMEM((8, tq), F32)] * ATTN_BUFS
                        + [pltpu.VMEM((MLA_WIDTH, tq), F32)]),
        compiler_params=pltpu.CompilerParams(dimension_semantics=("arbitrary", "arbitrary"),
                                             vmem_limit_bytes=VMEM_LIMIT),
        name="attention",
    )(*([q] * n_q_blocks), k, vt)


def _mix_residual(x_in, o1, o2, o3, mod_ref, wo_ref, g_ref, b, t, rows, ctx_len):
    r1, r2 = SGU_WIDTH, SGU_WIDTH + SSD_WIDTH
    mix = (jnp.dot(o1, wo_ref[0:r1, :], preferred_element_type=F32)
           + jnp.dot(o2, wo_ref[r1:r2, :], preferred_element_type=F32)
           + jnp.dot(o3, wo_ref[r2:, :], preferred_element_type=F32))
    x = x_in + _row_select(mod_ref, b, t, rows, ctx_len, 2 * D_MODEL) * mix
    shift = _row_select(mod_ref, b, t, rows, ctx_len, 3 * D_MODEL)
    scale = _row_select(mod_ref, b, t, rows, ctx_len, 4 * D_MODEL)
    return x, _rms(x) * g_ref[...] * (1.0 + scale) + shift


OUT_ROWS = 512


def _out_kernel(*refs, rows, ctx_len, n_sub):
    x_refs, o1_refs, o2_refs = refs[:n_sub], refs[n_sub:2 * n_sub], refs[2 * n_sub:3 * n_sub]
    o3_ref, mod_ref, wo_ref, g_ref, wr_ref, xo_ref, h_ref, gate_ref = refs[3 * n_sub:]
    pieces = lambda rs: rs[0][0] if n_sub == 1 else jnp.concatenate([r[0] for r in rs], axis=0)
    b = pl.program_id(0)
    t = pl.program_id(1) + 1
    x, h = _mix_residual(pieces(x_refs), pieces(o1_refs), pieces(o2_refs), o3_ref[0], mod_ref, wo_ref, g_ref, b, t,
                         rows, ctx_len)
    xo_ref[0] = x
    h_ref[0] = h.astype(BF16)
    h_hi = h.astype(BF16)
    h_lo = (h - h_hi.astype(F32)).astype(BF16)
    by_hi = jnp.dot(h_hi, wr_ref[...], preferred_element_type=F32)
    logits = by_hi[:, :LANES] + by_hi[:, LANES:] + jnp.dot(h_lo, wr_ref[:, :LANES], preferred_element_type=F32)
    lane = lax.broadcasted_iota(jnp.int32, logits.shape, 1)
    lane_f = lane.astype(F32)
    lg = jnp.where(lane < N_EXPERTS, logits, -jnp.inf)
    m1 = jnp.max(lg, axis=-1, keepdims=True)
    i1 = jnp.min(jnp.where(lg == m1, lane_f, float(LANES)), axis=-1, keepdims=True)
    lg2 = jnp.where(lane_f == i1, -jnp.inf, lg)
    m2 = jnp.max(lg2, axis=-1, keepdims=True)
    i2 = jnp.min(jnp.where(lg2 == m2, lane_f, float(LANES)), axis=-1, keepdims=True)
    e2 = jnp.exp(m2 - m1)
    w_top = 1.0 / (1.0 + e2)
    gate_ref[0] = jnp.where(lane_f == i1, w_top, 0.0) + jnp.where(lane_f == i2, e2 * w_top, 0.0)


def _out_proj_route(xa, o_sgu, o_ssd, o_mla, mod, wo, g, w_router, ctx_len):
    bsz, t_all, d = xa.shape
    n_out = t_all - ctx_len
    rows = OUT_ROWS if n_out % OUT_ROWS == 0 and OUT_ROWS % ctx_len == 0 else ctx_len
    n_sub = rows // ctx_len
    kern = functools.partial(_out_kernel, rows=rows, ctx_len=ctx_len, n_sub=n_sub)
    const = lambda b, t: (0, 0)
    piece = lambda w, j: pl.BlockSpec((1, ctx_len, w), lambda b, t: (b, n_sub * t + 1 + j, 0))
    tok = lambda w: [piece(w, j) for j in range(n_sub)]
    out_tok = lambda w: pl.BlockSpec((1, rows, w), lambda b, t: (b, t, 0))
    return pl.pallas_call(
        kern,
        out_shape=[jax.ShapeDtypeStruct((bsz, n_out, d), F32), jax.ShapeDtypeStruct((bsz, n_out, d), BF16),
                   jax.ShapeDtypeStruct((bsz, n_out, LANES), F32)],
        grid=(bsz, n_out // rows),
        in_specs=tok(d) + tok(SGU_WIDTH) + tok(SSD_WIDTH) + [
            out_tok(MLA_WIDTH), pl.BlockSpec(mod.shape, const), pl.BlockSpec(wo.shape, const),
            pl.BlockSpec(g.shape, const), pl.BlockSpec(w_router.shape, const)],
        out_specs=[out_tok(d), out_tok(d), out_tok(LANES)],
        compiler_params=pltpu.CompilerParams(dimension_semantics=("arbitrary", "arbitrary"),
                                             vmem_limit_bytes=VMEM_LIMIT),
        name="out_proj",
    )(*([xa] * n_sub + [o_sgu] * n_sub + [o_ssd] * n_sub), o_mla, mod, wo, g, w_router)


FF_SPLITS = ((0, 1536), (1536, D_FF))


def _mix_ffn_kernel(*refs, rows, ctx_len, n_tok_refs):
    tok_refs = refs[:n_tok_refs]
    (o1_ref, o2_ref, o3_ref, o3c_ref, mod_ref, wo_ref, g_ref, wg_ref, wu_ref, wd_ref,
     o_ref) = refs[n_tok_refs:]
    b = pl.program_id(0)
    t = pl.program_id(1)
    o3 = jnp.where(t == 0, o3c_ref[0], o3_ref[0])
    x, h = _mix_residual(_tokens_tile(tok_refs, t), o1_ref[0], o2_ref[0], o3, mod_ref, wo_ref, g_ref, b, t, rows,
                         ctx_len)
    h = h.astype(BF16)
    acc = None
    for c0, c1 in FF_SPLITS:
        gate = jnp.dot(h, wg_ref[:, c0:c1], preferred_element_type=F32)
        up = jnp.dot(h, wu_ref[:, c0:c1], preferred_element_type=F32)
        part = jnp.dot((_silu(gate) * up).astype(BF16), wd_ref[c0:c1, :], preferred_element_type=F32)
        acc = part if acc is None else acc + part
    o_ref[0] = x + _row_select(mod_ref, b, t, rows, ctx_len, 5 * D_MODEL) * acc


def _mix_ffn(tokens, o_sgu, o_ssd, o_mla, o_mla_ctx, mod, wo, g, wg, wu, wd, ctx_len):
    bsz, t_all, _ = o_sgu.shape
    d = D_MODEL
    rows = ctx_len
    tok_specs, tok_args = _token_specs(tokens, ctx_len, rows, d)
    kern = functools.partial(_mix_ffn_kernel, rows=rows, ctx_len=ctx_len, n_tok_refs=len(tok_args))
    const = lambda b, t: (0, 0)
    tok = lambda w: pl.BlockSpec((1, rows, w), lambda b, t: (b, t, 0))
    whole = lambda a: pl.BlockSpec(a.shape, const)
    return pl.pallas_call(
        kern,
        out_shape=jax.ShapeDtypeStruct((bsz, t_all, d), F32),
        grid=(bsz, t_all // rows),
        in_specs=tok_specs + [tok(SGU_WIDTH), tok(SSD_WIDTH),
                              pl.BlockSpec((1, rows, MLA_WIDTH), lambda b, t: (b, jnp.maximum(t - 1, 0), 0)),
                              pl.BlockSpec((1, rows, MLA_WIDTH), lambda b, t: (b, 0, 0)),
                              whole(mod), whole(wo), whole(g), whole(wg), whole(wu), whole(wd)],
        out_specs=tok(d),
        compiler_params=pltpu.CompilerParams(dimension_semantics=("arbitrary", "arbitrary"),
                                             vmem_limit_bytes=VMEM_LIMIT),
        name="mix_ffn",
    )(*tok_args, o_sgu, o_ssd, o_mla, o_mla_ctx, mod, wo, g, wg, wu, wd)


MOE_TILE = 1024
MOE_BLOCK = 144
MOE_SLOTS = 256


def _plan_kernel(gate_ref, rank_ref, rank_rows_ref, count_ref):
    rows = gate_ref.shape[0]
    routed = gate_ref[...] > 0.0
    ti = lax.broadcasted_iota(jnp.int32, (rows, rows), 0)
    tj = lax.broadcasted_iota(jnp.int32, (rows, rows), 1)
    earlier = jnp.where(tj < ti, 1.0, 0.0).astype(BF16)
    ones = jnp.where(routed, 1.0, 0.0)
    before = jnp.dot(earlier, ones.astype(BF16), preferred_element_type=F32)
    rank = jnp.where(routed, before, -1.0)
    rank_ref[...] = rank
    rank_rows_ref[0] = rank.T[0:N_EXPERTS, :]
    count_ref[0] = jnp.broadcast_to(jnp.sum(ones, axis=0, keepdims=True), (8, LANES)).astype(jnp.int32)


def _route_plan(gates):
    n_tok = gates.shape[0]
    n_tiles = n_tok // MOE_TILE
    return pl.pallas_call(
        _plan_kernel,
        out_shape=[jax.ShapeDtypeStruct((n_tok, LANES), F32),
                   jax.ShapeDtypeStruct((n_tiles, N_EXPERTS, MOE_TILE), F32),
                   jax.ShapeDtypeStruct((n_tiles, 8, LANES), jnp.int32)],
        grid=(n_tiles,),
        in_specs=[pl.BlockSpec((MOE_TILE, LANES), lambda i: (i, 0))],
        out_specs=[pl.BlockSpec((MOE_TILE, LANES), lambda i: (i, 0)),
                   pl.BlockSpec((1, N_EXPERTS, MOE_TILE), lambda i: (i, 0, 0)),
                   pl.BlockSpec((1, 8, LANES), lambda i: (i, 0, 0))],
        compiler_params=pltpu.CompilerParams(dimension_semantics=("arbitrary",), vmem_limit_bytes=VMEM_LIMIT),
        name="route_plan",
    )(gates)


def _moe_kernel(count_ref, x_ref, h_ref, gate_ref, rank_ref, rank_rows_ref, mod_ref, wg_ref, wu_ref, wd_ref, gf_ref,
                o_ref, *, tiles_per_batch):
    i = pl.program_id(0)
    e = pl.program_id(1)
    rows = h_ref.shape[0]
    d = h_ref.shape[1]
    lane = lax.broadcasted_iota(jnp.int32, (rows, LANES), 1)
    ge = jnp.sum(jnp.where(lane == e, gate_ref[...], 0.0), axis=-1, keepdims=True)
    rank_lanes = jnp.broadcast_to(jnp.sum(jnp.where(lane == e, rank_ref[...], 0.0), axis=-1, keepdims=True),
                                  (rows, MOE_SLOTS))
    rank_row = rank_rows_ref[0, pl.ds(e, 1), :]

    @pl.when(e == 0)
    def _():
        o_ref[...] = jnp.zeros(o_ref.shape, F32)

    slot_sub = lax.broadcasted_iota(jnp.int32, (MOE_BLOCK, rows), 0)
    slot_lane = lax.broadcasted_iota(jnp.int32, (1, MOE_SLOTS), 1)

    def block(j, carry):
        lo = j * MOE_BLOCK
        pick = jnp.where(rank_row == (slot_sub + lo).astype(F32), 1.0, 0.0).astype(BF16)
        hg = jnp.dot(pick, h_ref[...], preferred_element_type=F32).astype(BF16)
        gate = jnp.dot(hg, wg_ref[0], preferred_element_type=F32)
        up = jnp.dot(hg, wu_ref[0], preferred_element_type=F32)
        y = jnp.dot((_silu(gate) * up).astype(BF16), wd_ref[0], preferred_element_type=F32).astype(BF16)
        y = jnp.concatenate([y, jnp.zeros((MOE_SLOTS - MOE_BLOCK, d), BF16)], axis=0)
        target = jnp.where(slot_lane < MOE_BLOCK, slot_lane + lo, -2).astype(F32)
        place = jnp.where(rank_lanes == target, 1.0, 0.0).astype(BF16)
        o_ref[...] += ge * jnp.dot(place, y, preferred_element_type=F32)
        return carry

    n_blocks = (count_ref[i * N_EXPERTS + e] + (MOE_BLOCK - 1)) // MOE_BLOCK
    lax.fori_loop(0, n_blocks, block, 0)

    @pl.when(e == N_EXPERTS - 1)
    def _():
        gate5 = mod_ref[pl.ds(i // tiles_per_batch, 1), 5 * D_MODEL:6 * D_MODEL]
        o_ref[...] = _rms(x_ref[...] + gate5 * o_ref[...]) * gf_ref[...]


def _moe_final(x, h, gates, mod, wg, wu, wd, g_final):
    bsz, n, d = x.shape
    assert n % MOE_TILE == 0
    n_tok = bsz * n
    gates = gates.reshape(n_tok, LANES)
    rank, rank_rows, counts = _route_plan(gates)
    counts = counts[:, 0, :N_EXPERTS].reshape(-1)
    kern = functools.partial(_moe_kernel, tiles_per_batch=n // MOE_TILE)
    const = lambda i, e, bnd: (0, 0)
    tok = lambda w: pl.BlockSpec((MOE_TILE, w), lambda i, e, bnd: (i, 0))
    per_expert = lambda a, b: pl.BlockSpec((1, a, b), lambda i, e, bnd: (e, 0, 0))
    out = pl.pallas_call(
        kern,
        out_shape=jax.ShapeDtypeStruct((n_tok, d), F32),
        grid_spec=pltpu.PrefetchScalarGridSpec(
            num_scalar_prefetch=1,
            grid=(n_tok // MOE_TILE, N_EXPERTS),
            in_specs=[tok(d), tok(d), tok(LANES), tok(LANES),
                      pl.BlockSpec((1, N_EXPERTS, MOE_TILE), lambda i, e, bnd: (i, 0, 0)),
                      pl.BlockSpec(mod.shape, const),
                      per_expert(d, D_FF_EXPERT), per_expert(d, D_FF_EXPERT), per_expert(D_FF_EXPERT, d),
                      pl.BlockSpec(g_final.shape, const)],
            out_specs=tok(d)),
        compiler_params=pltpu.CompilerParams(dimension_semantics=("arbitrary", "arbitrary"),
                                             vmem_limit_bytes=VMEM_LIMIT),
        name="moe_final",
    )(counts, x.reshape(n_tok, d), h.reshape(n_tok, d), gates, rank, rank_rows, mod, wg, wu, wd, g_final)
    return out.reshape(bsz, n, d)


def _dt_lanes(v):
    gap = jnp.zeros(v.shape[:-1] + (DT_REP - v.shape[-1],), v.dtype)
    tail = jnp.zeros(v.shape[:-1] + (LANES - 3 * DT_REP,), v.dtype)
    return jnp.concatenate([v, gap, v, gap, v, gap, tail], axis=-1)


def _head_slots(w, per_head, take0, take1, slot=HEAD_SLOT):
    k = w.shape[0]
    w = w.reshape(k, MLA_HEADS, per_head)[:, :, take0:take1]
    w = jnp.pad(w, ((0, 0), (0, 0), (0, slot - (take1 - take0))))
    return w.reshape(k, MLA_HEADS * slot).astype(BF16)


def _q_slots(w):
    k = w.shape[0]
    w = w.reshape(k, MLA_HEADS, MLA_NOPE + MLA_ROPE)
    r = w[:, :, MLA_NOPE:]
    partner = jnp.concatenate([r[:, :, c0:c0 + ROPE_HALF] for c0 in ROPE_PARTNER_STARTS], axis=2)
    return jnp.concatenate([w, partner], axis=2).reshape(k, MLA_HEADS * HEAD_SLOT).astype(BF16)


def _rope_tables(n_latent, ctx_len):
    t = jnp.arange(n_latent)
    pos = jnp.stack([(t // GRID_W).astype(F32), (t % GRID_W).astype(F32)], axis=1)
    n_freq = MLA_ROPE // 4
    inv_freq = ROPE_THETA ** (-jnp.arange(n_freq, dtype=F32) / n_freq)
    ang = pos[:, :, None] * inv_freq
    cos, sin = jnp.cos(ang), jnp.sin(ang)
    cos_r = jnp.stack([cos, cos], axis=2).reshape(n_latent, MLA_ROPE)
    sin_r = jnp.stack([-sin, sin], axis=2).reshape(n_latent, MLA_ROPE)

    def slot(r, fill):
        rope = jnp.concatenate([jnp.full((ctx_len, MLA_ROPE), fill, F32), r], axis=0)
        rows = ctx_len + n_latent
        return jnp.concatenate([jnp.full((rows, ROPE_LANE0), fill, F32), rope, jnp.zeros((rows, MLA_ROPE), F32)],
                               axis=1)

    return slot(cos_r, 1.0), slot(sin_r, 0.0)


def kernel(x, c, ctx, c_ctx, w_mod, b_mod, g_mix, w_in, w_sgu, b_sgu, g_sgu, beta_sgu, conv_w, conv_b, dt_bias,
           a_log, d_skip, g_ssd, g_q, w_uq, g_kv, w_ukv, w_out, g_ffn, w_gate, w_up, w_down, w_router, w_gate_e,
           w_up_e, w_down_e, g_final):
    bsz, n, d = x.shape
    ctx_len = ctx.shape[1]
    depth = w_in.shape[0]
    assert bsz <= CTX_MOD_ROW and ctx_len % CHUNK == 0 and n % CHUNK == 0

    cond = jnp.concatenate([c, jnp.zeros((CTX_MOD_ROW - bsz, d), F32), c_ctx[None],
                            jnp.zeros((MOD_ROWS - CTX_MOD_ROW - 1, d), F32)], axis=0)
    mod_all = _mod_table(cond, w_mod, b_mod)
    cos, sin = _rope_tables(n, ctx_len)
    xa = (ctx, x)
    t_all = ctx_len + n
    rows_in = 768 if t_all % 768 == 0 else ctx_len

    out = None
    for layer in range(depth):
        last = layer == depth - 1
        mod = mod_all[layer]
        row = lambda v: v.reshape(1, -1)
        w_cat = jnp.transpose(w_sgu[layer], (1, 0, 2)).reshape(CHUNK, SGU_HEADS * CHUNK).astype(BF16)
        bias = jnp.repeat(b_sgu[layer].T, SGU_HEAD_DIM, axis=1)
        o_sgu, p_z, p_xbc, p_dt, p_q, p_kv, p_kr = _in_proj(
            xa, t_all, mod, row(g_mix[layer]), w_in, layer, w_cat, bias, row(g_sgu[layer]), row(beta_sgu[layer]),
            ctx_len, rows_in)
        o_ssd = _ssd(p_z, p_xbc, p_dt, conv_w[layer], row(conv_b[layer]), _dt_lanes(dt_bias[layer].reshape(1, -1)),
                     _dt_lanes(a_log[layer].reshape(1, -1)), row(jnp.repeat(d_skip[layer], SSD_HEAD_DIM)), row(g_ssd[layer]),
                     ctx_len)
        per_kv = MLA_NOPE + MLA_V
        q, k, v = _mla_proj(p_q, p_kv, p_kr, row(g_q[layer]), _q_slots(w_uq[layer]),
                            row(g_kv[layer]), _head_slots(w_ukv[layer], per_kv, 0, MLA_NOPE),
                            _head_slots(w_ukv[layer], per_kv, MLA_NOPE, per_kv, MLA_V).T, cos, sin, rows_in)
        o_mla = _attention(q, k, v, ctx_len, latent=True)
        o_mla_ctx = None if last else _attention(q, k, v, ctx_len, latent=False)
        wo = w_out[layer].astype(BF16)
        i = layer // 2
        if layer % 2 == 0:
            if last:
                raise NotImplementedError("final dense channel mixer")
            xa = _mix_ffn(xa, o_sgu, o_ssd, o_mla, o_mla_ctx, mod, wo, row(g_ffn[layer]),
                          w_gate[i].astype(BF16), w_up[i].astype(BF16), w_down[i].astype(BF16), ctx_len)
        else:
            if not last:
                raise NotImplementedError("expert channel mixer on a non-final layer")
            wr_hi = w_router[i].astype(BF16)
            wr_lo = (w_router[i] - wr_hi.astype(F32)).astype(BF16)
            pad = lambda w: jnp.pad(w, ((0, 0), (0, LANES - N_EXPERTS)))
            wr = jnp.concatenate([pad(wr_hi), pad(wr_lo)], axis=1)
            x_mid, h_ffn, gates = _out_proj_route(xa, o_sgu, o_ssd, o_mla, mod, wo, row(g_ffn[layer]), wr,
                                                  ctx_len)
            out = _moe_final(x_mid, h_ffn, gates, mod, w_gate_e[i].astype(BF16), w_up_e[i].astype(BF16),
                             w_down_e[i].astype(BF16), row(g_final))
    return out
```

```python
import functools

import jax
import jax.numpy as jnp
from jax import lax
from jax.experimental import pallas as pl
from jax.experimental.pallas import tpu as pltpu

F32 = jnp.float32
BF16 = jnp.bfloat16

D_MODEL = 1024
EPS = 1e-6
N_MOD = 6
GRID_W = 64
CHUNK = 128

SGU_HEADS = 4
SGU_HEAD_DIM = 64
SGU_WIDTH = 256

SSD_HEADS = 6
SSD_HEAD_DIM = 64
SSD_WIDTH = 384
SSD_GROUPS = 2
SSD_HPG = 3
SSD_STATE = 128
SSD_CONV = 5
SSD_CONV_CH = 896
SSD_GW = SSD_HPG * SSD_HEAD_DIM
DT_REP = 16

MLA_HEADS = 6
MLA_NOPE = 64
MLA_ROPE = 32
MLA_V = 64
MLA_WIDTH = 384
Q_LORA = 384
KV_LORA = 256
ROPE_THETA = 10000.0
MLA_SCALE = (MLA_NOPE + MLA_ROPE) ** -0.5
LOG2_E = 1.4426950408889634
HEAD_SLOT = 128
ROPE_LANE0 = MLA_NOPE
ROPE_HALF = MLA_ROPE // 4
ROPE_PARTNER_STARTS = (ROPE_HALF, 0, 3 * ROPE_HALF, 2 * ROPE_HALF)
assert HEAD_SLOT == ROPE_LANE0 + 2 * MLA_ROPE

D_FF = 2816
N_EXPERTS = 8
D_FF_EXPERT = 1408

LANES = 128
MOD_ROWS = 16
CTX_MOD_ROW = 8

IN_COLS = (("sgu", 0, 512), ("z", 512, 896), ("xbc", 896, 1792), ("dt", 1792, 1920), ("cq", 1920, 2304),
           ("ckv", 2304, 2560), ("kr", 2560, 2688))
IN_PAD_WIDTH = 2688

VMEM_LIMIT = 56 * 1024 * 1024


def _sigmoid(x):
    return 1.0 / (1.0 + jnp.exp(-x))


def _silu(x):
    return x * _sigmoid(x)


def _rms(x):
    return x * lax.rsqrt(jnp.mean(x * x, axis=-1, keepdims=True) + EPS)


def _aligned(v, m):
    return v if isinstance(v, int) else pl.multiple_of(v, m)


def _row_select(mod_ref, b, tile, rows, ctx_len, col0):
    row = tile * rows + lax.broadcasted_iota(jnp.int32, (rows, 1), 0)
    is_ctx = row < ctx_len
    mb = mod_ref[pl.ds(b, 1), col0:col0 + D_MODEL]
    mc = mod_ref[CTX_MOD_ROW:CTX_MOD_ROW + 1, col0:col0 + D_MODEL]
    return jnp.where(is_ctx, mc, mb)


def _mod_kernel(cond_ref, w_ref, b_ref, o_ref):
    s = _silu(cond_ref[...])
    s_hi = s.astype(BF16)
    s_lo = (s - s_hi.astype(F32)).astype(BF16)
    w = w_ref[0]
    w_hi = w.astype(BF16)
    w_lo = (w - w_hi.astype(F32)).astype(BF16)
    by_hi = jnp.dot(jnp.concatenate([s_hi, s_lo], axis=0), w_hi, preferred_element_type=F32)
    o_ref[0] = (by_hi[:MOD_ROWS] + by_hi[MOD_ROWS:] + jnp.dot(s_hi, w_lo, preferred_element_type=F32)
                + b_ref[0])


def _mod_table(cond, w_mod, b_mod):
    n_layers, d, width = w_mod.shape
    cb = 1536
    return pl.pallas_call(
        _mod_kernel,
        out_shape=jax.ShapeDtypeStruct((n_layers, MOD_ROWS, width), F32),
        grid=(n_layers, width // cb),
        in_specs=[pl.BlockSpec((MOD_ROWS, d), lambda l, j: (0, 0)),
                  pl.BlockSpec((1, d, cb), lambda l, j: (l, 0, j)),
                  pl.BlockSpec((1, 1, cb), lambda l, j: (l, 0, j))],
        out_specs=pl.BlockSpec((1, MOD_ROWS, cb), lambda l, j: (l, 0, j)),
        compiler_params=pltpu.CompilerParams(dimension_semantics=("arbitrary", "arbitrary"),
                                             vmem_limit_bytes=VMEM_LIMIT),
        name="mod_table",
    )(cond, w_mod, b_mod.reshape(n_layers, 1, width))


def _token_specs(tokens, ctx_len, rows, d):
    if not isinstance(tokens, tuple):
        return [pl.BlockSpec((1, rows, d), lambda b, t: (b, t, 0))], [tokens]
    ctx, lat = tokens
    n_sub = rows // ctx_len
    assert rows == n_sub * ctx_len
    piece = lambda j: pl.BlockSpec((1, ctx_len, d), lambda b, t: (b, jnp.maximum(n_sub * t + j - 1, 0), 0))
    return ([pl.BlockSpec((1, ctx_len, d), lambda b, t: (b, 0, 0))] + [piece(j) for j in range(n_sub)],
            [ctx] + [lat] * n_sub)


def _tokens_tile(refs, t):
    if len(refs) == 1:
        return refs[0][0]
    first = jnp.where(t == 0, refs[0][0], refs[1][0])
    return first if len(refs) == 2 else jnp.concatenate([first] + [r[0] for r in refs[2:]], axis=0)


IN_RAW_DT = (1792, 1804)
IN_RAW_KR = IN_RAW_DT[1] + Q_LORA + KV_LORA
IN_RAW_WIDTH = IN_RAW_KR + MLA_ROPE


def _pad_in_weight(wraw_ref, w_ref):
    dt0, dt1 = IN_RAW_DT
    for r0 in range(0, D_MODEL, LANES):
        rows = slice(r0, r0 + LANES)
        z = lambda n: jnp.zeros((LANES, n), F32)
        dt = wraw_ref[0, rows, dt0:dt1]
        gap = z(DT_REP - (dt1 - dt0))
        w_ref[rows, :] = jnp.concatenate(
            [wraw_ref[0, rows, 0:dt0], dt, gap, dt, gap, dt, gap, z(LANES - 3 * DT_REP),
             wraw_ref[0, rows, dt1:IN_RAW_KR], z(ROPE_LANE0), wraw_ref[0, rows, IN_RAW_KR:IN_RAW_WIDTH]]
            + [wraw_ref[0, rows, IN_RAW_KR + c0:IN_RAW_KR + c0 + ROPE_HALF] for c0 in ROPE_PARTNER_STARTS],
            axis=1).astype(BF16)


def _sgu_chunks(p_sgu, w_ref, bias_ref, g_ref, beta_ref, o_ref):
    lane = lax.broadcasted_iota(jnp.int32, (1, SGU_WIDTH), 1)
    head_of_lane = lane // SGU_HEAD_DIM
    w = w_ref[...]
    bias = bias_ref[...]
    c0 = 0.7978845608028654
    for c in range(p_sgu.shape[0] // CHUNK):
        p = p_sgu[c * CHUNK:(c + 1) * CHUNK, :]
        ge = 0.5 * p * (1.0 + jnp.tanh(c0 * (p + 0.044715 * (p * p * p))))
        u = ge[:, :SGU_WIDTH]
        v = ge[:, SGU_WIDTH:]
        mu = jnp.mean(v, axis=-1, keepdims=True)
        vc = v - mu
        vn = vc * lax.rsqrt(jnp.mean(vc * vc, axis=-1, keepdims=True) + EPS) * g_ref[...] + beta_ref[...]
        stacked = jnp.concatenate(
            [jnp.where(head_of_lane == h, vn, 0.0).astype(BF16) for h in range(SGU_HEADS)], axis=0)
        mixed = jnp.dot(w, stacked, preferred_element_type=F32) + bias
        o_ref[0, c * CHUNK:(c + 1) * CHUNK, :] = (u * mixed).astype(BF16)


def _in_kernel(*refs, rows, ctx_len, n_tok_refs):
    tok_refs = refs[:n_tok_refs]
    mod_ref, g_ref, wraw_ref, sgu_w_ref, sgu_bias_ref, sgu_g_ref, sgu_beta_ref, *out_refs, w_ref = refs[n_tok_refs:]
    b = pl.program_id(0)
    t = pl.program_id(1)

    @pl.when(jnp.logical_and(b == 0, t == 0))
    def _():
        _pad_in_weight(wraw_ref, w_ref)

    xn = _rms(_tokens_tile(tok_refs, t)) * g_ref[...]
    shift = _row_select(mod_ref, b, t, rows, ctx_len, 0)
    scale = _row_select(mod_ref, b, t, rows, ctx_len, D_MODEL)
    h = (xn * (1.0 + scale) + shift).astype(BF16)
    p = jnp.dot(h, w_ref[...], preferred_element_type=F32)
    (_, s0, s1), other_cols = IN_COLS[0], IN_COLS[1:]
    _sgu_chunks(p[:, s0:s1], sgu_w_ref, sgu_bias_ref, sgu_g_ref, sgu_beta_ref, out_refs[0])
    for o_ref, (_, c0, c1) in zip(out_refs[1:], other_cols):
        o_ref[0] = p[:, c0:c1]


def _in_proj(tokens, t_all, mod, g, w_in, layer, sgu_w, sgu_bias, sgu_g, sgu_beta, ctx_len, rows):
    tok_specs, tok_args = _token_specs(tokens, ctx_len, rows, D_MODEL)
    bsz = tok_args[0].shape[0]
    assert w_in.shape[1:] == (D_MODEL, IN_RAW_WIDTH) and rows % CHUNK == 0
    kern = functools.partial(_in_kernel, rows=rows, ctx_len=ctx_len, n_tok_refs=len(tok_args))
    const = lambda b, t: (0, 0)
    widths = [(SGU_WIDTH, BF16)] + [(c1 - c0, F32) for _, c0, c1 in IN_COLS[1:]]
    return pl.pallas_call(
        kern,
        out_shape=[jax.ShapeDtypeStruct((bsz, t_all, w), dt) for w, dt in widths],
        grid=(bsz, t_all // rows),
        in_specs=tok_specs + [pl.BlockSpec(mod.shape, const), pl.BlockSpec((1, D_MODEL), const),
                              pl.BlockSpec((1,) + w_in.shape[1:], lambda b, t: (layer, 0, 0)),
                              pl.BlockSpec(sgu_w.shape, const), pl.BlockSpec(sgu_bias.shape, const),
                              pl.BlockSpec(sgu_g.shape, const), pl.BlockSpec(sgu_beta.shape, const)],
        out_specs=[pl.BlockSpec((1, rows, w), lambda b, t: (b, t, 0)) for w, _ in widths],
        scratch_shapes=[pltpu.VMEM((D_MODEL, IN_PAD_WIDTH), BF16)],
        compiler_params=pltpu.CompilerParams(dimension_semantics=("arbitrary", "arbitrary"),
                                             vmem_limit_bytes=VMEM_LIMIT),
        name="in_proj",
    )(*tok_args, mod, g, w_in, sgu_w, sgu_bias, sgu_g, sgu_beta)


def _ssd_kernel(z_ref, xbc_ref, dt_ref, cw_ref, cb_ref, dtb_ref, alog_ref, skip_ref, g_ref, o_ref,
                xc_ref, yf_ref, yb_ref, st_ref, *, n_blk, n_ctx_blk):
    def conv_block(blk, seg_start, seg_end):
        r0 = _aligned(blk * CHUNK, CHUNK)
        for cg in range(SSD_CONV_CH // LANES):
            cols = slice(cg * LANES, (cg + 1) * LANES)
            zeros = jnp.zeros((8, LANES), F32)
            top = zeros if seg_start else xbc_ref[0, pl.ds(_aligned(r0 - 8, 8), 8), cols]
            bot = zeros if seg_end else xbc_ref[0, pl.ds(_aligned(r0 + CHUNK, 8), 8), cols]
            xw = jnp.concatenate([top, xbc_ref[0, pl.ds(r0, CHUNK), cols], bot], axis=0)
            acc = cb_ref[:, cols] + cw_ref[0:1, cols] * xw[6:6 + CHUNK]
            for k in range(1, SSD_CONV):
                acc = acc + cw_ref[k:k + 1, cols] * xw[6 + k:6 + k + CHUNK]
            xc_ref[pl.ds(r0, CHUNK), cols] = _silu(acc)

    n_interior = n_blk - n_ctx_blk - 2
    assert n_interior >= 0 and n_interior % 2 == 0
    for blk in sorted(set(list(range(n_ctx_blk)) + [n_ctx_blk, n_blk - 1])):
        conv_block(blk, blk == 0 or blk == n_ctx_blk, blk == n_ctx_blk - 1 or blk == n_blk - 1)

    li = lax.broadcasted_iota(jnp.int32, (CHUNK, CHUNK), 0)
    si = lax.broadcasted_iota(jnp.int32, (CHUNK, CHUNK), 1)
    a_neg = -jnp.exp(alog_ref[...])

    def chunk_step(c, direction):
        r0 = pl.multiple_of(c * CHUNK, CHUNK)
        rows = pl.ds(r0, CHUNK)
        mask = (si <= li) if direction == 0 else (si >= li)
        tri = jnp.where(mask, 1.0, 0.0).astype(BF16)
        dtr = dt_ref[0, rows, :] + dtb_ref[...]
        dt = jnp.maximum(dtr, 0.0) + jnp.log1p(jnp.exp(-jnp.abs(dtr)))
        adt = dt * a_neg
        p1 = adt.astype(BF16)
        r1 = adt - p1.astype(F32)
        p2 = r1.astype(BF16)
        p3 = (r1 - p2.astype(F32)).astype(BF16)
        parts = jnp.dot(tri, jnp.concatenate([p1, p2, p3], axis=1), preferred_element_type=F32)
        acs = parts[:, :LANES] + parts[:, LANES:2 * LANES] + parts[:, 2 * LANES:]
        end = CHUNK - 1 if direction == 0 else 0
        tot = acs[end:end + 1, :]
        to_end_dt = jnp.exp(tot - acs) * dt
        chunk_decay = jnp.exp(tot)
        per_head_rows = jnp.where(si < DT_REP, acs, jnp.where(si < 2 * DT_REP, dt, to_end_dt)).T
        first_half = si < SSD_HEAD_DIM
        cbs, bm_ts, y_offs = [], [], []
        state = st_ref[direction]
        state16 = state.astype(BF16)
        for g in range(SSD_GROUPS):
            bm = xc_ref[rows, SSD_WIDTH + g * SSD_STATE:SSD_WIDTH + (g + 1) * SSD_STATE]
            cm = xc_ref[rows, SSD_WIDTH + SSD_GROUPS * SSD_STATE + g * SSD_STATE:
                        SSD_WIDTH + SSD_GROUPS * SSD_STATE + (g + 1) * SSD_STATE]
            cm16 = cm.astype(BF16)
            cbs.append(lax.dot_general(cm16, bm.astype(BF16), (((1,), (1,)), ((), ())),
                                       preferred_element_type=F32))
            bm_ts.append(bm.T)
            y_offs.append(jnp.dot(cm16, state16, preferred_element_type=F32))
        y_pairs, state_pairs = [], []
        for k in range(SSD_HEADS // 2):
            pair = slice(k * LANES, (k + 1) * LANES)
            xs16 = xc_ref[rows, pair].astype(BF16)
            y_diag, contrib, exp_a, dec, y_off = [], [], [], [], []
            for hh in (2 * k, 2 * k + 1):
                g = hh // SSD_HPG
                col = direction * SSD_HEADS + hh
                a_col = jnp.broadcast_to(acs[:, col:col + 1], (CHUNK, CHUNK))
                seg = a_col - per_head_rows[col:col + 1, :]
                decay = jnp.exp(jnp.where(mask, seg, -1e30))
                m = (cbs[g] * decay * per_head_rows[DT_REP + col:DT_REP + col + 1, :]).astype(BF16)
                y_diag.append(jnp.dot(m, xs16, preferred_element_type=F32))
                exp_a.append(jnp.exp(a_col))
                y_off.append(y_offs[g][:, pair])
                lhs = (bm_ts[g] * per_head_rows[2 * DT_REP + col:2 * DT_REP + col + 1, :]).astype(BF16)
                contrib.append(jnp.dot(lhs, xs16, preferred_element_type=F32))
                dec.append(jnp.broadcast_to(chunk_decay[:, col:col + 1], (SSD_STATE, LANES)))
            pick = lambda ab: jnp.where(first_half, ab[0], ab[1])
            y_pairs.append(pick(y_diag) + pick(exp_a) * pick(y_off))
            state_pairs.append(pick(dec) * state[:, pair] + pick(contrib))
        st_ref[direction] = jnp.concatenate(state_pairs, axis=1)
        y = jnp.concatenate(y_pairs, axis=1)
        if direction == 0:
            yf_ref[rows, :] = skip_ref[...] * xc_ref[rows, 0:SSD_WIDTH] + y
        else:
            yb_ref[rows, :] = y

    def scan_body(step, carry):
        chunk_step(step, 0)
        chunk_step(jnp.where(step < n_ctx_blk, n_ctx_blk - 1 - step, n_blk - 1 - (step - n_ctx_blk)), 1)
        return carry

    def scan_conv_body(step, carry):
        scan_body(step, carry)
        conv_block(step + 1, False, False)
        conv_block(n_blk - 2 - (step - n_ctx_blk), False, False)
        return carry

    def out_chunk(c):
        rows = pl.ds(_aligned(c * CHUNK, CHUNK), CHUNK)
        gated = (yf_ref[rows, :] + yb_ref[rows, :]) * _silu(z_ref[0, rows, :])
        o_ref[0, rows, :] = (_rms(gated) * g_ref[...]).astype(BF16)

    def scan_out_body(step, carry):
        scan_body(step, carry)
        out_chunk(step - 1)
        out_chunk(n_blk - 1 - (step - 1 - n_ctx_blk))
        return carry

    st_ref[...] = jnp.zeros(st_ref.shape, F32)
    conv_steps_end = n_ctx_blk + n_interior // 2
    meet = (n_blk + n_ctx_blk) // 2
    lax.fori_loop(0, n_ctx_blk, scan_body, 0)
    lax.fori_loop(n_ctx_blk, conv_steps_end, scan_conv_body, 0)
    lax.fori_loop(conv_steps_end, meet + 1, scan_body, 0)
    lax.fori_loop(meet + 1, n_blk, scan_out_body, 0)
    for c in list(range(n_ctx_blk)) + [n_ctx_blk, n_blk - 1]:
        out_chunk(c)


def _ssd(p_z, p_xbc, p_dt, conv_w, conv_b, dt_bias, a_log, skip, g, ctx_len):
    bsz, t_all, _ = p_z.shape
    kern = functools.partial(_ssd_kernel, n_blk=t_all // CHUNK, n_ctx_blk=ctx_len // CHUNK)
    const = lambda b: (0, 0)
    per_b = lambda w: pl.BlockSpec((1, t_all, w), lambda b: (b, 0, 0))
    return pl.pallas_call(
        kern,
        out_shape=jax.ShapeDtypeStruct((bsz, t_all, SSD_WIDTH), BF16),
        grid=(bsz,),
        in_specs=[per_b(SSD_WIDTH), per_b(SSD_CONV_CH), per_b(LANES),
                  pl.BlockSpec(conv_w.shape, const), pl.BlockSpec(conv_b.shape, const),
                  pl.BlockSpec(dt_bias.shape, const), pl.BlockSpec(a_log.shape, const),
                  pl.BlockSpec(skip.shape, const), pl.BlockSpec(g.shape, const)],
        out_specs=per_b(SSD_WIDTH),
        scratch_shapes=[pltpu.VMEM((t_all, SSD_CONV_CH), F32), pltpu.VMEM((t_all, SSD_WIDTH), F32),
                        pltpu.VMEM((t_all, SSD_WIDTH), F32),
                        pltpu.VMEM((2, SSD_STATE, SSD_WIDTH), F32)],
        compiler_params=pltpu.CompilerParams(dimension_semantics=("arbitrary",), vmem_limit_bytes=VMEM_LIMIT),
        name="ssd",
    )(p_z, p_xbc, p_dt, conv_w, conv_b, dt_bias, a_log, skip, g)


def _mla_proj_kernel(pq_ref, pkv_ref, pkr_ref, gq_ref, wq_ref, gkv_ref, wk_ref, wv_ref, cos_ref, sin_ref,
                     q_ref, k_ref, v_ref):
    cos = cos_ref[...]
    sin = sin_ref[...]

    def rope(t):
        return t * cos + pltpu.roll(t, LANES - MLA_ROPE, 1) * sin

    qn = (_rms(pq_ref[0]) * gq_ref[...]).astype(BF16)
    q = jnp.dot(qn, wq_ref[...], preferred_element_type=F32)
    kvn = (_rms(pkv_ref[0]) * gkv_ref[...]).astype(BF16)
    k = jnp.dot(kvn, wk_ref[...], preferred_element_type=F32)
    v_t = lax.dot_general(wv_ref[...], kvn, (((1,), (1,)), ((), ())),
                          preferred_element_type=F32).astype(BF16)
    kr = rope(pkr_ref[0])
    for h in range(MLA_HEADS):
        slot = slice(h * HEAD_SLOT, (h + 1) * HEAD_SLOT)
        q_ref[0, h] = (rope(q[:, slot]) * (MLA_SCALE * LOG2_E)).astype(BF16)
        k_ref[0, h] = (k[:, slot] + kr).astype(BF16)
        v_ref[0, h] = v_t[h * MLA_V:(h + 1) * MLA_V, :]


def _mla_proj(p_q, p_kv, p_kr, g_q, w_q, g_kv, w_k, w_v, cos, sin, rows):
    bsz, t_all, _ = p_q.shape
    const = lambda b, t: (0, 0)
    tok = lambda w: pl.BlockSpec((1, rows, w), lambda b, t: (b, t, 0))
    tab = pl.BlockSpec((rows, LANES), lambda b, t: (t, 0))
    out = jax.ShapeDtypeStruct((bsz, MLA_HEADS, t_all, HEAD_SLOT), BF16)
    out_vt = jax.ShapeDtypeStruct((bsz, MLA_HEADS, MLA_V, t_all), BF16)
    head_tok = pl.BlockSpec((1, MLA_HEADS, rows, HEAD_SLOT), lambda b, t: (b, 0, t, 0))
    return pl.pallas_call(
        _mla_proj_kernel,
        out_shape=[out, out, out_vt],
        grid=(bsz, t_all // rows),
        in_specs=[tok(Q_LORA), tok(KV_LORA), tok(LANES),
                  pl.BlockSpec(g_q.shape, const), pl.BlockSpec(w_q.shape, const),
                  pl.BlockSpec(g_kv.shape, const), pl.BlockSpec(w_k.shape, const), pl.BlockSpec(w_v.shape, const),
                  tab, tab],
        out_specs=[head_tok, head_tok, pl.BlockSpec((1, MLA_HEADS, MLA_V, rows), lambda b, t: (b, 0, 0, t))],
        compiler_params=pltpu.CompilerParams(dimension_semantics=("arbitrary", "arbitrary"),
                                             vmem_limit_bytes=VMEM_LIMIT),
        name="mla_proj",
    )(p_q, p_kv, p_kr, g_q, w_q, g_kv, w_k, w_v, cos, sin)


ATTN_Q = 512
ATTN_BUFS = 2


def _attn_kernel(*refs, n_q_blocks):
    q_refs = refs[:n_q_blocks]
    k_ref, vt_ref, o_ref, *bufs, ot_ref = refs[n_q_blocks:]
    s_bufs, m_bufs = bufs[:ATTN_BUFS], bufs[ATTN_BUFS:]
    def scores(h, s_ref, m_ref):
        qh = jnp.concatenate([r[0, h] for r in q_refs], axis=0)
        s_t = lax.dot_general(k_ref[0, h], qh, (((1,), (1,)), ((), ())), preferred_element_type=F32)
        s_ref[...] = s_t
        m_ref[...] = jnp.broadcast_to(jnp.max(s_t, axis=0, keepdims=True), m_ref.shape)

    def weigh(h, s_ref, m_ref):
        p_t = jnp.exp2(s_ref[...] - m_ref[0:1, :])
        denom = jnp.sum(p_t, axis=0, keepdims=True)
        o_t = jnp.dot(vt_ref[0, h], p_t.astype(BF16), preferred_element_type=F32)
        ot_ref[pl.ds(pl.multiple_of(h * MLA_V, MLA_V), MLA_V), :] = o_t / denom

    scores(0, s_bufs[0], m_bufs[0])

    def head_group(j, carry):
        h0 = ATTN_BUFS * j
        for i in range(ATTN_BUFS):
            nxt = (i + 1) % ATTN_BUFS
            scores(jnp.minimum(h0 + i + 1, MLA_HEADS - 1), s_bufs[nxt], m_bufs[nxt])
            weigh(h0 + i, s_bufs[i], m_bufs[i])
        return carry

    lax.fori_loop(0, MLA_HEADS // ATTN_BUFS, head_group, 0)
    o_ref[0] = ot_ref[...].T.astype(BF16)


def _attention(q, k, vt, ctx_len, latent):
    bsz, n_heads, t_all, width = q.shape
    if latent:
        n_q_blocks = ATTN_Q // ctx_len
        n_rows, n_keys, tq = t_all - ctx_len, t_all, ATTN_Q
        q_specs = [pl.BlockSpec((1, n_heads, ctx_len, width),
                                functools.partial(lambda b, i, j: (b, 0, n_q_blocks * i + 1 + j, 0), j=j))
                   for j in range(n_q_blocks)]
    else:
        n_q_blocks = 1
        n_rows, n_keys, tq = ctx_len, ctx_len, ctx_len
        q_specs = [pl.BlockSpec((1, n_heads, ctx_len, width), lambda b, i: (b, 0, 0, 0))]
    assert n_rows % tq == 0
    return pl.pallas_call(
        functools.partial(_attn_kernel, n_q_blocks=n_q_blocks),
        out_shape=jax.ShapeDtypeStruct((bsz, n_rows, MLA_WIDTH), BF16),
        grid=(bsz, n_rows // tq),
        in_specs=q_specs + [pl.BlockSpec((1, n_heads, n_keys, width), lambda b, i: (b, 0, 0, 0)),
                            pl.BlockSpec((1, n_heads, MLA_V, n_keys), lambda b, i: (b, 0, 0, 0))],
        out_specs=pl.BlockSpec((1, tq, MLA_WIDTH), lambda b, i: (b, i, 0)),
        scratch_shapes=([pltpu.VMEM((n_keys, tq), F32)] * ATTN_BUFS + [pltpu.VMEM((8, tq), F32)] * ATTN_BUFS
                        + [pltpu.VMEM((MLA_WIDTH, tq), F32)]),
        compiler_params=pltpu.CompilerParams(dimension_semantics=("arbitrary", "arbitrary"),
                                             vmem_limit_bytes=VMEM_LIMIT),
        name="attention",
    )(*([q] * n_q_blocks), k, vt)


def _mix_residual(x_in, o1, o2, o3, mod_ref, wo_ref, g_ref, b, t, rows, ctx_len):
    r1, r2 = SGU_WIDTH, SGU_WIDTH + SSD_WIDTH
    mix = (jnp.dot(o1, wo_ref[0:r1, :], preferred_element_type=F32)
           + jnp.dot(o2, wo_ref[r1:r2, :], preferred_element_type=F32)
           + jnp.dot(o3, wo_ref[r2:, :], preferred_element_type=F32))
    x = x_in + _row_select(mod_ref, b, t, rows, ctx_len, 2 * D_MODEL) * mix
    shift = _row_select(mod_ref, b, t, rows, ctx_len, 3 * D_MODEL)
    scale = _row_select(mod_ref, b, t, rows, ctx_len, 4 * D_MODEL)
    return x, _rms(x) * g_ref[...] * (1.0 + scale) + shift


OUT_ROWS = 512


def _out_kernel(*refs, rows, ctx_len, n_sub):
    x_refs, o1_refs, o2_refs = refs[:n_sub], refs[n_sub:2 * n_sub], refs[2 * n_sub:3 * n_sub]
    o3_ref, mod_ref, wo_ref, g_ref, wr_ref, xo_ref, h_ref, gate_ref = refs[3 * n_sub:]
    pieces = lambda rs: rs[0][0] if n_sub == 1 else jnp.concatenate([r[0] for r in rs], axis=0)
    b = pl.program_id(0)
    t = pl.program_id(1) + 1
    x, h = _mix_residual(pieces(x_refs), pieces(o1_refs), pieces(o2_refs), o3_ref[0], mod_ref, wo_ref, g_ref, b, t,
                         rows, ctx_len)
    xo_ref[0] = x
    h_ref[0] = h.astype(BF16)
    h_hi = h.astype(BF16)
    h_lo = (h - h_hi.astype(F32)).astype(BF16)
    by_hi = jnp.dot(h_hi, wr_ref[...], preferred_element_type=F32)
    logits = by_hi[:, :LANES] + by_hi[:, LANES:] + jnp.dot(h_lo, wr_ref[:, :LANES], preferred_element_type=F32)
    lane = lax.broadcasted_iota(jnp.int32, logits.shape, 1)
    lane_f = lane.astype(F32)
    lg = jnp.where(lane < N_EXPERTS, logits, -jnp.inf)
    m1 = jnp.max(lg, axis=-1, keepdims=True)
    i1 = jnp.min(jnp.where(lg == m1, lane_f, float(LANES)), axis=-1, keepdims=True)
    lg2 = jnp.where(lane_f == i1, -jnp.inf, lg)
    m2 = jnp.max(lg2, axis=-1, keepdims=True)
    i2 = jnp.min(jnp.where(lg2 == m2, lane_f, float(LANES)), axis=-1, keepdims=True)
    e2 = jnp.exp(m2 - m1)
    w_top = 1.0 / (1.0 + e2)
    gate_ref[0] = jnp.where(lane_f == i1, w_top, 0.0) + jnp.where(lane_f == i2, e2 * w_top, 0.0)


def _out_proj_route(xa, o_sgu, o_ssd, o_mla, mod, wo, g, w_router, ctx_len):
    bsz, t_all, d = xa.shape
    n_out = t_all - ctx_len
    rows = OUT_ROWS if n_out % OUT_ROWS == 0 and OUT_ROWS % ctx_len == 0 else ctx_len
    n_sub = rows // ctx_len
    kern = functools.partial(_out_kernel, rows=rows, ctx_len=ctx_len, n_sub=n_sub)
    const = lambda b, t: (0, 0)
    piece = lambda w, j: pl.BlockSpec((1, ctx_len, w), lambda b, t: (b, n_sub * t + 1 + j, 0))
    tok = lambda w: [piece(w, j) for j in range(n_sub)]
    out_tok = lambda w: pl.BlockSpec((1, rows, w), lambda b, t: (b, t, 0))
    return pl.pallas_call(
        kern,
        out_shape=[jax.ShapeDtypeStruct((bsz, n_out, d), F32), jax.ShapeDtypeStruct((bsz, n_out, d), BF16),
                   jax.ShapeDtypeStruct((bsz, n_out, LANES), F32)],
        grid=(bsz, n_out // rows),
        in_specs=tok(d) + tok(SGU_WIDTH) + tok(SSD_WIDTH) + [
            out_tok(MLA_WIDTH), pl.BlockSpec(mod.shape, const), pl.BlockSpec(wo.shape, const),
            pl.BlockSpec(g.shape, const), pl.BlockSpec(w_router.shape, const)],
        out_specs=[out_tok(d), out_tok(d), out_tok(LANES)],
        compiler_params=pltpu.CompilerParams(dimension_semantics=("arbitrary", "arbitrary"),
                                             vmem_limit_bytes=VMEM_LIMIT),
        name="out_proj",
    )(*([xa] * n_sub + [o_sgu] * n_sub + [o_ssd] * n_sub), o_mla, mod, wo, g, w_router)


FF_SPLITS = ((0, 1536), (1536, D_FF))


def _mix_ffn_kernel(*refs, rows, ctx_len, n_tok_refs):
    tok_refs = refs[:n_tok_refs]
    (o1_ref, o2_ref, o3_ref, o3c_ref, mod_ref, wo_ref, g_ref, wg_ref, wu_ref, wd_ref,
     o_ref) = refs[n_tok_refs:]
    b = pl.program_id(0)
    t = pl.program_id(1)
    o3 = jnp.where(t == 0, o3c_ref[0], o3_ref[0])
    x, h = _mix_residual(_tokens_tile(tok_refs, t), o1_ref[0], o2_ref[0], o3, mod_ref, wo_ref, g_ref, b, t, rows,
                         ctx_len)
    h = h.astype(BF16)
    acc = None
    for c0, c1 in FF_SPLITS:
        gate = jnp.dot(h, wg_ref[:, c0:c1], preferred_element_type=F32)
        up = jnp.dot(h, wu_ref[:, c0:c1], preferred_element_type=F32)
        part = jnp.dot((_silu(gate) * up).astype(BF16), wd_ref[c0:c1, :], preferred_element_type=F32)
        acc = part if acc is None else acc + part
    o_ref[0] = x + _row_select(mod_ref, b, t, rows, ctx_len, 5 * D_MODEL) * acc


def _mix_ffn(tokens, o_sgu, o_ssd, o_mla, o_mla_ctx, mod, wo, g, wg, wu, wd, ctx_len):
    bsz, t_all, _ = o_sgu.shape
    d = D_MODEL
    rows = ctx_len
    tok_specs, tok_args = _token_specs(tokens, ctx_len, rows, d)
    kern = functools.partial(_mix_ffn_kernel, rows=rows, ctx_len=ctx_len, n_tok_refs=len(tok_args))
    const = lambda b, t: (0, 0)
    tok = lambda w: pl.BlockSpec((1, rows, w), lambda b, t: (b, t, 0))
    whole = lambda a: pl.BlockSpec(a.shape, const)
    return pl.pallas_call(
        kern,
        out_shape=jax.ShapeDtypeStruct((bsz, t_all, d), F32),
        grid=(bsz, t_all // rows),
        in_specs=tok_specs + [tok(SGU_WIDTH), tok(SSD_WIDTH),
                              pl.BlockSpec((1, rows, MLA_WIDTH), lambda b, t: (b, jnp.maximum(t - 1, 0), 0)),
                              pl.BlockSpec((1, rows, MLA_WIDTH), lambda b, t: (b, 0, 0)),
                              whole(mod), whole(wo), whole(g), whole(wg), whole(wu), whole(wd)],
        out_specs=tok(d),
        compiler_params=pltpu.CompilerParams(dimension_semantics=("arbitrary", "arbitrary"),
                                             vmem_limit_bytes=VMEM_LIMIT),
        name="mix_ffn",
    )(*tok_args, o_sgu, o_ssd, o_mla, o_mla_ctx, mod, wo, g, wg, wu, wd)


MOE_TILE = 1024
MOE_BLOCK = 144
MOE_SLOTS = 256


def _plan_kernel(gate_ref, rank_ref, rank_rows_ref, count_ref):
    rows = gate_ref.shape[0]
    routed = gate_ref[...] > 0.0
    ti = lax.broadcasted_iota(jnp.int32, (rows, rows), 0)
    tj = lax.broadcasted_iota(jnp.int32, (rows, rows), 1)
    earlier = jnp.where(tj < ti, 1.0, 0.0).astype(BF16)
    ones = jnp.where(routed, 1.0, 0.0)
    before = jnp.dot(earlier, ones.astype(BF16), preferred_element_type=F32)
    rank = jnp.where(routed, before, -1.0)
    rank_ref[...] = rank
    rank_rows_ref[0] = rank.T[0:N_EXPERTS, :]
    count_ref[0] = jnp.broadcast_to(jnp.sum(ones, axis=0, keepdims=True), (8, LANES)).astype(jnp.int32)


def _route_plan(gates):
    n_tok = gates.shape[0]
    n_tiles = n_tok // MOE_TILE
    return pl.pallas_call(
        _plan_kernel,
        out_shape=[jax.ShapeDtypeStruct((n_tok, LANES), F32),
                   jax.ShapeDtypeStruct((n_tiles, N_EXPERTS, MOE_TILE), F32),
                   jax.ShapeDtypeStruct((n_tiles, 8, LANES), jnp.int32)],
        grid=(n_tiles,),
        in_specs=[pl.BlockSpec((MOE_TILE, LANES), lambda i: (i, 0))],
        out_specs=[pl.BlockSpec((MOE_TILE, LANES), lambda i: (i, 0)),
                   pl.BlockSpec((1, N_EXPERTS, MOE_TILE), lambda i: (i, 0, 0)),
                   pl.BlockSpec((1, 8, LANES), lambda i: (i, 0, 0))],
        compiler_params=pltpu.CompilerParams(dimension_semantics=("arbitrary",), vmem_limit_bytes=VMEM_LIMIT),
        name="route_plan",
    )(gates)


def _moe_kernel(count_ref, x_ref, h_ref, gate_ref, rank_ref, rank_rows_ref, mod_ref, wg_ref, wu_ref, wd_ref, gf_ref,
                o_ref, *, tiles_per_batch):
    i = pl.program_id(0)
    e = pl.program_id(1)
    rows = h_ref.shape[0]
    d = h_ref.shape[1]
    lane = lax.broadcasted_iota(jnp.int32, (rows, LANES), 1)
    ge = jnp.sum(jnp.where(lane == e, gate_ref[...], 0.0), axis=-1, keepdims=True)
    rank_lanes = jnp.broadcast_to(jnp.sum(jnp.where(lane == e, rank_ref[...], 0.0), axis=-1, keepdims=True),
                                  (rows, MOE_SLOTS))
    rank_row = rank_rows_ref[0, pl.ds(e, 1), :]

    @pl.when(e == 0)
    def _():
        o_ref[...] = jnp.zeros(o_ref.shape, F32)

    slot_sub = lax.broadcasted_iota(jnp.int32, (MOE_BLOCK, rows), 0)
    slot_lane = lax.broadcasted_iota(jnp.int32, (1, MOE_SLOTS), 1)

    def block(j, carry):
        lo = j * MOE_BLOCK
        pick = jnp.where(rank_row == (slot_sub + lo).astype(F32), 1.0, 0.0).astype(BF16)
        hg = jnp.dot(pick, h_ref[...], preferred_element_type=F32).astype(BF16)
        gate = jnp.dot(hg, wg_ref[0], preferred_element_type=F32)
        up = jnp.dot(hg, wu_ref[0], preferred_element_type=F32)
        y = jnp.dot((_silu(gate) * up).astype(BF16), wd_ref[0], preferred_element_type=F32).astype(BF16)
        y = jnp.concatenate([y, jnp.zeros((MOE_SLOTS - MOE_BLOCK, d), BF16)], axis=0)
        target = jnp.where(slot_lane < MOE_BLOCK, slot_lane + lo, -2).astype(F32)
        place = jnp.where(rank_lanes == target, 1.0, 0.0).astype(BF16)
        o_ref[...] += ge * jnp.dot(place, y, preferred_element_type=F32)
        return carry

    n_blocks = (count_ref[i * N_EXPERTS + e] + (MOE_BLOCK - 1)) // MOE_BLOCK
    lax.fori_loop(0, n_blocks, block, 0)

    @pl.when(e == N_EXPERTS - 1)
    def _():
        gate5 = mod_ref[pl.ds(i // tiles_per_batch, 1), 5 * D_MODEL:6 * D_MODEL]
        o_ref[...] = _rms(x_ref[...] + gate5 * o_ref[...]) * gf_ref[...]


def _moe_final(x, h, gates, mod, wg, wu, wd, g_final):
    bsz, n, d = x.shape
    assert n % MOE_TILE == 0
    n_tok = bsz * n
    gates = gates.reshape(n_tok, LANES)
    rank, rank_rows, counts = _route_plan(gates)
    counts = counts[:, 0, :N_EXPERTS].reshape(-1)
    kern = functools.partial(_moe_kernel, tiles_per_batch=n // MOE_TILE)
    const = lambda i, e, bnd: (0, 0)
    tok = lambda w: pl.BlockSpec((MOE_TILE, w), lambda i, e, bnd: (i, 0))
    per_expert = lambda a, b: pl.BlockSpec((1, a, b), lambda i, e, bnd: (e, 0, 0))
    out = pl.pallas_call(
        kern,
        out_shape=jax.ShapeDtypeStruct((n_tok, d), F32),
        grid_spec=pltpu.PrefetchScalarGridSpec(
            num_scalar_prefetch=1,
            grid=(n_tok // MOE_TILE, N_EXPERTS),
            in_specs=[tok(d), tok(d), tok(LANES), tok(LANES),
                      pl.BlockSpec((1, N_EXPERTS, MOE_TILE), lambda i, e, bnd: (i, 0, 0)),
                      pl.BlockSpec(mod.shape, const),
                      per_expert(d, D_FF_EXPERT), per_expert(d, D_FF_EXPERT), per_expert(D_FF_EXPERT, d),
                      pl.BlockSpec(g_final.shape, const)],
            out_specs=tok(d)),
        compiler_params=pltpu.CompilerParams(dimension_semantics=("arbitrary", "arbitrary"),
                                             vmem_limit_bytes=VMEM_LIMIT),
        name="moe_final",
    )(counts, x.reshape(n_tok, d), h.reshape(n_tok, d), gates, rank, rank_rows, mod, wg, wu, wd, g_final)
    return out.reshape(bsz, n, d)


def _dt_lanes(v):
    gap = jnp.zeros(v.shape[:-1] + (DT_REP - v.shape[-1],), v.dtype)
    tail = jnp.zeros(v.shape[:-1] + (LANES - 3 * DT_REP,), v.dtype)
    return jnp.concatenate([v, gap, v, gap, v, gap, tail], axis=-1)


def _head_slots(w, per_head, take0, take1, slot=HEAD_SLOT):
    k = w.shape[0]
    w = w.reshape(k, MLA_HEADS, per_head)[:, :, take0:take1]
    w = jnp.pad(w, ((0, 0), (0, 0), (0, slot - (take1 - take0))))
    return w.reshape(k, MLA_HEADS * slot).astype(BF16)


def _q_slots(w):
    k = w.shape[0]
    w = w.reshape(k, MLA_HEADS, MLA_NOPE + MLA_ROPE)
    r = w[:, :, MLA_NOPE:]
    partner = jnp.concatenate([r[:, :, c0:c0 + ROPE_HALF] for c0 in ROPE_PARTNER_STARTS], axis=2)
    return jnp.concatenate([w, partner], axis=2).reshape(k, MLA_HEADS * HEAD_SLOT).astype(BF16)


def _rope_tables(n_latent, ctx_len):
    t = jnp.arange(n_latent)
    pos = jnp.stack([(t // GRID_W).astype(F32), (t % GRID_W).astype(F32)], axis=1)
    n_freq = MLA_ROPE // 4
    inv_freq = ROPE_THETA ** (-jnp.arange(n_freq, dtype=F32) / n_freq)
    ang = pos[:, :, None] * inv_freq
    cos, sin = jnp.cos(ang), jnp.sin(ang)
    cos_r = jnp.stack([cos, cos], axis=2).reshape(n_latent, MLA_ROPE)
    sin_r = jnp.stack([-sin, sin], axis=2).reshape(n_latent, MLA_ROPE)

    def slot(r, fill):
        rope = jnp.concatenate([jnp.full((ctx_len, MLA_ROPE), fill, F32), r], axis=0)
        rows = ctx_len + n_latent
        return jnp.concatenate([jnp.full((rows, ROPE_LANE0), fill, F32), rope, jnp.zeros((rows, MLA_ROPE), F32)],
                               axis=1)

    return slot(cos_r, 1.0), slot(sin_r, 0.0)


def kernel(x, c, ctx, c_ctx, w_mod, b_mod, g_mix, w_in, w_sgu, b_sgu, g_sgu, beta_sgu, conv_w, conv_b, dt_bias,
           a_log, d_skip, g_ssd, g_q, w_uq, g_kv, w_ukv, w_out, g_ffn, w_gate, w_up, w_down, w_router, w_gate_e,
           w_up_e, w_down_e, g_final):
    bsz, n, d = x.shape
    ctx_len = ctx.shape[1]
    depth = w_in.shape[0]
    assert bsz <= CTX_MOD_ROW and ctx_len % CHUNK == 0 and n % CHUNK == 0

    cond = jnp.concatenate([c, jnp.zeros((CTX_MOD_ROW - bsz, d), F32), c_ctx[None],
                            jnp.zeros((MOD_ROWS - CTX_MOD_ROW - 1, d), F32)], axis=0)
    mod_all = _mod_table(cond, w_mod, b_mod)
    cos, sin = _rope_tables(n, ctx_len)
    xa = (ctx, x)
    t_all = ctx_len + n
    rows_in = 768 if t_all % 768 == 0 else ctx_len

    out = None
    for layer in range(depth):
        last = layer == depth - 1
        mod = mod_all[layer]
        row = lambda v: v.reshape(1, -1)
        w_cat = jnp.transpose(w_sgu[layer], (1, 0, 2)).reshape(CHUNK, SGU_HEADS * CHUNK).astype(BF16)
        bias = jnp.repeat(b_sgu[layer].T, SGU_HEAD_DIM, axis=1)
        o_sgu, p_z, p_xbc, p_dt, p_q, p_kv, p_kr = _in_proj(
            xa, t_all, mod, row(g_mix[layer]), w_in, layer, w_cat, bias, row(g_sgu[layer]), row(beta_sgu[layer]),
            ctx_len, rows_in)
        o_ssd = _ssd(p_z, p_xbc, p_dt, conv_w[layer], row(conv_b[layer]), _dt_lanes(dt_bias[layer].reshape(1, -1)),
                     _dt_lanes(a_log[layer].reshape(1, -1)), row(jnp.repeat(d_skip[layer], SSD_HEAD_DIM)), row(g_ssd[layer]),
                     ctx_len)
        per_kv = MLA_NOPE + MLA_V
        q, k, v = _mla_proj(p_q, p_kv, p_kr, row(g_q[layer]), _q_slots(w_uq[layer]),
                            row(g_kv[layer]), _head_slots(w_ukv[layer], per_kv, 0, MLA_NOPE),
                            _head_slots(w_ukv[layer], per_kv, MLA_NOPE, per_kv, MLA_V).T, cos, sin, rows_in)
        o_mla = _attention(q, k, v, ctx_len, latent=True)
        o_mla_ctx = None if last else _attention(q, k, v, ctx_len, latent=False)
        wo = w_out[layer].astype(BF16)
        i = layer // 2
        if layer % 2 == 0:
            if last:
                raise NotImplementedError("final dense channel mixer")
            xa = _mix_ffn(xa, o_sgu, o_ssd, o_mla, o_mla_ctx, mod, wo, row(g_ffn[layer]),
                          w_gate[i].astype(BF16), w_up[i].astype(BF16), w_down[i].astype(BF16), ctx_len)
        else:
            if not last:
                raise NotImplementedError("expert channel mixer on a non-final layer")
            wr_hi = w_router[i].astype(BF16)
            wr_lo = (w_router[i] - wr_hi.astype(F32)).astype(BF16)
            pad = lambda w: jnp.pad(w, ((0, 0), (0, LANES - N_EXPERTS)))
            wr = jnp.concatenate([pad(wr_hi), pad(wr_lo)], axis=1)
            x_mid, h_ffn, gates = _out_proj_route(xa, o_sgu, o_ssd, o_mla, mod, wo, row(g_ffn[layer]), wr,
                                                  ctx_len)
            out = _moe_final(x_mid, h_ffn, gates, mod, w_gate_e[i].astype(BF16), w_up_e[i].astype(BF16),
                             w_down_e[i].astype(BF16), row(g_final))
    return out
```

```python
import functools

import jax
import jax.numpy as jnp
from jax import lax
from jax.experimental import pallas as pl
from jax.experimental.pallas import tpu as pltpu

F32 = jnp.float32
BF16 = jnp.bfloat16

D_MODEL = 1024
EPS = 1e-6
N_MOD = 6
GRID_W = 64
CHUNK = 128

SGU_HEADS = 4
SGU_HEAD_DIM = 64
SGU_WIDTH = 256

SSD_HEADS = 6
SSD_HEAD_DIM = 64
SSD_WIDTH = 384
SSD_GROUPS = 2
SSD_HPG = 3
SSD_STATE = 128
SSD_CONV = 5
SSD_CONV_CH = 896
SSD_GW = SSD_HPG * SSD_HEAD_DIM
DT_REP = 16

MLA_HEADS = 6
MLA_NOPE = 64
MLA_ROPE = 32
MLA_V = 64
MLA_WIDTH = 384
Q_LORA = 384
KV_LORA = 256
ROPE_THETA = 10000.0
MLA_SCALE = (MLA_NOPE + MLA_ROPE) ** -0.5
LOG2_E = 1.4426950408889634
HEAD_SLOT = 128
ROPE_LANE0 = MLA_NOPE
ROPE_HALF = MLA_ROPE // 4
ROPE_PARTNER_STARTS = (ROPE_HALF, 0, 3 * ROPE_HALF, 2 * ROPE_HALF)
assert HEAD_SLOT == ROPE_LANE0 + 2 * MLA_ROPE

D_FF = 2816
N_EXPERTS = 8
D_FF_EXPERT = 1408

LANES = 128
MOD_ROWS = 16
CTX_MOD_ROW = 8

IN_COLS = (("sgu", 0, 512), ("z", 512, 896), ("xbc", 896, 1792), ("dt", 1792, 1920), ("cq", 1920, 2304),
           ("ckv", 2304, 2560), ("kr", 2560, 2688))
IN_PAD_WIDTH = 2688

VMEM_LIMIT = 56 * 1024 * 1024


def _sigmoid(x):
    return 1.0 / (1.0 + jnp.exp(-x))


def _silu(x):
    return x * _sigmoid(x)


def _rms(x):
    return x * lax.rsqrt(jnp.mean(x * x, axis=-1, keepdims=True) + EPS)


def _aligned(v, m):
    return v if isinstance(v, int) else pl.multiple_of(v, m)


def _row_select(mod_ref, b, tile, rows, ctx_len, col0):
    row = tile * rows + lax.broadcasted_iota(jnp.int32, (rows, 1), 0)
    is_ctx = row < ctx_len
    mb = mod_ref[pl.ds(b, 1), col0:col0 + D_MODEL]
    mc = mod_ref[CTX_MOD_ROW:CTX_MOD_ROW + 1, col0:col0 + D_MODEL]
    return jnp.where(is_ctx, mc, mb)


def _mod_kernel(cond_ref, w_ref, b_ref, o_ref):
    s = _silu(cond_ref[...])
    s_hi = s.astype(BF16)
    s_lo = (s - s_hi.astype(F32)).astype(BF16)
    w = w_ref[0]
    w_hi = w.astype(BF16)
    w_lo = (w - w_hi.astype(F32)).astype(BF16)
    by_hi = jnp.dot(jnp.concatenate([s_hi, s_lo], axis=0), w_hi, preferred_element_type=F32)
    o_ref[0] = (by_hi[:MOD_ROWS] + by_hi[MOD_ROWS:] + jnp.dot(s_hi, w_lo, preferred_element_type=F32)
                + b_ref[0])


def _mod_table(cond, w_mod, b_mod):
    n_layers, d, width = w_mod.shape
    cb = 1536
    return pl.pallas_call(
        _mod_kernel,
        out_shape=jax.ShapeDtypeStruct((n_layers, MOD_ROWS, width), F32),
        grid=(n_layers, width // cb),
        in_specs=[pl.BlockSpec((MOD_ROWS, d), lambda l, j: (0, 0)),
                  pl.BlockSpec((1, d, cb), lambda l, j: (l, 0, j)),
                  pl.BlockSpec((1, 1, cb), lambda l, j: (l, 0, j))],
        out_specs=pl.BlockSpec((1, MOD_ROWS, cb), lambda l, j: (l, 0, j)),
        compiler_params=pltpu.CompilerParams(dimension_semantics=("arbitrary", "arbitrary"),
                                             vmem_limit_bytes=VMEM_LIMIT),
        name="mod_table",
    )(cond, w_mod, b_mod.reshape(n_layers, 1, width))


def _token_specs(tokens, ctx_len, rows, d):
    if not isinstance(tokens, tuple):
        return [pl.BlockSpec((1, rows, d), lambda b, t: (b, t, 0))], [tokens]
    ctx, lat = tokens
    n_sub = rows // ctx_len
    assert rows == n_sub * ctx_len
    piece = lambda j: pl.BlockSpec((1, ctx_len, d), lambda b, t: (b, jnp.maximum(n_sub * t + j - 1, 0), 0))
    return ([pl.BlockSpec((1, ctx_len, d), lambda b, t: (b, 0, 0))] + [piece(j) for j in range(n_sub)],
            [ctx] + [lat] * n_sub)


def _tokens_tile(refs, t):
    if len(refs) == 1:
        return refs[0][0]
    first = jnp.where(t == 0, refs[0][0], refs[1][0])
    return first if len(refs) == 2 else jnp.concatenate([first] + [r[0] for r in refs[2:]], axis=0)


IN_RAW_DT = (1792, 1804)
IN_RAW_KR = IN_RAW_DT[1] + Q_LORA + KV_LORA
IN_RAW_WIDTH = IN_RAW_KR + MLA_ROPE


def _pad_in_weight(wraw_ref, w_ref):
    dt0, dt1 = IN_RAW_DT
    for r0 in range(0, D_MODEL, LANES):
        rows = slice(r0, r0 + LANES)
        z = lambda n: jnp.zeros((LANES, n), F32)
        dt = wraw_ref[0, rows, dt0:dt1]
        gap = z(DT_REP - (dt1 - dt0))
        w_ref[rows, :] = jnp.concatenate(
            [wraw_ref[0, rows, 0:dt0], dt, gap, dt, gap, dt, gap, z(LANES - 3 * DT_REP),
             wraw_ref[0, rows, dt1:IN_RAW_KR], z(ROPE_LANE0), wraw_ref[0, rows, IN_RAW_KR:IN_RAW_WIDTH]]
            + [wraw_ref[0, rows, IN_RAW_KR + c0:IN_RAW_KR + c0 + ROPE_HALF] for c0 in ROPE_PARTNER_STARTS],
            axis=1).astype(BF16)


def _sgu_chunks(p_sgu, w_ref, bias_ref, g_ref, beta_ref, o_ref):
    lane = lax.broadcasted_iota(jnp.int32, (1, SGU_WIDTH), 1)
    head_of_lane = lane // SGU_HEAD_DIM
    w = w_ref[...]
    bias = bias_ref[...]
    c0 = 0.7978845608028654
    for c in range(p_sgu.shape[0] // CHUNK):
        p = p_sgu[c * CHUNK:(c + 1) * CHUNK, :]
        ge = 0.5 * p * (1.0 + jnp.tanh(c0 * (p + 0.044715 * (p * p * p))))
        u = ge[:, :SGU_WIDTH]
        v = ge[:, SGU_WIDTH:]
        mu = jnp.mean(v, axis=-1, keepdims=True)
        vc = v - mu
        vn = vc * lax.rsqrt(jnp.mean(vc * vc, axis=-1, keepdims=True) + EPS) * g_ref[...] + beta_ref[...]
        stacked = jnp.concatenate(
            [jnp.where(head_of_lane == h, vn, 0.0).astype(BF16) for h in range(SGU_HEADS)], axis=0)
        mixed = jnp.dot(w, stacked, preferred_element_type=F32) + bias
        o_ref[0, c * CHUNK:(c + 1) * CHUNK, :] = (u * mixed).astype(BF16)


def _in_kernel(*refs, rows, ctx_len, n_tok_refs):
    tok_refs = refs[:n_tok_refs]
    mod_ref, g_ref, wraw_ref, sgu_w_ref, sgu_bias_ref, sgu_g_ref, sgu_beta_ref, *out_refs, w_ref = refs[n_tok_refs:]
    b = pl.program_id(0)
    t = pl.program_id(1)

    @pl.when(jnp.logical_and(b == 0, t == 0))
    def _():
        _pad_in_weight(wraw_ref, w_ref)

    xn = _rms(_tokens_tile(tok_refs, t)) * g_ref[...]
    shift = _row_select(mod_ref, b, t, rows, ctx_len, 0)
    scale = _row_select(mod_ref, b, t, rows, ctx_len, D_MODEL)
    h = (xn * (1.0 + scale) + shift).astype(BF16)
    p = jnp.dot(h, w_ref[...], preferred_element_type=F32)
    (_, s0, s1), other_cols = IN_COLS[0], IN_COLS[1:]
    _sgu_chunks(p[:, s0:s1], sgu_w_ref, sgu_bias_ref, sgu_g_ref, sgu_beta_ref, out_refs[0])
    for o_ref, (_, c0, c1) in zip(out_refs[1:], other_cols):
        o_ref[0] = p[:, c0:c1]


def _in_proj(tokens, t_all, mod, g, w_in, layer, sgu_w, sgu_bias, sgu_g, sgu_beta, ctx_len, rows):
    tok_specs, tok_args = _token_specs(tokens, ctx_len, rows, D_MODEL)
    bsz = tok_args[0].shape[0]
    assert w_in.shape[1:] == (D_MODEL, IN_RAW_WIDTH) and rows % CHUNK == 0
    kern = functools.partial(_in_kernel, rows=rows, ctx_len=ctx_len, n_tok_refs=len(tok_args))
    const = lambda b, t: (0, 0)
    widths = [(SGU_WIDTH, BF16)] + [(c1 - c0, F32) for _, c0, c1 in IN_COLS[1:]]
    return pl.pallas_call(
        kern,
        out_shape=[jax.ShapeDtypeStruct((bsz, t_all, w), dt) for w, dt in widths],
        grid=(bsz, t_all // rows),
        in_specs=tok_specs + [pl.BlockSpec(mod.shape, const), pl.BlockSpec((1, D_MODEL), const),
                              pl.BlockSpec((1,) + w_in.shape[1:], lambda b, t: (layer, 0, 0)),
                              pl.BlockSpec(sgu_w.shape, const), pl.BlockSpec(sgu_bias.shape, const),
                              pl.BlockSpec(sgu_g.shape, const), pl.BlockSpec(sgu_beta.shape, const)],
        out_specs=[pl.BlockSpec((1, rows, w), lambda b, t: (b, t, 0)) for w, _ in widths],
        scratch_shapes=[pltpu.VMEM((D_MODEL, IN_PAD_WIDTH), BF16)],
        compiler_params=pltpu.CompilerParams(dimension_semantics=("arbitrary", "arbitrary"),
                                             vmem_limit_bytes=VMEM_LIMIT),
        name="in_proj",
    )(*tok_args, mod, g, w_in, sgu_w, sgu_bias, sgu_g, sgu_beta)


def _ssd_kernel(z_ref, xbc_ref, dt_ref, cw_ref, cb_ref, dtb_ref, alog_ref, skip_ref, g_ref, o_ref,
                xc_ref, yf_ref, yb_ref, st_ref, *, n_blk, n_ctx_blk):
    def conv_block(blk, seg_start, seg_end):
        r0 = _aligned(blk * CHUNK, CHUNK)
        for cg in range(SSD_CONV_CH // LANES):
            cols = slice(cg * LANES, (cg + 1) * LANES)
            zeros = jnp.zeros((8, LANES), F32)
            top = zeros if seg_start else xbc_ref[0, pl.ds(_aligned(r0 - 8, 8), 8), cols]
            bot = zeros if seg_end else xbc_ref[0, pl.ds(_aligned(r0 + CHUNK, 8), 8), cols]
            xw = jnp.concatenate([top, xbc_ref[0, pl.ds(r0, CHUNK), cols], bot], axis=0)
            acc = cb_ref[:, cols] + cw_ref[0:1, cols] * xw[6:6 + CHUNK]
            for k in range(1, SSD_CONV):
                acc = acc + cw_ref[k:k + 1, cols] * xw[6 + k:6 + k + CHUNK]
            xc_ref[pl.ds(r0, CHUNK), cols] = _silu(acc)

    n_interior = n_blk - n_ctx_blk - 2
    assert n_interior >= 0 and n_interior % 2 == 0
    for blk in sorted(set(list(range(n_ctx_blk)) + [n_ctx_blk, n_blk - 1])):
        conv_block(blk, blk == 0 or blk == n_ctx_blk, blk == n_ctx_blk - 1 or blk == n_blk - 1)

    li = lax.broadcasted_iota(jnp.int32, (CHUNK, CHUNK), 0)
    si = lax.broadcasted_iota(jnp.int32, (CHUNK, CHUNK), 1)
    a_neg = -jnp.exp(alog_ref[...])

    def chunk_step(c, direction):
        r0 = pl.multiple_of(c * CHUNK, CHUNK)
        rows = pl.ds(r0, CHUNK)
        mask = (si <= li) if direction == 0 else (si >= li)
        tri = jnp.where(mask, 1.0, 0.0).astype(BF16)
        dtr = dt_ref[0, rows, :] + dtb_ref[...]
        dt = jnp.maximum(dtr, 0.0) + jnp.log1p(jnp.exp(-jnp.abs(dtr)))
        adt = dt * a_neg
        p1 = adt.astype(BF16)
        r1 = adt - p1.astype(F32)
        p2 = r1.astype(BF16)
        p3 = (r1 - p2.astype(F32)).astype(BF16)
        parts = jnp.dot(tri, jnp.concatenate([p1, p2, p3], axis=1), preferred_element_type=F32)
        acs = parts[:, :LANES] + parts[:, LANES:2 * LANES] + parts[:, 2 * LANES:]
        end = CHUNK - 1 if direction == 0 else 0
        tot = acs[end:end + 1, :]
        to_end_dt = jnp.exp(tot - acs) * dt
        chunk_decay = jnp.exp(tot)
        per_head_rows = jnp.where(si < DT_REP, acs, jnp.where(si < 2 * DT_REP, dt, to_end_dt)).T
        first_half = si < SSD_HEAD_DIM
        cbs, bm_ts, y_offs = [], [], []
        state = st_ref[direction]
        state16 = state.astype(BF16)
        for g in range(SSD_GROUPS):
            bm = xc_ref[rows, SSD_WIDTH + g * SSD_STATE:SSD_WIDTH + (g + 1) * SSD_STATE]
            cm = xc_ref[rows, SSD_WIDTH + SSD_GROUPS * SSD_STATE + g * SSD_STATE:
                        SSD_WIDTH + SSD_GROUPS * SSD_STATE + (g + 1) * SSD_STATE]
            cm16 = cm.astype(BF16)
            cbs.append(lax.dot_general(cm16, bm.astype(BF16), (((1,), (1,)), ((), ())),
                                       preferred_element_type=F32))
            bm_ts.append(bm.T)
            y_offs.append(jnp.dot(cm16, state16, preferred_element_type=F32))
        y_pairs, state_pairs = [], []
        for k in range(SSD_HEADS // 2):
            pair = slice(k * LANES, (k + 1) * LANES)
            xs16 = xc_ref[rows, pair].astype(BF16)
            y_diag, contrib, exp_a, dec, y_off = [], [], [], [], []
            for hh in (2 * k, 2 * k + 1):
                g = hh // SSD_HPG
                col = direction * SSD_HEADS + hh
                a_col = jnp.broadcast_to(acs[:, col:col + 1], (CHUNK, CHUNK))
                seg = a_col - per_head_rows[col:col + 1, :]
                decay = jnp.exp(jnp.where(mask, seg, -1e30))
                m = (cbs[g] * decay * per_head_rows[DT_REP + col:DT_REP + col + 1, :]).astype(BF16)
                y_diag.append(jnp.dot(m, xs16, preferred_element_type=F32))
                exp_a.append(jnp.exp(a_col))
                y_off.append(y_offs[g][:, pair])
                lhs = (bm_ts[g] * per_head_rows[2 * DT_REP + col:2 * DT_REP + col + 1, :]).astype(BF16)
                contrib.append(jnp.dot(lhs, xs16, preferred_element_type=F32))
                dec.append(jnp.broadcast_to(chunk_decay[:, col:col + 1], (SSD_STATE, LANES)))
            pick = lambda ab: jnp.where(first_half, ab[0], ab[1])
            y_pairs.append(pick(y_diag) + pick(exp_a) * pick(y_off))
            state_pairs.append(pick(dec) * state[:, pair] + pick(contrib))
        st_ref[direction] = jnp.concatenate(state_pairs, axis=1)
        y = jnp.concatenate(y_pairs, axis=1)
        if direction == 0:
            yf_ref[rows, :] = skip_ref[...] * xc_ref[rows, 0:SSD_WIDTH] + y
        else:
            yb_ref[rows, :] = y

    def scan_body(step, carry):
        chunk_step(step, 0)
        chunk_step(jnp.where(step < n_ctx_blk, n_ctx_blk - 1 - step, n_blk - 1 - (step - n_ctx_blk)), 1)
        return carry

    def scan_conv_body(step, carry):
        scan_body(step, carry)
        conv_block(step + 1, False, False)
        conv_block(n_blk - 2 - (step - n_ctx_blk), False, False)
        return carry

    def out_chunk(c):
        rows = pl.ds(_aligned(c * CHUNK, CHUNK), CHUNK)
        gated = (yf_ref[rows, :] + yb_ref[rows, :]) * _silu(z_ref[0, rows, :])
        o_ref[0, rows, :] = (_rms(gated) * g_ref[...]).astype(BF16)

    def scan_out_body(step, carry):
        scan_body(step, carry)
        out_chunk(step - 1)
        out_chunk(n_blk - 1 - (step - 1 - n_ctx_blk))
        return carry

    st_ref[...] = jnp.zeros(st_ref.shape, F32)
    conv_steps_end = n_ctx_blk + n_interior // 2
    meet = (n_blk + n_ctx_blk) // 2
    lax.fori_loop(0, n_ctx_blk, scan_body, 0)
    lax.fori_loop(n_ctx_blk, conv_steps_end, scan_conv_body, 0)
    lax.fori_loop(conv_steps_end, meet + 1, scan_body, 0)
    lax.fori_loop(meet + 1, n_blk, scan_out_body, 0)
    for c in list(range(n_ctx_blk)) + [n_ctx_blk, n_blk - 1]:
        out_chunk(c)


def _ssd(p_z, p_xbc, p_dt, conv_w, conv_b, dt_bias, a_log, skip, g, ctx_len):
    bsz, t_all, _ = p_z.shape
    kern = functools.partial(_ssd_kernel, n_blk=t_all // CHUNK, n_ctx_blk=ctx_len // CHUNK)
    const = lambda b: (0, 0)
    per_b = lambda w: pl.BlockSpec((1, t_all, w), lambda b: (b, 0, 0))
    return pl.pallas_call(
        kern,
        out_shape=jax.ShapeDtypeStruct((bsz, t_all, SSD_WIDTH), BF16),
        grid=(bsz,),
        in_specs=[per_b(SSD_WIDTH), per_b(SSD_CONV_CH), per_b(LANES),
                  pl.BlockSpec(conv_w.shape, const), pl.BlockSpec(conv_b.shape, const),
                  pl.BlockSpec(dt_bias.shape, const), pl.BlockSpec(a_log.shape, const),
                  pl.BlockSpec(skip.shape, const), pl.BlockSpec(g.shape, const)],
        out_specs=per_b(SSD_WIDTH),
        scratch_shapes=[pltpu.VMEM((t_all, SSD_CONV_CH), F32), pltpu.VMEM((t_all, SSD_WIDTH), F32),
                        pltpu.VMEM((t_all, SSD_WIDTH), F32),
                        pltpu.VMEM((2, SSD_STATE, SSD_WIDTH), F32)],
        compiler_params=pltpu.CompilerParams(dimension_semantics=("arbitrary",), vmem_limit_bytes=VMEM_LIMIT),
        name="ssd",
    )(p_z, p_xbc, p_dt, conv_w, conv_b, dt_bias, a_log, skip, g)


def _mla_proj_kernel(pq_ref, pkv_ref, pkr_ref, gq_ref, wq_ref, gkv_ref, wk_ref, wv_ref, cos_ref, sin_ref,
                     q_ref, k_ref, v_ref):
    cos = cos_ref[...]
    sin = sin_ref[...]

    def rope(t):
        return t * cos + pltpu.roll(t, LANES - MLA_ROPE, 1) * sin

    qn = (_rms(pq_ref[0]) * gq_ref[...]).astype(BF16)
    q = jnp.dot(qn, wq_ref[...], preferred_element_type=F32)
    kvn = (_rms(pkv_ref[0]) * gkv_ref[...]).astype(BF16)
    k = jnp.dot(kvn, wk_ref[...], preferred_element_type=F32)
    v_t = lax.dot_general(wv_ref[...], kvn, (((1,), (1,)), ((), ())),
                          preferred_element_type=F32).astype(BF16)
    kr = rope(pkr_ref[0])
    for h in range(MLA_HEADS):
        slot = slice(h * HEAD_SLOT, (h + 1) * HEAD_SLOT)
        q_ref[0, h] = (rope(q[:, slot]) * (MLA_SCALE * LOG2_E)).astype(BF16)
        k_ref[0, h] = (k[:, slot] + kr).astype(BF16)
        v_ref[0, h] = v_t[h * MLA_V:(h + 1) * MLA_V, :]


def _mla_proj(p_q, p_kv, p_kr, g_q, w_q, g_kv, w_k, w_v, cos, sin, rows):
    bsz, t_all, _ = p_q.shape
    const = lambda b, t: (0, 0)
    tok = lambda w: pl.BlockSpec((1, rows, w), lambda b, t: (b, t, 0))
    tab = pl.BlockSpec((rows, LANES), lambda b, t: (t, 0))
    out = jax.ShapeDtypeStruct((bsz, MLA_HEADS, t_all, HEAD_SLOT), BF16)
    out_vt = jax.ShapeDtypeStruct((bsz, MLA_HEADS, MLA_V, t_all), BF16)
    head_tok = pl.BlockSpec((1, MLA_HEADS, rows, HEAD_SLOT), lambda b, t: (b, 0, t, 0))
    return pl.pallas_call(
        _mla_proj_kernel,
        out_shape=[out, out, out_vt],
        grid=(bsz, t_all // rows),
        in_specs=[tok(Q_LORA), tok(KV_LORA), tok(LANES),
                  pl.BlockSpec(g_q.shape, const), pl.BlockSpec(w_q.shape, const),
                  pl.BlockSpec(g_kv.shape, const), pl.BlockSpec(w_k.shape, const), pl.BlockSpec(w_v.shape, const),
                  tab, tab],
        out_specs=[head_tok, head_tok, pl.BlockSpec((1, MLA_HEADS, MLA_V, rows), lambda b, t: (b, 0, 0, t))],
        compiler_params=pltpu.CompilerParams(dimension_semantics=("arbitrary", "arbitrary"),
                                             vmem_limit_bytes=VMEM_LIMIT),
        name="mla_proj",
    )(p_q, p_kv, p_kr, g_q, w_q, g_kv, w_k, w_v, cos, sin)


ATTN_Q = 512
ATTN_BUFS = 2
ATTN_PARTS = 3


def _attn_kernel(*refs, n_q_blocks):
    q_refs = refs[:n_q_blocks]
    k_ref, vt_ref, o_ref, *bufs, ot_ref = refs[n_q_blocks:]
    s_bufs, m_bufs = bufs[:ATTN_BUFS], bufs[ATTN_BUFS:]
    def scores(h, s_ref, m_ref):
        qh = jnp.concatenate([r[0, h] for r in q_refs], axis=0)
        s_t = lax.dot_general(k_ref[0, h], qh, (((1,), (1,)), ((), ())), preferred_element_type=F32)
        s_ref[...] = s_t
        m_ref[...] = jnp.broadcast_to(jnp.max(s_t, axis=0, keepdims=True), m_ref.shape)

    scores(0, s_bufs[0], m_bufs[0])

    n_keys = k_ref.shape[2]
    n_parts = ATTN_PARTS if n_keys % (ATTN_PARTS * LANES) == 0 else 1
    part = n_keys // n_parts

    def stage(h_new, h_old, new_s, new_m, old_s, old_m):
        qh = jnp.concatenate([r[0, h_new] for r in q_refs], axis=0)
        m_old = old_m[0:1, :]
        m_new, denom, acc = None, None, None
        for c in range(n_parts):
            keys = slice(c * part, (c + 1) * part)
            s_c = lax.dot_general(k_ref[0, h_new, keys, :], qh, (((1,), (1,)), ((), ())),
                                  preferred_element_type=F32)
            new_s[keys, :] = s_c
            m_c = jnp.max(s_c, axis=0, keepdims=True)
            m_new = m_c if m_new is None else jnp.maximum(m_new, m_c)
            p_c = jnp.exp2(old_s[keys, :] - m_old)
            d_c = jnp.sum(p_c, axis=0, keepdims=True)
            o_c = jnp.dot(vt_ref[0, h_old, :, keys], p_c.astype(BF16), preferred_element_type=F32)
            denom, acc = (d_c, o_c) if denom is None else (denom + d_c, acc + o_c)
        new_m[...] = jnp.broadcast_to(m_new, new_m.shape)
        ot_ref[pl.ds(pl.multiple_of(h_old * MLA_V, MLA_V), MLA_V), :] = acc / denom

    def head_group(j, carry):
        h0 = ATTN_BUFS * j
        for i in range(ATTN_BUFS):
            nxt = (i + 1) % ATTN_BUFS
            stage(jnp.minimum(h0 + i + 1, MLA_HEADS - 1), h0 + i, s_bufs[nxt], m_bufs[nxt], s_bufs[i], m_bufs[i])
        return carry

    lax.fori_loop(0, MLA_HEADS // ATTN_BUFS, head_group, 0)
    o_ref[0] = ot_ref[...].T.astype(BF16)


def _attention(q, k, vt, ctx_len, latent):
    bsz, n_heads, t_all, width = q.shape
    if latent:
        n_q_blocks = ATTN_Q // ctx_len
        n_rows, n_keys, tq = t_all - ctx_len, t_all, ATTN_Q
        q_specs = [pl.BlockSpec((1, n_heads, ctx_len, width),
                                functools.partial(lambda b, i, j: (b, 0, n_q_blocks * i + 1 + j, 0), j=j))
                   for j in range(n_q_blocks)]
    else:
        n_q_blocks = 1
        n_rows, n_keys, tq = ctx_len, ctx_len, ctx_len
        q_specs = [pl.BlockSpec((1, n_heads, ctx_len, width), lambda b, i: (b, 0, 0, 0))]
    assert n_rows % tq == 0
    return pl.pallas_call(
        functools.partial(_attn_kernel, n_q_blocks=n_q_blocks),
        out_shape=jax.ShapeDtypeStruct((bsz, n_rows, MLA_WIDTH), BF16),
        grid=(bsz, n_rows // tq),
        in_specs=q_specs + [pl.BlockSpec((1, n_heads, n_keys, width), lambda b, i: (b, 0, 0, 0)),
                            pl.BlockSpec((1, n_heads, MLA_V, n_keys), lambda b, i: (b, 0, 0, 0))],
        out_specs=pl.BlockSpec((1, tq, MLA_WIDTH), lambda b, i: (b, i, 0)),
        scratch_shapes=([pltpu.VMEM((n_keys, tq), F32)] * ATTN_BUFS + [pltpu.VMEM((8, tq), F32)] * ATTN_BUFS
                        + [pltpu.VMEM((MLA_WIDTH, tq), F32)]),
        compiler_params=pltpu.CompilerParams(dimension_semantics=("arbitrary", "arbitrary"),
                                             vmem_limit_bytes=VMEM_LIMIT),
        name="attention",
    )(*([q] * n_q_blocks), k, vt)


def _mix_residual(x_in, o1, o2, o3, mod_ref, wo_ref, g_ref, b, t, rows, ctx_len):
    r1, r2 = SGU_WIDTH, SGU_WIDTH + SSD_WIDTH
    mix = (jnp.dot(o1, wo_ref[0:r1, :], preferred_element_type=F32)
           + jnp.dot(o2, wo_ref[r1:r2, :], preferred_element_type=F32)
           + jnp.dot(o3, wo_ref[r2:, :], preferred_element_type=F32))
    x = x_in + _row_select(mod_ref, b, t, rows, ctx_len, 2 * D_MODEL) * mix
    shift = _row_select(mod_ref, b, t, rows, ctx_len, 3 * D_MODEL)
    scale = _row_select(mod_ref, b, t, rows, ctx_len, 4 * D_MODEL)
    return x, _rms(x) * g_ref[...] * (1.0 + scale) + shift


OUT_ROWS = 512


def _out_kernel(*refs, rows, ctx_len, n_sub):
    x_refs, o1_refs, o2_refs = refs[:n_sub], refs[n_sub:2 * n_sub], refs[2 * n_sub:3 * n_sub]
    o3_ref, mod_ref, wo_ref, g_ref, wr_ref, xo_ref, h_ref, gate_ref = refs[3 * n_sub:]
    pieces = lambda rs: rs[0][0] if n_sub == 1 else jnp.concatenate([r[0] for r in rs], axis=0)
    b = pl.program_id(0)
    t = pl.program_id(1) + 1
    x, h = _mix_residual(pieces(x_refs), pieces(o1_refs), pieces(o2_refs), o3_ref[0], mod_ref, wo_ref, g_ref, b, t,
                         rows, ctx_len)
    xo_ref[0] = x
    h_ref[0] = h.astype(BF16)
    h_hi = h.astype(BF16)
    h_lo = (h - h_hi.astype(F32)).astype(BF16)
    by_hi = jnp.dot(h_hi, wr_ref[...], preferred_element_type=F32)
    logits = by_hi[:, :LANES] + by_hi[:, LANES:] + jnp.dot(h_lo, wr_ref[:, :LANES], preferred_element_type=F32)
    lane = lax.broadcasted_iota(jnp.int32, logits.shape, 1)
    lane_f = lane.astype(F32)
    lg = jnp.where(lane < N_EXPERTS, logits, -jnp.inf)
    m1 = jnp.max(lg, axis=-1, keepdims=True)
    i1 = jnp.min(jnp.where(lg == m1, lane_f, float(LANES)), axis=-1, keepdims=True)
    lg2 = jnp.where(lane_f == i1, -jnp.inf, lg)
    m2 = jnp.max(lg2, axis=-1, keepdims=True)
    i2 = jnp.min(jnp.where(lg2 == m2, lane_f, float(LANES)), axis=-1, keepdims=True)
    e2 = jnp.exp(m2 - m1)
    w_top = 1.0 / (1.0 + e2)
    gate_ref[0] = jnp.where(lane_f == i1, w_top, 0.0) + jnp.where(lane_f == i2, e2 * w_top, 0.0)


def _out_proj_route(xa, o_sgu, o_ssd, o_mla, mod, wo, g, w_router, ctx_len):
    bsz, t_all, d = xa.shape
    n_out = t_all - ctx_len
    rows = OUT_ROWS if n_out % OUT_ROWS == 0 and OUT_ROWS % ctx_len == 0 else ctx_len
    n_sub = rows // ctx_len
    kern = functools.partial(_out_kernel, rows=rows, ctx_len=ctx_len, n_sub=n_sub)
    const = lambda b, t: (0, 0)
    piece = lambda w, j: pl.BlockSpec((1, ctx_len, w), lambda b, t: (b, n_sub * t + 1 + j, 0))
    tok = lambda w: [piece(w, j) for j in range(n_sub)]
    out_tok = lambda w: pl.BlockSpec((1, rows, w), lambda b, t: (b, t, 0))
    return pl.pallas_call(
        kern,
        out_shape=[jax.ShapeDtypeStruct((bsz, n_out, d), F32), jax.ShapeDtypeStruct((bsz, n_out, d), BF16),
                   jax.ShapeDtypeStruct((bsz, n_out, LANES), F32)],
        grid=(bsz, n_out // rows),
        in_specs=tok(d) + tok(SGU_WIDTH) + tok(SSD_WIDTH) + [
            out_tok(MLA_WIDTH), pl.BlockSpec(mod.shape, const), pl.BlockSpec(wo.shape, const),
            pl.BlockSpec(g.shape, const), pl.BlockSpec(w_router.shape, const)],
        out_specs=[out_tok(d), out_tok(d), out_tok(LANES)],
        compiler_params=pltpu.CompilerParams(dimension_semantics=("arbitrary", "arbitrary"),
                                             vmem_limit_bytes=VMEM_LIMIT),
        name="out_proj",
    )(*([xa] * n_sub + [o_sgu] * n_sub + [o_ssd] * n_sub), o_mla, mod, wo, g, w_router)


FF_SPLITS = ((0, 1536), (1536, D_FF))


def _mix_ffn_kernel(*refs, rows, ctx_len, n_tok_refs):
    tok_refs = refs[:n_tok_refs]
    (o1_ref, o2_ref, o3_ref, o3c_ref, mod_ref, wo_ref, g_ref, wg_ref, wu_ref, wd_ref,
     o_ref) = refs[n_tok_refs:]
    b = pl.program_id(0)
    t = pl.program_id(1)
    o3 = jnp.where(t == 0, o3c_ref[0], o3_ref[0])
    x, h = _mix_residual(_tokens_tile(tok_refs, t), o1_ref[0], o2_ref[0], o3, mod_ref, wo_ref, g_ref, b, t, rows,
                         ctx_len)
    h = h.astype(BF16)
    acc = None
    for c0, c1 in FF_SPLITS:
        gate = jnp.dot(h, wg_ref[:, c0:c1], preferred_element_type=F32)
        up = jnp.dot(h, wu_ref[:, c0:c1], preferred_element_type=F32)
        part = jnp.dot((_silu(gate) * up).astype(BF16), wd_ref[c0:c1, :], preferred_element_type=F32)
        acc = part if acc is None else acc + part
    o_ref[0] = x + _row_select(mod_ref, b, t, rows, ctx_len, 5 * D_MODEL) * acc


def _mix_ffn(tokens, o_sgu, o_ssd, o_mla, o_mla_ctx, mod, wo, g, wg, wu, wd, ctx_len):
    bsz, t_all, _ = o_sgu.shape
    d = D_MODEL
    rows = ctx_len
    tok_specs, tok_args = _token_specs(tokens, ctx_len, rows, d)
    kern = functools.partial(_mix_ffn_kernel, rows=rows, ctx_len=ctx_len, n_tok_refs=len(tok_args))
    const = lambda b, t: (0, 0)
    tok = lambda w: pl.BlockSpec((1, rows, w), lambda b, t: (b, t, 0))
    whole = lambda a: pl.BlockSpec(a.shape, const)
    return pl.pallas_call(
        kern,
        out_shape=jax.ShapeDtypeStruct((bsz, t_all, d), F32),
        grid=(bsz, t_all // rows),
        in_specs=tok_specs + [tok(SGU_WIDTH), tok(SSD_WIDTH),
                              pl.BlockSpec((1, rows, MLA_WIDTH), lambda b, t: (b, jnp.maximum(t - 1, 0), 0)),
                              pl.BlockSpec((1, rows, MLA_WIDTH), lambda b, t: (b, 0, 0)),
                              whole(mod), whole(wo), whole(g), whole(wg), whole(wu), whole(wd)],
        out_specs=tok(d),
        compiler_params=pltpu.CompilerParams(dimension_semantics=("arbitrary", "arbitrary"),
                                             vmem_limit_bytes=VMEM_LIMIT),
        name="mix_ffn",
    )(*tok_args, o_sgu, o_ssd, o_mla, o_mla_ctx, mod, wo, g, wg, wu, wd)


MOE_TILE = 1024
MOE_BLOCK = 144
MOE_SLOTS = 256


def _plan_kernel(gate_ref, rank_ref, rank_rows_ref, count_ref):
    rows = gate_ref.shape[0]
    routed = gate_ref[...] > 0.0
    ti = lax.broadcasted_iota(jnp.int32, (rows, rows), 0)
    tj = lax.broadcasted_iota(jnp.int32, (rows, rows), 1)
    earlier = jnp.where(tj < ti, 1.0, 0.0).astype(BF16)
    ones = jnp.where(routed, 1.0, 0.0)
    before = jnp.dot(earlier, ones.astype(BF16), preferred_element_type=F32)
    rank = jnp.where(routed, before, -1.0)
    rank_ref[...] = rank
    rank_rows_ref[0] = rank.T[0:N_EXPERTS, :]
    count_ref[0] = jnp.broadcast_to(jnp.sum(ones, axis=0, keepdims=True), (8, LANES)).astype(jnp.int32)


def _route_plan(gates):
    n_tok = gates.shape[0]
    n_tiles = n_tok // MOE_TILE
    return pl.pallas_call(
        _plan_kernel,
        out_shape=[jax.ShapeDtypeStruct((n_tok, LANES), F32),
                   jax.ShapeDtypeStruct((n_tiles, N_EXPERTS, MOE_TILE), F32),
                   jax.ShapeDtypeStruct((n_tiles, 8, LANES), jnp.int32)],
        grid=(n_tiles,),
        in_specs=[pl.BlockSpec((MOE_TILE, LANES), lambda i: (i, 0))],
        out_specs=[pl.BlockSpec((MOE_TILE, LANES), lambda i: (i, 0)),
                   pl.BlockSpec((1, N_EXPERTS, MOE_TILE), lambda i: (i, 0, 0)),
                   pl.BlockSpec((1, 8, LANES), lambda i: (i, 0, 0))],
        compiler_params=pltpu.CompilerParams(dimension_semantics=("arbitrary",), vmem_limit_bytes=VMEM_LIMIT),
        name="route_plan",
    )(gates)


def _moe_kernel(count_ref, x_ref, h_ref, gate_ref, rank_ref, rank_rows_ref, mod_ref, wg_ref, wu_ref, wd_ref, gf_ref,
                o_ref, *, tiles_per_batch):
    i = pl.program_id(0)
    e = pl.program_id(1)
    rows = h_ref.shape[0]
    d = h_ref.shape[1]
    lane = lax.broadcasted_iota(jnp.int32, (rows, LANES), 1)
    ge = jnp.sum(jnp.where(lane == e, gate_ref[...], 0.0), axis=-1, keepdims=True)
    rank_lanes = jnp.broadcast_to(jnp.sum(jnp.where(lane == e, rank_ref[...], 0.0), axis=-1, keepdims=True),
                                  (rows, MOE_SLOTS))
    rank_row = rank_rows_ref[0, pl.ds(e, 1), :]

    @pl.when(e == 0)
    def _():
        o_ref[...] = jnp.zeros(o_ref.shape, F32)

    slot_sub = lax.broadcasted_iota(jnp.int32, (MOE_BLOCK, rows), 0)
    slot_lane = lax.broadcasted_iota(jnp.int32, (1, MOE_SLOTS), 1)

    def block(j, carry):
        lo = j * MOE_BLOCK
        pick = jnp.where(rank_row == (slot_sub + lo).astype(F32), 1.0, 0.0).astype(BF16)
        hg = jnp.dot(pick, h_ref[...], preferred_element_type=F32).astype(BF16)
        gate = jnp.dot(hg, wg_ref[0], preferred_element_type=F32)
        up = jnp.dot(hg, wu_ref[0], preferred_element_type=F32)
        y = jnp.dot((_silu(gate) * up).astype(BF16), wd_ref[0], preferred_element_type=F32).astype(BF16)
        y = jnp.concatenate([y, jnp.zeros((MOE_SLOTS - MOE_BLOCK, d), BF16)], axis=0)
        target = jnp.where(slot_lane < MOE_BLOCK, slot_lane + lo, -2).astype(F32)
        place = jnp.where(rank_lanes == target, 1.0, 0.0).astype(BF16)
        o_ref[...] += ge * jnp.dot(place, y, preferred_element_type=F32)
        return carry

    n_blocks = (count_ref[i * N_EXPERTS + e] + (MOE_BLOCK - 1)) // MOE_BLOCK
    lax.fori_loop(0, n_blocks, block, 0)

    @pl.when(e == N_EXPERTS - 1)
    def _():
        gate5 = mod_ref[pl.ds(i // tiles_per_batch, 1), 5 * D_MODEL:6 * D_MODEL]
        o_ref[...] = _rms(x_ref[...] + gate5 * o_ref[...]) * gf_ref[...]


def _moe_final(x, h, gates, mod, wg, wu, wd, g_final):
    bsz, n, d = x.shape
    assert n % MOE_TILE == 0
    n_tok = bsz * n
    gates = gates.reshape(n_tok, LANES)
    rank, rank_rows, counts = _route_plan(gates)
    counts = counts[:, 0, :N_EXPERTS].reshape(-1)
    kern = functools.partial(_moe_kernel, tiles_per_batch=n // MOE_TILE)
    const = lambda i, e, bnd: (0, 0)
    tok = lambda w: pl.BlockSpec((MOE_TILE, w), lambda i, e, bnd: (i, 0))
    per_expert = lambda a, b: pl.BlockSpec((1, a, b), lambda i, e, bnd: (e, 0, 0))
    out = pl.pallas_call(
        kern,
        out_shape=jax.ShapeDtypeStruct((n_tok, d), F32),
        grid_spec=pltpu.PrefetchScalarGridSpec(
            num_scalar_prefetch=1,
            grid=(n_tok // MOE_TILE, N_EXPERTS),
            in_specs=[tok(d), tok(d), tok(LANES), tok(LANES),
                      pl.BlockSpec((1, N_EXPERTS, MOE_TILE), lambda i, e, bnd: (i, 0, 0)),
                      pl.BlockSpec(mod.shape, const),
                      per_expert(d, D_FF_EXPERT), per_expert(d, D_FF_EXPERT), per_expert(D_FF_EXPERT, d),
                      pl.BlockSpec(g_final.shape, const)],
            out_specs=tok(d)),
        compiler_params=pltpu.CompilerParams(dimension_semantics=("arbitrary", "arbitrary"),
                                             vmem_limit_bytes=VMEM_LIMIT),
        name="moe_final",
    )(counts, x.reshape(n_tok, d), h.reshape(n_tok, d), gates, rank, rank_rows, mod, wg, wu, wd, g_final)
    return out.reshape(bsz, n, d)


def _dt_lanes(v):
    gap = jnp.zeros(v.shape[:-1] + (DT_REP - v.shape[-1],), v.dtype)
    tail = jnp.zeros(v.shape[:-1] + (LANES - 3 * DT_REP,), v.dtype)
    return jnp.concatenate([v, gap, v, gap, v, gap, tail], axis=-1)


def _head_slots(w, per_head, take0, take1, slot=HEAD_SLOT):
    k = w.shape[0]
    w = w.reshape(k, MLA_HEADS, per_head)[:, :, take0:take1]
    w = jnp.pad(w, ((0, 0), (0, 0), (0, slot - (take1 - take0))))
    return w.reshape(k, MLA_HEADS * slot).astype(BF16)


def _q_slots(w):
    k = w.shape[0]
    w = w.reshape(k, MLA_HEADS, MLA_NOPE + MLA_ROPE)
    r = w[:, :, MLA_NOPE:]
    partner = jnp.concatenate([r[:, :, c0:c0 + ROPE_HALF] for c0 in ROPE_PARTNER_STARTS], axis=2)
    return jnp.concatenate([w, partner], axis=2).reshape(k, MLA_HEADS * HEAD_SLOT).astype(BF16)


def _rope_tables(n_latent, ctx_len):
    t = jnp.arange(n_latent)
    pos = jnp.stack([(t // GRID_W).astype(F32), (t % GRID_W).astype(F32)], axis=1)
    n_freq = MLA_ROPE // 4
    inv_freq = ROPE_THETA ** (-jnp.arange(n_freq, dtype=F32) / n_freq)
    ang = pos[:, :, None] * inv_freq
    cos, sin = jnp.cos(ang), jnp.sin(ang)
    cos_r = jnp.stack([cos, cos], axis=2).reshape(n_latent, MLA_ROPE)
    sin_r = jnp.stack([-sin, sin], axis=2).reshape(n_latent, MLA_ROPE)

    def slot(r, fill):
        rope = jnp.concatenate([jnp.full((ctx_len, MLA_ROPE), fill, F32), r], axis=0)
        rows = ctx_len + n_latent
        return jnp.concatenate([jnp.full((rows, ROPE_LANE0), fill, F32), rope, jnp.zeros((rows, MLA_ROPE), F32)],
                               axis=1)

    return slot(cos_r, 1.0), slot(sin_r, 0.0)


def kernel(x, c, ctx, c_ctx, w_mod, b_mod, g_mix, w_in, w_sgu, b_sgu, g_sgu, beta_sgu, conv_w, conv_b, dt_bias,
           a_log, d_skip, g_ssd, g_q, w_uq, g_kv, w_ukv, w_out, g_ffn, w_gate, w_up, w_down, w_router, w_gate_e,
           w_up_e, w_down_e, g_final):
    bsz, n, d = x.shape
    ctx_len = ctx.shape[1]
    depth = w_in.shape[0]
    assert bsz <= CTX_MOD_ROW and ctx_len % CHUNK == 0 and n % CHUNK == 0

    cond = jnp.concatenate([c, jnp.zeros((CTX_MOD_ROW - bsz, d), F32), c_ctx[None],
                            jnp.zeros((MOD_ROWS - CTX_MOD_ROW - 1, d), F32)], axis=0)
    mod_all = _mod_table(cond, w_mod, b_mod)
    cos, sin = _rope_tables(n, ctx_len)
    xa = (ctx, x)
    t_all = ctx_len + n
    rows_in = 768 if t_all % 768 == 0 else ctx_len

    out = None
    for layer in range(depth):
        last = layer == depth - 1
        mod = mod_all[layer]
        row = lambda v: v.reshape(1, -1)
        w_cat = jnp.transpose(w_sgu[layer], (1, 0, 2)).reshape(CHUNK, SGU_HEADS * CHUNK).astype(BF16)
        bias = jnp.repeat(b_sgu[layer].T, SGU_HEAD_DIM, axis=1)
        o_sgu, p_z, p_xbc, p_dt, p_q, p_kv, p_kr = _in_proj(
            xa, t_all, mod, row(g_mix[layer]), w_in, layer, w_cat, bias, row(g_sgu[layer]), row(beta_sgu[layer]),
            ctx_len, rows_in)
        o_ssd = _ssd(p_z, p_xbc, p_dt, conv_w[layer], row(conv_b[layer]), _dt_lanes(dt_bias[layer].reshape(1, -1)),
                     _dt_lanes(a_log[layer].reshape(1, -1)), row(jnp.repeat(d_skip[layer], SSD_HEAD_DIM)), row(g_ssd[layer]),
                     ctx_len)
        per_kv = MLA_NOPE + MLA_V
        q, k, v = _mla_proj(p_q, p_kv, p_kr, row(g_q[layer]), _q_slots(w_uq[layer]),
                            row(g_kv[layer]), _head_slots(w_ukv[layer], per_kv, 0, MLA_NOPE),
                            _head_slots(w_ukv[layer], per_kv, MLA_NOPE, per_kv, MLA_V).T, cos, sin, rows_in)
        o_mla = _attention(q, k, v, ctx_len, latent=True)
        o_mla_ctx = None if last else _attention(q, k, v, ctx_len, latent=False)
        wo = w_out[layer].astype(BF16)
        i = layer // 2
        if layer % 2 == 0:
            if last:
                raise NotImplementedError("final dense channel mixer")
            xa = _mix_ffn(xa, o_sgu, o_ssd, o_mla, o_mla_ctx, mod, wo, row(g_ffn[layer]),
                          w_gate[i].astype(BF16), w_up[i].astype(BF16), w_down[i].astype(BF16), ctx_len)
        else:
            if not last:
                raise NotImplementedError("expert channel mixer on a non-final layer")
            wr_hi = w_router[i].astype(BF16)
            wr_lo = (w_router[i] - wr_hi.astype(F32)).astype(BF16)
            pad = lambda w: jnp.pad(w, ((0, 0), (0, LANES - N_EXPERTS)))
            wr = jnp.concatenate([pad(wr_hi), pad(wr_lo)], axis=1)
            x_mid, h_ffn, gates = _out_proj_route(xa, o_sgu, o_ssd, o_mla, mod, wo, row(g_ffn[layer]), wr,
                                                  ctx_len)
            out = _moe_final(x_mid, h_ffn, gates, mod, w_gate_e[i].astype(BF16), w_up_e[i].astype(BF16),
                             w_down_e[i].astype(BF16), row(g_final))
    return out
```

```python
import functools

import jax
import jax.numpy as jnp
from jax import lax
from jax.experimental import pallas as pl
from jax.experimental.pallas import tpu as pltpu

F32 = jnp.float32
BF16 = jnp.bfloat16

D_MODEL = 1024
EPS = 1e-6
N_MOD = 6
GRID_W = 64
CHUNK = 128

SGU_HEADS = 4
SGU_HEAD_DIM = 64
SGU_WIDTH = 256

SSD_HEADS = 6
SSD_HEAD_DIM = 64
SSD_WIDTH = 384
SSD_GROUPS = 2
SSD_HPG = 3
SSD_STATE = 128
SSD_CONV = 5
SSD_CONV_CH = 896
SSD_GW = SSD_HPG * SSD_HEAD_DIM
DT_REP = 16

MLA_HEADS = 6
MLA_NOPE = 64
MLA_ROPE = 32
MLA_V = 64
MLA_WIDTH = 384
Q_LORA = 384
KV_LORA = 256
ROPE_THETA = 10000.0
MLA_SCALE = (MLA_NOPE + MLA_ROPE) ** -0.5
LOG2_E = 1.4426950408889634
HEAD_SLOT = 128
ROPE_LANE0 = MLA_NOPE
ROPE_HALF = MLA_ROPE // 4
ROPE_PARTNER_STARTS = (ROPE_HALF, 0, 3 * ROPE_HALF, 2 * ROPE_HALF)
assert HEAD_SLOT == ROPE_LANE0 + 2 * MLA_ROPE

D_FF = 2816
N_EXPERTS = 8
D_FF_EXPERT = 1408

LANES = 128
MOD_ROWS = 16
CTX_MOD_ROW = 8

IN_COLS = (("sgu", 0, 512), ("z", 512, 896), ("xbc", 896, 1792), ("dt", 1792, 1920), ("cq", 1920, 2304),
           ("ckv", 2304, 2560), ("kr", 2560, 2688))
IN_PAD_WIDTH = 2688

VMEM_LIMIT = 56 * 1024 * 1024


def _sigmoid(x):
    return 1.0 / (1.0 + jnp.exp(-x))


def _silu(x):
    return x * _sigmoid(x)


def _rms(x):
    return x * lax.rsqrt(jnp.mean(x * x, axis=-1, keepdims=True) + EPS)


def _aligned(v, m):
    return v if isinstance(v, int) else pl.multiple_of(v, m)


def _row_select(mod_ref, b, tile, rows, ctx_len, col0):
    row = tile * rows + lax.broadcasted_iota(jnp.int32, (rows, 1), 0)
    is_ctx = row < ctx_len
    mb = mod_ref[pl.ds(b, 1), col0:col0 + D_MODEL]
    mc = mod_ref[CTX_MOD_ROW:CTX_MOD_ROW + 1, col0:col0 + D_MODEL]
    return jnp.where(is_ctx, mc, mb)


def _mod_kernel(cond_ref, w_ref, b_ref, o_ref):
    s = _silu(cond_ref[...])
    s_hi = s.astype(BF16)
    s_lo = (s - s_hi.astype(F32)).astype(BF16)
    w = w_ref[0]
    w_hi = w.astype(BF16)
    w_lo = (w - w_hi.astype(F32)).astype(BF16)
    by_hi = jnp.dot(jnp.concatenate([s_hi, s_lo], axis=0), w_hi, preferred_element_type=F32)
    o_ref[0] = (by_hi[:MOD_ROWS] + by_hi[MOD_ROWS:] + jnp.dot(s_hi, w_lo, preferred_element_type=F32)
                + b_ref[0])


def _mod_table(cond, w_mod, b_mod):
    n_layers, d, width = w_mod.shape
    cb = 1536
    return pl.pallas_call(
        _mod_kernel,
        out_shape=jax.ShapeDtypeStruct((n_layers, MOD_ROWS, width), F32),
        grid=(n_layers, width // cb),
        in_specs=[pl.BlockSpec((MOD_ROWS, d), lambda l, j: (0, 0)),
                  pl.BlockSpec((1, d, cb), lambda l, j: (l, 0, j)),
                  pl.BlockSpec((1, 1, cb), lambda l, j: (l, 0, j))],
        out_specs=pl.BlockSpec((1, MOD_ROWS, cb), lambda l, j: (l, 0, j)),
        compiler_params=pltpu.CompilerParams(dimension_semantics=("arbitrary", "arbitrary"),
                                             vmem_limit_bytes=VMEM_LIMIT),
        name="mod_table",
    )(cond, w_mod, b_mod.reshape(n_layers, 1, width))


def _token_specs(tokens, ctx_len, rows, d):
    if not isinstance(tokens, tuple):
        return [pl.BlockSpec((1, rows, d), lambda b, t: (b, t, 0))], [tokens]
    ctx, lat = tokens
    n_sub = rows // ctx_len
    assert rows == n_sub * ctx_len
    piece = lambda j: pl.BlockSpec((1, ctx_len, d), lambda b, t: (b, jnp.maximum(n_sub * t + j - 1, 0), 0))
    return ([pl.BlockSpec((1, ctx_len, d), lambda b, t: (b, 0, 0))] + [piece(j) for j in range(n_sub)],
            [ctx] + [lat] * n_sub)


def _tokens_tile(refs, t):
    if len(refs) == 1:
        return refs[0][0]
    first = jnp.where(t == 0, refs[0][0], refs[1][0])
    return first if len(refs) == 2 else jnp.concatenate([first] + [r[0] for r in refs[2:]], axis=0)


IN_RAW_DT = (1792, 1804)
IN_RAW_KR = IN_RAW_DT[1] + Q_LORA + KV_LORA
IN_RAW_WIDTH = IN_RAW_KR + MLA_ROPE


def _pad_in_weight(wraw_ref, w_ref):
    dt0, dt1 = IN_RAW_DT
    for r0 in range(0, D_MODEL, LANES):
        rows = slice(r0, r0 + LANES)
        z = lambda n: jnp.zeros((LANES, n), F32)
        dt = wraw_ref[0, rows, dt0:dt1]
        gap = z(DT_REP - (dt1 - dt0))
        w_ref[rows, :] = jnp.concatenate(
            [wraw_ref[0, rows, 0:dt0], dt, gap, dt, gap, dt, gap, z(LANES - 3 * DT_REP),
             wraw_ref[0, rows, dt1:IN_RAW_KR], z(ROPE_LANE0), wraw_ref[0, rows, IN_RAW_KR:IN_RAW_WIDTH]]
            + [wraw_ref[0, rows, IN_RAW_KR + c0:IN_RAW_KR + c0 + ROPE_HALF] for c0 in ROPE_PARTNER_STARTS],
            axis=1).astype(BF16)


def _sgu_chunks(p_sgu, w_ref, bias_ref, g_ref, beta_ref, o_ref):
    lane = lax.broadcasted_iota(jnp.int32, (1, SGU_WIDTH), 1)
    head_of_lane = lane // SGU_HEAD_DIM
    w = w_ref[...]
    bias = bias_ref[...]
    c0 = 0.7978845608028654
    for c in range(p_sgu.shape[0] // CHUNK):
        p = p_sgu[c * CHUNK:(c + 1) * CHUNK, :]
        ge = 0.5 * p * (1.0 + jnp.tanh(c0 * (p + 0.044715 * (p * p * p))))
        u = ge[:, :SGU_WIDTH]
        v = ge[:, SGU_WIDTH:]
        mu = jnp.mean(v, axis=-1, keepdims=True)
        vc = v - mu
        vn = vc * lax.rsqrt(jnp.mean(vc * vc, axis=-1, keepdims=True) + EPS) * g_ref[...] + beta_ref[...]
        stacked = jnp.concatenate(
            [jnp.where(head_of_lane == h, vn, 0.0).astype(BF16) for h in range(SGU_HEADS)], axis=0)
        mixed = jnp.dot(w, stacked, preferred_element_type=F32) + bias
        o_ref[0, c * CHUNK:(c + 1) * CHUNK, :] = (u * mixed).astype(BF16)


def _in_kernel(*refs, rows, ctx_len, n_tok_refs):
    tok_refs = refs[:n_tok_refs]
    mod_ref, g_ref, wraw_ref, sgu_w_ref, sgu_bias_ref, sgu_g_ref, sgu_beta_ref, *out_refs, w_ref = refs[n_tok_refs:]
    b = pl.program_id(0)
    t = pl.program_id(1)

    @pl.when(jnp.logical_and(b == 0, t == 0))
    def _():
        _pad_in_weight(wraw_ref, w_ref)

    xn = _rms(_tokens_tile(tok_refs, t)) * g_ref[...]
    shift = _row_select(mod_ref, b, t, rows, ctx_len, 0)
    scale = _row_select(mod_ref, b, t, rows, ctx_len, D_MODEL)
    h = (xn * (1.0 + scale) + shift).astype(BF16)
    p = jnp.dot(h, w_ref[...], preferred_element_type=F32)
    (_, s0, s1), other_cols = IN_COLS[0], IN_COLS[1:]
    _sgu_chunks(p[:, s0:s1], sgu_w_ref, sgu_bias_ref, sgu_g_ref, sgu_beta_ref, out_refs[0])
    for o_ref, (_, c0, c1) in zip(out_refs[1:], other_cols):
        o_ref[0] = p[:, c0:c1]


def _in_proj(tokens, t_all, mod, g, w_in, layer, sgu_w, sgu_bias, sgu_g, sgu_beta, ctx_len, rows):
    tok_specs, tok_args = _token_specs(tokens, ctx_len, rows, D_MODEL)
    bsz = tok_args[0].shape[0]
    assert w_in.shape[1:] == (D_MODEL, IN_RAW_WIDTH) and rows % CHUNK == 0
    kern = functools.partial(_in_kernel, rows=rows, ctx_len=ctx_len, n_tok_refs=len(tok_args))
    const = lambda b, t: (0, 0)
    widths = [(SGU_WIDTH, BF16)] + [(c1 - c0, F32) for _, c0, c1 in IN_COLS[1:]]
    return pl.pallas_call(
        kern,
        out_shape=[jax.ShapeDtypeStruct((bsz, t_all, w), dt) for w, dt in widths],
        grid=(bsz, t_all // rows),
        in_specs=tok_specs + [pl.BlockSpec(mod.shape, const), pl.BlockSpec((1, D_MODEL), const),
                              pl.BlockSpec((1,) + w_in.shape[1:], lambda b, t: (layer, 0, 0)),
                              pl.BlockSpec(sgu_w.shape, const), pl.BlockSpec(sgu_bias.shape, const),
                              pl.BlockSpec(sgu_g.shape, const), pl.BlockSpec(sgu_beta.shape, const)],
        out_specs=[pl.BlockSpec((1, rows, w), lambda b, t: (b, t, 0)) for w, _ in widths],
        scratch_shapes=[pltpu.VMEM((D_MODEL, IN_PAD_WIDTH), BF16)],
        compiler_params=pltpu.CompilerParams(dimension_semantics=("arbitrary", "arbitrary"),
                                             vmem_limit_bytes=VMEM_LIMIT),
        name="in_proj",
    )(*tok_args, mod, g, w_in, sgu_w, sgu_bias, sgu_g, sgu_beta)


def _ssd_kernel(z_ref, xbc_ref, dt_ref, cw_ref, cb_ref, dtb_ref, alog_ref, skip_ref, g_ref, o_ref,
                xc_ref, yf_ref, yb_ref, st_ref, *, n_blk, n_ctx_blk):
    def conv_block(blk, seg_start, seg_end):
        r0 = _aligned(blk * CHUNK, CHUNK)
        for cg in range(SSD_CONV_CH // LANES):
            cols = slice(cg * LANES, (cg + 1) * LANES)
            zeros = jnp.zeros((8, LANES), F32)
            top = zeros if seg_start else xbc_ref[0, pl.ds(_aligned(r0 - 8, 8), 8), cols]
            bot = zeros if seg_end else xbc_ref[0, pl.ds(_aligned(r0 + CHUNK, 8), 8), cols]
            xw = jnp.concatenate([top, xbc_ref[0, pl.ds(r0, CHUNK), cols], bot], axis=0)
            acc = cb_ref[:, cols] + cw_ref[0:1, cols] * xw[6:6 + CHUNK]
            for k in range(1, SSD_CONV):
                acc = acc + cw_ref[k:k + 1, cols] * xw[6 + k:6 + k + CHUNK]
            xc_ref[pl.ds(r0, CHUNK), cols] = _silu(acc)

    n_interior = n_blk - n_ctx_blk - 2
    assert n_interior >= 0 and n_interior % 2 == 0
    for blk in sorted(set(list(range(n_ctx_blk)) + [n_ctx_blk, n_blk - 1])):
        conv_block(blk, blk == 0 or blk == n_ctx_blk, blk == n_ctx_blk - 1 or blk == n_blk - 1)

    li = lax.broadcasted_iota(jnp.int32, (CHUNK, CHUNK), 0)
    si = lax.broadcasted_iota(jnp.int32, (CHUNK, CHUNK), 1)
    a_neg = -jnp.exp(alog_ref[...])

    def chunk_step(c, direction):
        r0 = pl.multiple_of(c * CHUNK, CHUNK)
        rows = pl.ds(r0, CHUNK)
        mask = (si <= li) if direction == 0 else (si >= li)
        tri = jnp.where(mask, 1.0, 0.0).astype(BF16)
        dtr = dt_ref[0, rows, :] + dtb_ref[...]
        dt = jnp.maximum(dtr, 0.0) + jnp.log1p(jnp.exp(-jnp.abs(dtr)))
        adt = dt * a_neg
        p1 = adt.astype(BF16)
        r1 = adt - p1.astype(F32)
        p2 = r1.astype(BF16)
        p3 = (r1 - p2.astype(F32)).astype(BF16)
        parts = jnp.dot(tri, jnp.concatenate([p1, p2, p3], axis=1), preferred_element_type=F32)
        acs = parts[:, :LANES] + parts[:, LANES:2 * LANES] + parts[:, 2 * LANES:]
        end = CHUNK - 1 if direction == 0 else 0
        tot = acs[end:end + 1, :]
        to_end_dt = jnp.exp(tot - acs) * dt
        chunk_decay = jnp.exp(tot)
        per_head_rows = jnp.where(si < DT_REP, acs, jnp.where(si < 2 * DT_REP, dt, to_end_dt)).T
        first_half = si < SSD_HEAD_DIM
        cbs, bm_ts, y_offs = [], [], []
        state = st_ref[direction]
        state16 = state.astype(BF16)
        for g in range(SSD_GROUPS):
            bm = xc_ref[rows, SSD_WIDTH + g * SSD_STATE:SSD_WIDTH + (g + 1) * SSD_STATE]
            cm = xc_ref[rows, SSD_WIDTH + SSD_GROUPS * SSD_STATE + g * SSD_STATE:
                        SSD_WIDTH + SSD_GROUPS * SSD_STATE + (g + 1) * SSD_STATE]
            cm16 = cm.astype(BF16)
            cbs.append(lax.dot_general(cm16, bm.astype(BF16), (((1,), (1,)), ((), ())),
                                       preferred_element_type=F32))
            bm_ts.append(bm.T)
            y_offs.append(jnp.dot(cm16, state16, preferred_element_type=F32))
        y_pairs, state_pairs = [], []
        for k in range(SSD_HEADS // 2):
            pair = slice(k * LANES, (k + 1) * LANES)
            xs16 = xc_ref[rows, pair].astype(BF16)
            y_diag, contrib, exp_a, dec, y_off = [], [], [], [], []
            for hh in (2 * k, 2 * k + 1):
                g = hh // SSD_HPG
                col = direction * SSD_HEADS + hh
                a_col = jnp.broadcast_to(acs[:, col:col + 1], (CHUNK, CHUNK))
                seg = a_col - per_head_rows[col:col + 1, :]
                decay = jnp.exp(jnp.where(mask, seg, -1e30))
                m = (cbs[g] * decay * per_head_rows[DT_REP + col:DT_REP + col + 1, :]).astype(BF16)
                y_diag.append(jnp.dot(m, xs16, preferred_element_type=F32))
                exp_a.append(jnp.exp(a_col))
                y_off.append(y_offs[g][:, pair])
                lhs = (bm_ts[g] * per_head_rows[2 * DT_REP + col:2 * DT_REP + col + 1, :]).astype(BF16)
                contrib.append(jnp.dot(lhs, xs16, preferred_element_type=F32))
                dec.append(jnp.broadcast_to(chunk_decay[:, col:col + 1], (SSD_STATE, LANES)))
            pick = lambda ab: jnp.where(first_half, ab[0], ab[1])
            y_pairs.append(pick(y_diag) + pick(exp_a) * pick(y_off))
            state_pairs.append(pick(dec) * state[:, pair] + pick(contrib))
        st_ref[direction] = jnp.concatenate(state_pairs, axis=1)
        y = jnp.concatenate(y_pairs, axis=1)
        if direction == 0:
            yf_ref[rows, :] = skip_ref[...] * xc_ref[rows, 0:SSD_WIDTH] + y
        else:
            yb_ref[rows, :] = y

    def scan_body(step, carry):
        chunk_step(step, 0)
        chunk_step(jnp.where(step < n_ctx_blk, n_ctx_blk - 1 - step, n_blk - 1 - (step - n_ctx_blk)), 1)
        return carry

    def scan_conv_body(step, carry):
        scan_body(step, carry)
        conv_block(step + 1, False, False)
        conv_block(n_blk - 2 - (step - n_ctx_blk), False, False)
        return carry

    def out_chunk(c):
        rows = pl.ds(_aligned(c * CHUNK, CHUNK), CHUNK)
        gated = (yf_ref[rows, :] + yb_ref[rows, :]) * _silu(z_ref[0, rows, :])
        o_ref[0, rows, :] = (_rms(gated) * g_ref[...]).astype(BF16)

    def scan_out_body(step, carry):
        scan_body(step, carry)
        out_chunk(step - 1)
        out_chunk(n_blk - 1 - (step - 1 - n_ctx_blk))
        return carry

    st_ref[...] = jnp.zeros(st_ref.shape, F32)
    conv_steps_end = n_ctx_blk + n_interior // 2
    meet = (n_blk + n_ctx_blk) // 2
    lax.fori_loop(0, n_ctx_blk, scan_body, 0)
    lax.fori_loop(n_ctx_blk, conv_steps_end, scan_conv_body, 0)
    lax.fori_loop(conv_steps_end, meet + 1, scan_body, 0)
    lax.fori_loop(meet + 1, n_blk, scan_out_body, 0)
    for c in list(range(n_ctx_blk)) + [n_ctx_blk, n_blk - 1]:
        out_chunk(c)


def _ssd(p_z, p_xbc, p_dt, conv_w, conv_b, dt_bias, a_log, skip, g, ctx_len):
    bsz, t_all, _ = p_z.shape
    kern = functools.partial(_ssd_kernel, n_blk=t_all // CHUNK, n_ctx_blk=ctx_len // CHUNK)
    const = lambda b: (0, 0)
    per_b = lambda w: pl.BlockSpec((1, t_all, w), lambda b: (b, 0, 0))
    return pl.pallas_call(
        kern,
        out_shape=jax.ShapeDtypeStruct((bsz, t_all, SSD_WIDTH), BF16),
        grid=(bsz,),
        in_specs=[per_b(SSD_WIDTH), per_b(SSD_CONV_CH), per_b(LANES),
                  pl.BlockSpec(conv_w.shape, const), pl.BlockSpec(conv_b.shape, const),
                  pl.BlockSpec(dt_bias.shape, const), pl.BlockSpec(a_log.shape, const),
                  pl.BlockSpec(skip.shape, const), pl.BlockSpec(g.shape, const)],
        out_specs=per_b(SSD_WIDTH),
        scratch_shapes=[pltpu.VMEM((t_all, SSD_CONV_CH), F32), pltpu.VMEM((t_all, SSD_WIDTH), F32),
                        pltpu.VMEM((t_all, SSD_WIDTH), F32),
                        pltpu.VMEM((2, SSD_STATE, SSD_WIDTH), F32)],
        compiler_params=pltpu.CompilerParams(dimension_semantics=("arbitrary",), vmem_limit_bytes=VMEM_LIMIT),
        name="ssd",
    )(p_z, p_xbc, p_dt, conv_w, conv_b, dt_bias, a_log, skip, g)


def _mla_proj_kernel(pq_ref, pkv_ref, pkr_ref, gq_ref, wq_ref, gkv_ref, wk_ref, wv_ref, cos_ref, sin_ref,
                     q_ref, k_ref, v_ref):
    cos = cos_ref[...]
    sin = sin_ref[...]

    def rope(t):
        return t * cos + pltpu.roll(t, LANES - MLA_ROPE, 1) * sin

    qn = (_rms(pq_ref[0]) * gq_ref[...]).astype(BF16)
    q = jnp.dot(qn, wq_ref[...], preferred_element_type=F32)
    kvn = (_rms(pkv_ref[0]) * gkv_ref[...]).astype(BF16)
    k = jnp.dot(kvn, wk_ref[...], preferred_element_type=F32)
    v_t = lax.dot_general(wv_ref[...], kvn, (((1,), (1,)), ((), ())),
                          preferred_element_type=F32).astype(BF16)
    kr = rope(pkr_ref[0])
    for h in range(MLA_HEADS):
        slot = slice(h * HEAD_SLOT, (h + 1) * HEAD_SLOT)
        q_ref[0, h] = (rope(q[:, slot]) * (MLA_SCALE * LOG2_E)).astype(BF16)
        k_ref[0, h] = (k[:, slot] + kr).astype(BF16)
        v_ref[0, h] = v_t[h * MLA_V:(h + 1) * MLA_V, :]


def _mla_proj(p_q, p_kv, p_kr, g_q, w_q, g_kv, w_k, w_v, cos, sin, rows):
    bsz, t_all, _ = p_q.shape
    const = lambda b, t: (0, 0)
    tok = lambda w: pl.BlockSpec((1, rows, w), lambda b, t: (b, t, 0))
    tab = pl.BlockSpec((rows, LANES), lambda b, t: (t, 0))
    out = jax.ShapeDtypeStruct((bsz, MLA_HEADS, t_all, HEAD_SLOT), BF16)
    out_vt = jax.ShapeDtypeStruct((bsz, MLA_HEADS, MLA_V, t_all), BF16)
    head_tok = pl.BlockSpec((1, MLA_HEADS, rows, HEAD_SLOT), lambda b, t: (b, 0, t, 0))
    return pl.pallas_call(
        _mla_proj_kernel,
        out_shape=[out, out, out_vt],
        grid=(bsz, t_all // rows),
        in_specs=[tok(Q_LORA), tok(KV_LORA), tok(LANES),
                  pl.BlockSpec(g_q.shape, const), pl.BlockSpec(w_q.shape, const),
                  pl.BlockSpec(g_kv.shape, const), pl.BlockSpec(w_k.shape, const), pl.BlockSpec(w_v.shape, const),
                  tab, tab],
        out_specs=[head_tok, head_tok, pl.BlockSpec((1, MLA_HEADS, MLA_V, rows), lambda b, t: (b, 0, 0, t))],
        compiler_params=pltpu.CompilerParams(dimension_semantics=("arbitrary", "arbitrary"),
                                             vmem_limit_bytes=VMEM_LIMIT),
        name="mla_proj",
    )(p_q, p_kv, p_kr, g_q, w_q, g_kv, w_k, w_v, cos, sin)


ATTN_Q = 512
ATTN_BUFS = 2
ATTN_PARTS = 3


def _attn_kernel(*refs, n_q_blocks):
    q_refs = refs[:n_q_blocks]
    k_ref, vt_ref, o_ref, *bufs, ot_ref = refs[n_q_blocks:]
    s_bufs, m_bufs = bufs[:ATTN_BUFS], bufs[ATTN_BUFS:]
    def scores(h, s_ref, m_ref):
        qh = jnp.concatenate([r[0, h] for r in q_refs], axis=0)
        s_t = lax.dot_general(k_ref[0, h], qh, (((1,), (1,)), ((), ())), preferred_element_type=F32)
        s_ref[...] = s_t
        m_ref[...] = jnp.broadcast_to(jnp.max(s_t, axis=0, keepdims=True), m_ref.shape)

    scores(0, s_bufs[0], m_bufs[0])

    n_keys = k_ref.shape[2]
    n_parts = ATTN_PARTS if n_keys % (ATTN_PARTS * LANES) == 0 else 1
    part = n_keys // n_parts

    def stage(h_new, h_old, new_s, new_m, old_s, old_m):
        qh = jnp.concatenate([r[0, h_new] for r in q_refs], axis=0)
        m_old = old_m[0:1, :]
        m_new, denom, acc = None, None, None
        for c in range(n_parts):
            keys = slice(c * part, (c + 1) * part)
            s_c = lax.dot_general(k_ref[0, h_new, keys, :], qh, (((1,), (1,)), ((), ())),
                                  preferred_element_type=F32)
            new_s[keys, :] = s_c
            m_c = jnp.max(s_c, axis=0, keepdims=True)
            m_new = m_c if m_new is None else jnp.maximum(m_new, m_c)
            p_c = jnp.exp2(old_s[keys, :] - m_old)
            d_c = jnp.sum(p_c, axis=0, keepdims=True)
            o_c = jnp.dot(vt_ref[0, h_old, :, keys], p_c.astype(BF16), preferred_element_type=F32)
            denom, acc = (d_c, o_c) if denom is None else (denom + d_c, acc + o_c)
        new_m[...] = jnp.broadcast_to(m_new, new_m.shape)
        ot_ref[pl.ds(pl.multiple_of(h_old * MLA_V, MLA_V), MLA_V), :] = acc / denom

    def head_group(j, carry):
        h0 = ATTN_BUFS * j
        for i in range(ATTN_BUFS):
            nxt = (i + 1) % ATTN_BUFS
            stage(jnp.minimum(h0 + i + 1, MLA_HEADS - 1), h0 + i, s_bufs[nxt], m_bufs[nxt], s_bufs[i], m_bufs[i])
        return carry

    lax.fori_loop(0, MLA_HEADS // ATTN_BUFS, head_group, 0)
    o_ref[0] = ot_ref[...].T.astype(BF16)


def _attention(q, k, vt, ctx_len, latent):
    bsz, n_heads, t_all, width = q.shape
    if latent:
        n_q_blocks = ATTN_Q // ctx_len
        n_rows, n_keys, tq = t_all - ctx_len, t_all, ATTN_Q
        q_specs = [pl.BlockSpec((1, n_heads, ctx_len, width),
                                functools.partial(lambda b, i, j: (b, 0, n_q_blocks * i + 1 + j, 0), j=j))
                   for j in range(n_q_blocks)]
    else:
        n_q_blocks = 1
        n_rows, n_keys, tq = ctx_len, ctx_len, ctx_len
        q_specs = [pl.BlockSpec((1, n_heads, ctx_len, width), lambda b, i: (b, 0, 0, 0))]
    assert n_rows % tq == 0
    return pl.pallas_call(
        functools.partial(_attn_kernel, n_q_blocks=n_q_blocks),
        out_shape=jax.ShapeDtypeStruct((bsz, n_rows, MLA_WIDTH), BF16),
        grid=(bsz, n_rows // tq),
        in_specs=q_specs + [pl.BlockSpec((1, n_heads, n_keys, width), lambda b, i: (b, 0, 0, 0)),
                            pl.BlockSpec((1, n_heads, MLA_V, n_keys), lambda b, i: (b, 0, 0, 0))],
        out_specs=pl.BlockSpec((1, tq, MLA_WIDTH), lambda b, i: (b, i, 0)),
        scratch_shapes=([pltpu.VMEM((n_keys, tq), F32)] * ATTN_BUFS + [pltpu.VMEM((8, tq), F32)] * ATTN_BUFS
                        + [pltpu.VMEM((MLA_WIDTH, tq), F32)]),
        compiler_params=pltpu.CompilerParams(dimension_semantics=("arbitrary", "arbitrary"),
                                             vmem_limit_bytes=VMEM_LIMIT),
        name="attention",
    )(*([q] * n_q_blocks), k, vt)


def _mix_residual(x_in, o1, o2, o3, mod_ref, wo_ref, g_ref, b, t, rows, ctx_len):
    r1, r2 = SGU_WIDTH, SGU_WIDTH + SSD_WIDTH
    mix = (jnp.dot(o1, wo_ref[0:r1, :], preferred_element_type=F32)
           + jnp.dot(o2, wo_ref[r1:r2, :], preferred_element_type=F32)
           + jnp.dot(o3, wo_ref[r2:, :], preferred_element_type=F32))
    x = x_in + _row_select(mod_ref, b, t, rows, ctx_len, 2 * D_MODEL) * mix
    shift = _row_select(mod_ref, b, t, rows, ctx_len, 3 * D_MODEL)
    scale = _row_select(mod_ref, b, t, rows, ctx_len, 4 * D_MODEL)
    return x, _rms(x) * g_ref[...] * (1.0 + scale) + shift


OUT_ROWS = 512


def _out_kernel(*refs, rows, ctx_len, n_sub):
    x_refs, o1_refs, o2_refs = refs[:n_sub], refs[n_sub:2 * n_sub], refs[2 * n_sub:3 * n_sub]
    o3_ref, mod_ref, wo_ref, g_ref, wr_ref, xo_ref, h_ref, gate_ref = refs[3 * n_sub:]
    pieces = lambda rs: rs[0][0] if n_sub == 1 else jnp.concatenate([r[0] for r in rs], axis=0)
    b = pl.program_id(0)
    t = pl.program_id(1) + 1
    x, h = _mix_residual(pieces(x_refs), pieces(o1_refs), pieces(o2_refs), o3_ref[0], mod_ref, wo_ref, g_ref, b, t,
                         rows, ctx_len)
    xo_ref[0] = x
    h_ref[0] = h.astype(BF16)
    h_hi = h.astype(BF16)
    h_lo = (h - h_hi.astype(F32)).astype(BF16)
    by_hi = jnp.dot(h_hi, wr_ref[...], preferred_element_type=F32)
    logits = by_hi[:, :LANES] + by_hi[:, LANES:] + jnp.dot(h_lo, wr_ref[:, :LANES], preferred_element_type=F32)
    lane = lax.broadcasted_iota(jnp.int32, logits.shape, 1)
    lane_f = lane.astype(F32)
    lg = jnp.where(lane < N_EXPERTS, logits, -jnp.inf)
    m1 = jnp.max(lg, axis=-1, keepdims=True)
    i1 = jnp.min(jnp.where(lg == m1, lane_f, float(LANES)), axis=-1, keepdims=True)
    lg2 = jnp.where(lane_f == i1, -jnp.inf, lg)
    m2 = jnp.max(lg2, axis=-1, keepdims=True)
    i2 = jnp.min(jnp.where(lg2 == m2, lane_f, float(LANES)), axis=-1, keepdims=True)
    e2 = jnp.exp(m2 - m1)
    w_top = 1.0 / (1.0 + e2)
    gate_ref[0] = jnp.where(lane_f == i1, w_top, 0.0) + jnp.where(lane_f == i2, e2 * w_top, 0.0)


def _out_proj_route(xa, o_sgu, o_ssd, o_mla, mod, wo, g, w_router, ctx_len):
    bsz, t_all, d = xa.shape
    n_out = t_all - ctx_len
    rows = OUT_ROWS if n_out % OUT_ROWS == 0 and OUT_ROWS % ctx_len == 0 else ctx_len
    n_sub = rows // ctx_len
    kern = functools.partial(_out_kernel, rows=rows, ctx_len=ctx_len, n_sub=n_sub)
    const = lambda b, t: (0, 0)
    piece = lambda w, j: pl.BlockSpec((1, ctx_len, w), lambda b, t: (b, n_sub * t + 1 + j, 0))
    tok = lambda w: [piece(w, j) for j in range(n_sub)]
    out_tok = lambda w: pl.BlockSpec((1, rows, w), lambda b, t: (b, t, 0))
    return pl.pallas_call(
        kern,
        out_shape=[jax.ShapeDtypeStruct((bsz, n_out, d), F32), jax.ShapeDtypeStruct((bsz, n_out, d), BF16),
                   jax.ShapeDtypeStruct((bsz, n_out, LANES), F32)],
        grid=(bsz, n_out // rows),
        in_specs=tok(d) + tok(SGU_WIDTH) + tok(SSD_WIDTH) + [
            out_tok(MLA_WIDTH), pl.BlockSpec(mod.shape, const), pl.BlockSpec(wo.shape, const),
            pl.BlockSpec(g.shape, const), pl.BlockSpec(w_router.shape, const)],
        out_specs=[out_tok(d), out_tok(d), out_tok(LANES)],
        compiler_params=pltpu.CompilerParams(dimension_semantics=("arbitrary", "arbitrary"),
                                             vmem_limit_bytes=VMEM_LIMIT),
        name="out_proj",
    )(*([xa] * n_sub + [o_sgu] * n_sub + [o_ssd] * n_sub), o_mla, mod, wo, g, w_router)


FF_SPLITS = ((0, 1536), (1536, D_FF))


MIX_SUB = 3


def _mix_ffn_kernel(*refs, ctx_len, n_tok_refs, n_sub):
    ctx_ref, *x_refs = refs[:n_tok_refs]
    o1_ref, o2_ref, *rest = refs[n_tok_refs:]
    o3_refs = rest[:n_sub]
    o3c_ref, mod_ref, wo_ref, g_ref, wg_ref, wu_ref, wd_ref, o_ref = rest[n_sub:]
    b = pl.program_id(0)
    t = pl.program_id(1)
    for j in range(n_sub):
        sub = slice(j * ctx_len, (j + 1) * ctx_len)
        piece = n_sub * t + j
        x_in, o3 = x_refs[j][0], o3_refs[j][0]
        if j == 0:
            x_in = jnp.where(t == 0, ctx_ref[0], x_in)
            o3 = jnp.where(t == 0, o3c_ref[0], o3)
        x, h = _mix_residual(x_in, o1_ref[0, sub, :], o2_ref[0, sub, :], o3, mod_ref, wo_ref, g_ref, b, piece,
                             ctx_len, ctx_len)
        h = h.astype(BF16)
        acc = None
        for c0, c1 in FF_SPLITS:
            gate = jnp.dot(h, wg_ref[:, c0:c1], preferred_element_type=F32)
            up = jnp.dot(h, wu_ref[:, c0:c1], preferred_element_type=F32)
            part = jnp.dot((_silu(gate) * up).astype(BF16), wd_ref[c0:c1, :], preferred_element_type=F32)
            acc = part if acc is None else acc + part
        o_ref[0, sub, :] = x + _row_select(mod_ref, b, piece, ctx_len, ctx_len, 5 * D_MODEL) * acc


def _mix_ffn(tokens, o_sgu, o_ssd, o_mla, o_mla_ctx, mod, wo, g, wg, wu, wd, ctx_len):
    bsz, t_all, _ = o_sgu.shape
    d = D_MODEL
    assert isinstance(tokens, tuple)
    n_sub = MIX_SUB if t_all % (MIX_SUB * ctx_len) == 0 else 1
    rows = n_sub * ctx_len
    tok_specs, tok_args = _token_specs(tokens, ctx_len, rows, d)
    kern = functools.partial(_mix_ffn_kernel, ctx_len=ctx_len, n_tok_refs=len(tok_args), n_sub=n_sub)
    const = lambda b, t: (0, 0)
    tok = lambda w: pl.BlockSpec((1, rows, w), lambda b, t: (b, t, 0))
    lat_piece = lambda j: pl.BlockSpec((1, ctx_len, MLA_WIDTH),
                                       lambda b, t: (b, jnp.maximum(n_sub * t + j - 1, 0), 0))
    whole = lambda a: pl.BlockSpec(a.shape, const, pipeline_mode=pl.Buffered(1))
    return pl.pallas_call(
        kern,
        out_shape=jax.ShapeDtypeStruct((bsz, t_all, d), F32),
        grid=(bsz, t_all // rows),
        in_specs=tok_specs + [tok(SGU_WIDTH), tok(SSD_WIDTH)] + [lat_piece(j) for j in range(n_sub)] + [
            pl.BlockSpec((1, ctx_len, MLA_WIDTH), lambda b, t: (b, 0, 0)),
            pl.BlockSpec(mod.shape, const), whole(wo), pl.BlockSpec(g.shape, const), whole(wg), whole(wu), whole(wd)],
        out_specs=tok(d),
        compiler_params=pltpu.CompilerParams(dimension_semantics=("arbitrary", "arbitrary"),
                                             vmem_limit_bytes=VMEM_LIMIT),
        name="mix_ffn",
    )(*tok_args, o_sgu, o_ssd, *([o_mla] * n_sub), o_mla_ctx, mod, wo, g, wg, wu, wd)


MOE_TILE = 1024
MOE_BLOCK = 144
MOE_SLOTS = 256


def _plan_kernel(gate_ref, rank_ref, rank_rows_ref, count_ref):
    rows = gate_ref.shape[0]
    routed = gate_ref[...] > 0.0
    ti = lax.broadcasted_iota(jnp.int32, (rows, rows), 0)
    tj = lax.broadcasted_iota(jnp.int32, (rows, rows), 1)
    earlier = jnp.where(tj < ti, 1.0, 0.0).astype(BF16)
    ones = jnp.where(routed, 1.0, 0.0)
    before = jnp.dot(earlier, ones.astype(BF16), preferred_element_type=F32)
    rank = jnp.where(routed, before, -1.0)
    rank_ref[...] = rank
    rank_rows_ref[0] = rank.T[0:N_EXPERTS, :]
    count_ref[0] = jnp.broadcast_to(jnp.sum(ones, axis=0, keepdims=True), (8, LANES)).astype(jnp.int32)


def _route_plan(gates):
    n_tok = gates.shape[0]
    n_tiles = n_tok // MOE_TILE
    return pl.pallas_call(
        _plan_kernel,
        out_shape=[jax.ShapeDtypeStruct((n_tok, LANES), F32),
                   jax.ShapeDtypeStruct((n_tiles, N_EXPERTS, MOE_TILE), F32),
                   jax.ShapeDtypeStruct((n_tiles, 8, LANES), jnp.int32)],
        grid=(n_tiles,),
        in_specs=[pl.BlockSpec((MOE_TILE, LANES), lambda i: (i, 0))],
        out_specs=[pl.BlockSpec((MOE_TILE, LANES), lambda i: (i, 0)),
                   pl.BlockSpec((1, N_EXPERTS, MOE_TILE), lambda i: (i, 0, 0)),
                   pl.BlockSpec((1, 8, LANES), lambda i: (i, 0, 0))],
        compiler_params=pltpu.CompilerParams(dimension_semantics=("arbitrary",), vmem_limit_bytes=VMEM_LIMIT),
        name="route_plan",
    )(gates)


def _moe_kernel(count_ref, x_ref, h_ref, gate_ref, rank_ref, rank_rows_ref, mod_ref, wg_ref, wu_ref, wd_ref, gf_ref,
                o_ref, *, tiles_per_batch):
    i = pl.program_id(0)
    e = pl.program_id(1)
    rows = h_ref.shape[0]
    d = h_ref.shape[1]
    lane = lax.broadcasted_iota(jnp.int32, (rows, LANES), 1)
    ge = jnp.sum(jnp.where(lane == e, gate_ref[...], 0.0), axis=-1, keepdims=True)
    rank_lanes = jnp.broadcast_to(jnp.sum(jnp.where(lane == e, rank_ref[...], 0.0), axis=-1, keepdims=True),
                                  (rows, MOE_SLOTS))
    rank_row = rank_rows_ref[0, pl.ds(e, 1), :]

    @pl.when(e == 0)
    def _():
        o_ref[...] = jnp.zeros(o_ref.shape, F32)

    slot_sub = lax.broadcasted_iota(jnp.int32, (MOE_BLOCK, rows), 0)
    slot_lane = lax.broadcasted_iota(jnp.int32, (1, MOE_SLOTS), 1)

    def block(j, carry):
        lo = j * MOE_BLOCK
        pick = jnp.where(rank_row == (slot_sub + lo).astype(F32), 1.0, 0.0).astype(BF16)
        hg = jnp.dot(pick, h_ref[...], preferred_element_type=F32).astype(BF16)
        gate = jnp.dot(hg, wg_ref[0], preferred_element_type=F32)
        up = jnp.dot(hg, wu_ref[0], preferred_element_type=F32)
        y = jnp.dot((_silu(gate) * up).astype(BF16), wd_ref[0], preferred_element_type=F32).astype(BF16)
        y = jnp.concatenate([y, jnp.zeros((MOE_SLOTS - MOE_BLOCK, d), BF16)], axis=0)
        target = jnp.where(slot_lane < MOE_BLOCK, slot_lane + lo, -2).astype(F32)
        place = jnp.where(rank_lanes == target, 1.0, 0.0).astype(BF16)
        o_ref[...] += ge * jnp.dot(place, y, preferred_element_type=F32)
        return carry

    n_blocks = (count_ref[i * N_EXPERTS + e] + (MOE_BLOCK - 1)) // MOE_BLOCK
    lax.fori_loop(0, n_blocks, block, 0)

    @pl.when(e == N_EXPERTS - 1)
    def _():
        gate5 = mod_ref[pl.ds(i // tiles_per_batch, 1), 5 * D_MODEL:6 * D_MODEL]
        o_ref[...] = _rms(x_ref[...] + gate5 * o_ref[...]) * gf_ref[...]


def _moe_final(x, h, gates, mod, wg, wu, wd, g_final):
    bsz, n, d = x.shape
    assert n % MOE_TILE == 0
    n_tok = bsz * n
    gates = gates.reshape(n_tok, LANES)
    rank, rank_rows, counts = _route_plan(gates)
    counts = counts[:, 0, :N_EXPERTS].reshape(-1)
    kern = functools.partial(_moe_kernel, tiles_per_batch=n // MOE_TILE)
    const = lambda i, e, bnd: (0, 0)
    tok = lambda w: pl.BlockSpec((MOE_TILE, w), lambda i, e, bnd: (i, 0))
    per_expert = lambda a, b: pl.BlockSpec((1, a, b), lambda i, e, bnd: (e, 0, 0))
    out = pl.pallas_call(
        kern,
        out_shape=jax.ShapeDtypeStruct((n_tok, d), F32),
        grid_spec=pltpu.PrefetchScalarGridSpec(
            num_scalar_prefetch=1,
            grid=(n_tok // MOE_TILE, N_EXPERTS),
            in_specs=[tok(d), tok(d), tok(LANES), tok(LANES),
                      pl.BlockSpec((1, N_EXPERTS, MOE_TILE), lambda i, e, bnd: (i, 0, 0)),
                      pl.BlockSpec(mod.shape, const),
                      per_expert(d, D_FF_EXPERT), per_expert(d, D_FF_EXPERT), per_expert(D_FF_EXPERT, d),
                      pl.BlockSpec(g_final.shape, const)],
            out_specs=tok(d)),
        compiler_params=pltpu.CompilerParams(dimension_semantics=("arbitrary", "arbitrary"),
                                             vmem_limit_bytes=VMEM_LIMIT),
        name="moe_final",
    )(counts, x.reshape(n_tok, d), h.reshape(n_tok, d), gates, rank, rank_rows, mod, wg, wu, wd, g_final)
    return out.reshape(bsz, n, d)


def _dt_lanes(v):
    gap = jnp.zeros(v.shape[:-1] + (DT_REP - v.shape[-1],), v.dtype)
    tail = jnp.zeros(v.shape[:-1] + (LANES - 3 * DT_REP,), v.dtype)
    return jnp.concatenate([v, gap, v, gap, v, gap, tail], axis=-1)


def _head_slots(w, per_head, take0, take1, slot=HEAD_SLOT):
    k = w.shape[0]
    w = w.reshape(k, MLA_HEADS, per_head)[:, :, take0:take1]
    w = jnp.pad(w, ((0, 0), (0, 0), (0, slot - (take1 - take0))))
    return w.reshape(k, MLA_HEADS * slot).astype(BF16)


def _q_slots(w):
    k = w.shape[0]
    w = w.reshape(k, MLA_HEADS, MLA_NOPE + MLA_ROPE)
    r = w[:, :, MLA_NOPE:]
    partner = jnp.concatenate([r[:, :, c0:c0 + ROPE_HALF] for c0 in ROPE_PARTNER_STARTS], axis=2)
    return jnp.concatenate([w, partner], axis=2).reshape(k, MLA_HEADS * HEAD_SLOT).astype(BF16)


def _rope_tables(n_latent, ctx_len):
    t = jnp.arange(n_latent)
    pos = jnp.stack([(t // GRID_W).astype(F32), (t % GRID_W).astype(F32)], axis=1)
    n_freq = MLA_ROPE // 4
    inv_freq = ROPE_THETA ** (-jnp.arange(n_freq, dtype=F32) / n_freq)
    ang = pos[:, :, None] * inv_freq
    cos, sin = jnp.cos(ang), jnp.sin(ang)
    cos_r = jnp.stack([cos, cos], axis=2).reshape(n_latent, MLA_ROPE)
    sin_r = jnp.stack([-sin, sin], axis=2).reshape(n_latent, MLA_ROPE)

    def slot(r, fill):
        rope = jnp.concatenate([jnp.full((ctx_len, MLA_ROPE), fill, F32), r], axis=0)
        rows = ctx_len + n_latent
        return jnp.concatenate([jnp.full((rows, ROPE_LANE0), fill, F32), rope, jnp.zeros((rows, MLA_ROPE), F32)],
                               axis=1)

    return slot(cos_r, 1.0), slot(sin_r, 0.0)


def kernel(x, c, ctx, c_ctx, w_mod, b_mod, g_mix, w_in, w_sgu, b_sgu, g_sgu, beta_sgu, conv_w, conv_b, dt_bias,
           a_log, d_skip, g_ssd, g_q, w_uq, g_kv, w_ukv, w_out, g_ffn, w_gate, w_up, w_down, w_router, w_gate_e,
           w_up_e, w_down_e, g_final):
    bsz, n, d = x.shape
    ctx_len = ctx.shape[1]
    depth = w_in.shape[0]
    assert bsz <= CTX_MOD_ROW and ctx_len % CHUNK == 0 and n % CHUNK == 0

    cond = jnp.concatenate([c, jnp.zeros((CTX_MOD_ROW - bsz, d), F32), c_ctx[None],
                            jnp.zeros((MOD_ROWS - CTX_MOD_ROW - 1, d), F32)], axis=0)
    mod_all = _mod_table(cond, w_mod, b_mod)
    cos, sin = _rope_tables(n, ctx_len)
    xa = (ctx, x)
    t_all = ctx_len + n
    rows_in = 768 if t_all % 768 == 0 else ctx_len

    out = None
    for layer in range(depth):
        last = layer == depth - 1
        mod = mod_all[layer]
        row = lambda v: v.reshape(1, -1)
        w_cat = jnp.transpose(w_sgu[layer], (1, 0, 2)).reshape(CHUNK, SGU_HEADS * CHUNK).astype(BF16)
        bias = jnp.repeat(b_sgu[layer].T, SGU_HEAD_DIM, axis=1)
        o_sgu, p_z, p_xbc, p_dt, p_q, p_kv, p_kr = _in_proj(
            xa, t_all, mod, row(g_mix[layer]), w_in, layer, w_cat, bias, row(g_sgu[layer]), row(beta_sgu[layer]),
            ctx_len, rows_in)
        o_ssd = _ssd(p_z, p_xbc, p_dt, conv_w[layer], row(conv_b[layer]), _dt_lanes(dt_bias[layer].reshape(1, -1)),
                     _dt_lanes(a_log[layer].reshape(1, -1)), row(jnp.repeat(d_skip[layer], SSD_HEAD_DIM)), row(g_ssd[layer]),
                     ctx_len)
        per_kv = MLA_NOPE + MLA_V
        q, k, v = _mla_proj(p_q, p_kv, p_kr, row(g_q[layer]), _q_slots(w_uq[layer]),
                            row(g_kv[layer]), _head_slots(w_ukv[layer], per_kv, 0, MLA_NOPE),
                            _head_slots(w_ukv[layer], per_kv, MLA_NOPE, per_kv, MLA_V).T, cos, sin, rows_in)
        o_mla = _attention(q, k, v, ctx_len, latent=True)
        o_mla_ctx = None if last else _attention(q, k, v, ctx_len, latent=False)
        wo = w_out[layer].astype(BF16)
        i = layer // 2
        if layer % 2 == 0:
            if last:
                raise NotImplementedError("final dense channel mixer")
            xa = _mix_ffn(xa, o_sgu, o_ssd, o_mla, o_mla_ctx, mod, wo, row(g_ffn[layer]),
                          w_gate[i].astype(BF16), w_up[i].astype(BF16), w_down[i].astype(BF16), ctx_len)
        else:
            if not last:
                raise NotImplementedError("expert channel mixer on a non-final layer")
            wr_hi = w_router[i].astype(BF16)
            wr_lo = (w_router[i] - wr_hi.astype(F32)).astype(BF16)
            pad = lambda w: jnp.pad(w, ((0, 0), (0, LANES - N_EXPERTS)))
            wr = jnp.concatenate([pad(wr_hi), pad(wr_lo)], axis=1)
            x_mid, h_ffn, gates = _out_proj_route(xa, o_sgu, o_ssd, o_mla, mod, wo, row(g_ffn[layer]), wr,
                                                  ctx_len)
            out = _moe_final(x_mid, h_ffn, gates, mod, w_gate_e[i].astype(BF16), w_up_e[i].astype(BF16),
                             w_down_e[i].astype(BF16), row(g_final))
    return out
```
